```python
import math
import jax, jax.numpy as jnp
from jax import lax
import numpy as np

D_MODEL = 2048
BATCH = 1
SEQ = 8192
DEPTH = 4

ATTN_HEADS = 8
ATTN_HEAD_DIM = 128
REL_BUCKETS = 32
REL_MAX_DISTANCE = 2048

SSD_HEADS = 16
SSD_HEAD_DIM = 64
SSD_INNER = SSD_HEADS * SSD_HEAD_DIM
SSD_GROUPS = 2
SSD_STATE = 128
SSD_CONV = 4
SSD_CHUNK = 256
SSD_CONV_DIM = SSD_INNER + 2 * SSD_GROUPS * SSD_STATE

MOBA_HEADS = ATTN_HEADS
MOBA_WIDTH = MOBA_HEADS * ATTN_HEAD_DIM
MOBA_BLOCK = 256
MOBA_TOPK = 3
MOBA_Q_CHUNK = 32

NSA_Q_HEADS = ATTN_HEADS
NSA_KV_HEADS = 2
NSA_GROUP = NSA_Q_HEADS // NSA_KV_HEADS
NSA_WIDTH = NSA_Q_HEADS * ATTN_HEAD_DIM
NSA_KV_WIDTH = NSA_KV_HEADS * ATTN_HEAD_DIM
NSA_CMP_BLOCK = 32
NSA_CMP_STRIDE = 16
NSA_SEL_BLOCK = 64
NSA_SEL_TOPN = 16
NSA_WINDOW = 512
NSA_Q_CHUNK = 128

RET_HEADS = 8
RET_KEY_DIM = 128
RET_VALUE_DIM = 128
RET_WIDTH = RET_HEADS * RET_VALUE_DIM
RET_CHUNK = 256
ROPE_BASE = 10000.0

FFN_DIM = 5632
FFN_CONV = 3

MIX_WIDTH = SSD_INNER + MOBA_WIDTH
EVEN_IN = SSD_INNER + SSD_CONV_DIM + SSD_HEADS + 3 * MOBA_WIDTH
ODD_IN = NSA_WIDTH + 6 * NSA_KV_WIDTH + 3 * NSA_Q_HEADS + 4 * RET_WIDTH
SEQ_MULTIPLE = 256
LN_EPS = 1e-5
NEG = -1e30
DEEPNORM_ALPHA = (2 * DEPTH) ** 0.25
DEEPNORM_BETA = (8 * DEPTH) ** -0.25

kernel_name = "hybrid_ssd_moba_nsa_retention_trunk"


def layer_norm(x, g, b):
    xf = x.astype(jnp.float32)
    mu = xf.mean(-1, keepdims=True)
    var = jnp.square(xf - mu).mean(-1, keepdims=True)
    return ((xf - mu) * lax.rsqrt(var + LN_EPS) * g + b).astype(x.dtype)


def causal_depthwise_conv(x, w, b):
    k, c = w.shape
    y = lax.conv_general_dilated(x, w[:, None, :].astype(x.dtype), window_strides=(1,),
                                 padding=[(k - 1, 0)], dimension_numbers=('NWC', 'WIO', 'NWC'),
                                 feature_group_count=c)
    return y + b


def rel_bucket(dist):
    n = jnp.maximum(dist, 0)
    max_exact = REL_BUCKETS // 2
    nf = jnp.maximum(n, max_exact).astype(jnp.float32)
    large = max_exact + (jnp.log(nf / max_exact) / math.log(REL_MAX_DISTANCE / max_exact)
                         * (REL_BUCKETS - max_exact)).astype(jnp.int32)
    large = jnp.minimum(large, REL_BUCKETS - 1)
    return jnp.where(n < max_exact, n, large)


def head_bias(table, bucket):
    return jax.vmap(lambda tab_h, b_h: tab_h[b_h], in_axes=(1, 1), out_axes=1)(table.astype(jnp.float32), bucket)


def masked_softmax(logits, mask):
    p = jax.nn.softmax(jnp.where(mask, logits, NEG), axis=-1)
    return jnp.where(mask, p, 0.0)


def segsum_decay(a_cum):
    L = a_cum.shape[-1]
    diff = a_cum[..., :, None] - a_cum[..., None, :]
    causal = jnp.tril(jnp.ones((L, L), bool))
    return jnp.where(causal, jnp.exp(jnp.where(causal, diff, 0.0)), 0.0)


def ssd_mixer(xbc_raw, z, dt_raw, conv_w, conv_b, dt_bias, a_log, d_skip, norm_w):
    f32 = jnp.float32
    B_, T, _ = z.shape
    G, hpg, P, N, L = SSD_GROUPS, SSD_HEADS // SSD_GROUPS, SSD_HEAD_DIM, SSD_STATE, SSD_CHUNK
    nc = T // L
    xbc = jax.nn.silu(causal_depthwise_conv(xbc_raw, conv_w, conv_b)).astype(f32)
    xs, bs, cs = jnp.split(xbc, [SSD_INNER, SSD_INNER + G * N], axis=-1)
    xs = xs.reshape(B_, nc, L, SSD_HEADS, P)
    bs = bs.reshape(B_, nc, L, G, N)
    cs = cs.reshape(B_, nc, L, G, N)
    dt = jax.nn.softplus(dt_raw.astype(f32) + dt_bias.astype(f32))
    a = -jnp.exp(a_log.astype(f32))
    dt_c = dt.reshape(B_, nc, L, SSD_HEADS)
    a_cum = jnp.cumsum((dt_c * a).transpose(0, 3, 1, 2), axis=-1)
    xdt_g = (xs * dt_c[..., None]).reshape(B_, nc, L, G, hpg, P)
    cb = jnp.einsum('bclgn,bcsgn->bgcls', cs, bs)
    decay = segsum_decay(a_cum).reshape(B_, G, hpg, nc, L, L)
    y_diag = jnp.einsum('bgcls,bghcls,bcsghp->bclghp', cb, decay, xdt_g)
    decay_to_end = jnp.exp(a_cum[..., -1:] - a_cum).reshape(B_, G, hpg, nc, L)
    states = jnp.einsum('bcsgn,bghcs,bcsghp->bcghpn', bs, decay_to_end, xdt_g)
    chunk_decay = jnp.exp(a_cum[..., -1]).reshape(B_, G, hpg, nc).transpose(3, 0, 1, 2)

    def step(h, inp):
        s, dec = inp
        return h * dec[..., None, None] + s, h

    h0 = jnp.zeros((B_, G, hpg, P, N), f32)
    _, prev = lax.scan(step, h0, (states.transpose(1, 0, 2, 3, 4, 5), chunk_decay))
    prev = prev.transpose(1, 0, 2, 3, 4, 5)
    decay_in = jnp.exp(a_cum).reshape(B_, G, hpg, nc, L)
    y_off = jnp.einsum('bclgn,bcghpn,bghcl->bclghp', cs, prev, decay_in)
    y = (y_diag + y_off).reshape(B_, T, SSD_HEADS, P) + xs.reshape(B_, T, SSD_HEADS, P) * d_skip.astype(f32)[:, None]
    yg = (y.reshape(B_, T, SSD_INNER) * jax.nn.silu(z.astype(f32))).reshape(B_, T, G, SSD_INNER // G)
    yg = yg * lax.rsqrt(jnp.mean(jnp.square(yg), axis=-1, keepdims=True) + LN_EPS)
    return (yg.reshape(B_, T, SSD_INNER) * norm_w).astype(z.dtype)


def moba_attention(q, k, v, rel_bias):
    f32 = jnp.float32
    B_, T, H, Dh = q.shape
    nb = T // MOBA_BLOCK
    scale = Dh ** -0.5
    pos = jnp.arange(T)
    kb = k.reshape(B_, nb, MOBA_BLOCK, H, Dh)
    vb = v.reshape(B_, nb, MOBA_BLOCK, H, Dh)
    kmean = kb.astype(f32).mean(2)
    gate = jnp.einsum('bthd,bnhd->bhtn', q.astype(f32), kmean)
    cur = pos // MOBA_BLOCK
    past = jnp.arange(nb)[None, :] < cur[:, None]
    sc, idx = lax.top_k(jnp.where(past, gate, -jnp.inf), min(MOBA_TOPK, nb))
    blocks = jnp.concatenate([idx, jnp.broadcast_to(cur[None, None, :, None], (B_, H, T, 1)).astype(idx.dtype)], -1)
    ok = jnp.concatenate([sc > -jnp.inf, jnp.ones((B_, H, T, 1), bool)], -1)
    nsel = blocks.shape[-1]
    Cq = MOBA_Q_CHUNK
    nqc = T // Cq
    q_c = q.astype(f32).reshape(B_, nqc, Cq, H, Dh).transpose(1, 0, 3, 2, 4)
    blk_c = blocks.reshape(B_, H, nqc, Cq, nsel).transpose(2, 0, 1, 3, 4)
    ok_c = ok.reshape(B_, H, nqc, Cq, nsel).transpose(2, 0, 1, 3, 4)
    pos_c = pos.reshape(nqc, Cq)
    kbh = kb.transpose(0, 3, 1, 2, 4)
    vbh = vb.transpose(0, 3, 1, 2, 4)
    bi = jnp.arange(B_)[:, None, None, None]
    hi = jnp.arange(H)[None, :, None, None]
    koff = jnp.arange(MOBA_BLOCK)

    def chunk(args):
        q_i, blk_i, ok_i, qp = args
        kg = kbh[bi, hi, blk_i]
        vg = vbh[bi, hi, blk_i].reshape(B_, H, Cq, nsel * MOBA_BLOCK, Dh)
        kpos = blk_i[..., None] * MOBA_BLOCK + koff
        qpb = qp[None, None, :, None, None]
        logits = jnp.einsum('bhqd,bhqskd->bhqsk', q_i, kg) * scale + head_bias(rel_bias, rel_bucket(qpb - kpos))
        mask = ok_i[..., None] & (kpos <= qpb)
        p = masked_softmax(logits.reshape(B_, H, Cq, -1), mask.reshape(B_, H, Cq, -1))
        return jnp.einsum('bhqk,bhqkd->bhqd', p, vg)

    out = lax.map(chunk, (q_c, blk_c, ok_c, pos_c))
    return out.transpose(1, 0, 3, 2, 4).reshape(B_, T, H * Dh).astype(q.dtype)


def nsa_attention(q, kc, vc, ks, vs, kw, vw, gates, cmp_pe, cmp_w1, cmp_w2, rel_bias):
    f32 = jnp.float32
    B_, T, Hq, Dh = q.shape
    Hkv, G = NSA_KV_HEADS, NSA_GROUP
    scale = Dh ** -0.5
    pos = jnp.arange(T)
    qg = q.astype(f32).reshape(B_, T, Hkv, G, Dh)

    ncb = (T - NSA_CMP_BLOCK) // NSA_CMP_STRIDE + 1
    cmp_start = jnp.arange(ncb) * NSA_CMP_STRIDE
    cidx = cmp_start[:, None] + jnp.arange(NSA_CMP_BLOCK)[None, :]

    def compress(x, pe, w1, w2):
        xb = x[:, cidx] + pe[:, None, :]
        xb = xb.transpose(0, 1, 3, 2, 4).reshape(B_, ncb, Hkv, NSA_CMP_BLOCK * Dh)
        return jax.nn.gelu(xb @ w1) @ w2

    k_cmp = compress(kc, cmp_pe[0], cmp_w1[0], cmp_w2[0])
    v_cmp = compress(vc, cmp_pe[1], cmp_w1[1], cmp_w2[1])
    cmp_end = cmp_start + NSA_CMP_BLOCK - 1
    cmp_mask = cmp_end[None, :] <= pos[:, None]
    bias_c = rel_bias.astype(f32)[rel_bucket(pos[:, None] - cmp_end[None, :])].transpose(2, 0, 1).reshape(Hkv, G, T, ncb)
    logits_c = jnp.einsum('btkgd,bnkd->bkgtn', qg, k_cmp) * scale + bias_c
    p_cmp = masked_softmax(logits_c, cmp_mask)
    o_cmp = jnp.einsum('bkgtn,bnkd->btkgd', p_cmp, v_cmp)

    ns = T // NSA_SEL_BLOCK
    sel_start = jnp.arange(ns) * NSA_SEL_BLOCK
    overlap = ((cmp_start[:, None] < sel_start[None, :] + NSA_SEL_BLOCK)
               & (cmp_start[:, None] + NSA_CMP_BLOCK > sel_start[None, :])).astype(f32)
    imp = jnp.einsum('bkgtn,nj->bktj', p_cmp, overlap)
    cur = pos // NSA_SEL_BLOCK
    j = jnp.arange(ns)[None, :]
    forced = (j == 0) | (j == cur[:, None]) | (j == cur[:, None] - 1)
    score = jnp.where(forced, jnp.inf, jnp.where(j <= cur[:, None], imp, -jnp.inf))
    n_sel = min(NSA_SEL_TOPN, ns)
    sc, sidx = lax.top_k(score, n_sel)
    sok = sc > -jnp.inf
    ksb = ks.reshape(B_, ns, NSA_SEL_BLOCK, Hkv, Dh).transpose(0, 3, 1, 2, 4)
    vsb = vs.reshape(B_, ns, NSA_SEL_BLOCK, Hkv, Dh).transpose(0, 3, 1, 2, 4)
    Cq = NSA_Q_CHUNK
    nqc = T // Cq
    q_c = qg.reshape(B_, nqc, Cq, Hkv, G, Dh).transpose(1, 0, 3, 4, 2, 5)
    idx_c = sidx.reshape(B_, Hkv, nqc, Cq, n_sel).transpose(2, 0, 1, 3, 4)
    ok_c = sok.reshape(B_, Hkv, nqc, Cq, n_sel).transpose(2, 0, 1, 3, 4)
    pos_c = pos.reshape(nqc, Cq)
    bi = jnp.arange(B_)[:, None, None, None]
    hi = jnp.arange(Hkv)[None, :, None, None]
    koff = jnp.arange(NSA_SEL_BLOCK)

    def sel_chunk(args):
        q_i, idx_i, ok_i, qp = args
        kg = ksb[bi, hi, idx_i].reshape(B_, Hkv, Cq, n_sel * NSA_SEL_BLOCK, Dh)
        vg = vsb[bi, hi, idx_i].reshape(B_, Hkv, Cq, n_sel * NSA_SEL_BLOCK, Dh)
        kpos = idx_i[..., None] * NSA_SEL_BLOCK + koff
        qpb = qp[None, None, :, None, None]
        mask = (ok_i[..., None] & (kpos <= qpb)).reshape(B_, Hkv, 1, Cq, -1)
        bucket = rel_bucket(qpb - kpos).reshape(B_, Hkv, Cq, -1)
        bias = head_bias(rel_bias, jnp.repeat(bucket, G, axis=1)).reshape(B_, Hkv, G, Cq, -1)
        logits = jnp.einsum('bkgqd,bkqsd->bkgqs', q_i, kg) * scale + bias
        p = masked_softmax(logits, mask)
        return jnp.einsum('bkgqs,bkqsd->bkgqd', p, vg)

    o_sel = lax.map(sel_chunk, (q_c, idx_c, ok_c, pos_c))
    o_sel = o_sel.transpose(1, 0, 4, 2, 3, 5).reshape(B_, T, Hkv, G, Dh)

    WB = NSA_Q_CHUNK
    nwb = T // WB
    span = WB + NSA_WINDOW
    widx = jnp.arange(nwb)[:, None] * WB + jnp.arange(span)[None, :]
    kwin = jnp.pad(kw, ((0, 0), (NSA_WINDOW, 0), (0, 0), (0, 0)))[:, widx]
    vwin = jnp.pad(vw, ((0, 0), (NSA_WINDOW, 0), (0, 0), (0, 0)))[:, widx]
    rel = jnp.arange(WB)[:, None] + NSA_WINDOW - jnp.arange(span)[None, :]
    wmask = ((rel >= 0) & (rel < NSA_WINDOW))[None] & ((widx - NSA_WINDOW) >= 0)[:, None, :]
    bias_w = rel_bias.astype(f32)[rel_bucket(rel)].transpose(2, 0, 1).reshape(Hkv, G, 1, WB, span)
    qw = qg.reshape(B_, nwb, WB, Hkv, G, Dh)
    logits_w = jnp.einsum('bnqkgd,bnskd->bkgnqs', qw, kwin) * scale + bias_w
    p_w = masked_softmax(logits_w, wmask)
    o_win = jnp.einsum('bkgnqs,bnskd->bnqkgd', p_w, vwin).reshape(B_, T, Hkv, G, Dh)

    g = jax.nn.sigmoid(gates.astype(f32)).reshape(B_, T, Hkv, G, 3)
    o = g[..., 0:1] * o_cmp + g[..., 1:2] * o_sel + g[..., 2:3] * o_win
    return o.reshape(B_, T, Hq * Dh).astype(q.dtype)


def rotary(x, pos):
    D = x.shape[-1]
    inv = 1.0 / (ROPE_BASE ** (jnp.arange(0, D, 2, dtype=jnp.float32) / D))
    ang = pos[:, None].astype(jnp.float32) * inv[None, :]
    cos = jnp.cos(ang)[None, :, None, :]
    sin = jnp.sin(ang)[None, :, None, :]
    x1, x2 = x[..., 0::2], x[..., 1::2]
    return jnp.stack([x1 * cos - x2 * sin, x1 * sin + x2 * cos], axis=-1).reshape(x.shape)


def retention(q, k, v, gate):
    f32 = jnp.float32
    B_, T, H, Dk = q.shape
    Dv = v.shape[-1]
    C = RET_CHUNK
    nc = T // C
    pos = jnp.arange(T)
    q = rotary(q.astype(f32), pos).reshape(B_, nc, C, H, Dk)
    k = (rotary(k.astype(f32), pos) * Dk ** -0.5).reshape(B_, nc, C, H, Dk)
    v = v.astype(f32).reshape(B_, nc, C, H, Dv)
    log_g = jnp.log(1.0 - 2.0 ** (-5.0 - jnp.arange(H, dtype=f32)))
    i = jnp.arange(C)
    rel = i[:, None] - i[None, :]
    inner_decay = jnp.where(rel >= 0, jnp.exp(log_g[:, None, None] * jnp.maximum(rel, 0)), 0.0)
    inner = jnp.einsum('bcihd,bcjhd->bchij', q, k) * inner_decay
    y_in = jnp.einsum('bchij,bcjhe->bcihe', inner, v)
    k_decay = jnp.exp(log_g[None, :] * (C - 1 - i)[:, None])
    kv = jnp.einsum('bcjhd,jh,bcjhe->bchde', k, k_decay, v)
    chunk_decay = jnp.exp(log_g * C)[None, :, None, None]

    def step(R, kv_c):
        return R * chunk_decay + kv_c, R

    _, prev = lax.scan(step, jnp.zeros((B_, H, Dk, Dv), f32), kv.transpose(1, 0, 2, 3, 4))
    prev = prev.transpose(1, 0, 2, 3, 4)
    q_decay = jnp.exp(log_g[None, :] * (i + 1)[:, None])
    y_cross = jnp.einsum('bcihd,ih,bchde->bcihe', q, q_decay, prev)
    y = (y_in + y_cross).reshape(B_, T, H, Dv)
    mu = y.mean(-1, keepdims=True)
    var = jnp.square(y - mu).mean(-1, keepdims=True)
    y = ((y - mu) * lax.rsqrt(var + LN_EPS)).reshape(B_, T, H * Dv)
    return (jax.nn.silu(gate.astype(f32)) * y).astype(gate.dtype)


def even_mixer(h, w_in, conv_w, conv_b, dt_bias, a_log, d_skip, norm_w, w_out, rel_bias):
    B_, T, _ = h.shape
    p = h @ w_in
    sizes = [SSD_INNER, SSD_CONV_DIM, SSD_HEADS, MOBA_WIDTH, MOBA_WIDTH, MOBA_WIDTH]
    z, xbc, dt, q, k, v = jnp.split(p, np.cumsum(sizes)[:-1].tolist(), axis=-1)
    y_ssd = ssd_mixer(xbc, z, dt, conv_w, conv_b, dt_bias, a_log, d_skip, norm_w)
    shp = (B_, T, MOBA_HEADS, ATTN_HEAD_DIM)
    y_moba = moba_attention(q.reshape(shp), k.reshape(shp), v.reshape(shp), rel_bias)
    return jnp.concatenate([y_ssd, y_moba], axis=-1) @ w_out


def odd_mixer(h, w_in, cmp_pe, cmp_w1, cmp_w2, w_out, rel_bias):
    B_, T, _ = h.shape
    p = h @ w_in
    sizes = [NSA_WIDTH] + [NSA_KV_WIDTH] * 6 + [3 * NSA_Q_HEADS] + [RET_WIDTH] * 4
    q, kc, vc, ks, vs, kw, vw, gates, rq, rk, rv, rg = jnp.split(p, np.cumsum(sizes)[:-1].tolist(), axis=-1)
    kvs = (B_, T, NSA_KV_HEADS, ATTN_HEAD_DIM)
    y_nsa = nsa_attention(q.reshape(B_, T, NSA_Q_HEADS, ATTN_HEAD_DIM), kc.reshape(kvs), vc.reshape(kvs),
                          ks.reshape(kvs), vs.reshape(kvs), kw.reshape(kvs), vw.reshape(kvs), gates,
                          cmp_pe, cmp_w1, cmp_w2, rel_bias)
    y_ret = retention(rq.reshape(B_, T, RET_HEADS, RET_KEY_DIM), rk.reshape(B_, T, RET_HEADS, RET_KEY_DIM),
                      rv.reshape(B_, T, RET_HEADS, RET_VALUE_DIM), rg)
    return jnp.concatenate([y_nsa, y_ret], axis=-1) @ w_out


def conv_ffn(x, w_up, conv_w, conv_b, w_down):
    a, u = jnp.split(x @ w_up, 2, axis=-1)
    a = causal_depthwise_conv(a, conv_w, conv_b)
    return (jax.nn.silu(a) * u) @ w_down


def setup_inputs(seed: int = 0) -> dict:
    key = jax.random.key(seed)
    ks = jax.random.split(key, 24)
    n_even = (DEPTH + 1) // 2
    n_odd = DEPTH // 2
    f32 = jnp.float32

    def normal(k, shape, scale):
        return jax.random.normal(k, shape, f32) * scale

    dt0 = jnp.exp(jax.random.uniform(ks[5], (n_even, SSD_HEADS), f32, math.log(1e-3), math.log(1e-1)))
    return {
        "x": normal(ks[0], (BATCH, SEQ, D_MODEL), 1.0),
        "rel_bias": normal(ks[1], (REL_BUCKETS, ATTN_HEADS), 0.2),
        "ev_w_in": normal(ks[2], (n_even, D_MODEL, EVEN_IN), D_MODEL ** -0.5),
        "ev_conv_w": normal(ks[3], (n_even, SSD_CONV, SSD_CONV_DIM), SSD_CONV ** -0.5),
        "ev_conv_b": normal(ks[4], (n_even, SSD_CONV_DIM), 0.02),
        "ev_dt_bias": dt0 + jnp.log(-jnp.expm1(-dt0)),
        "ev_a_log": jnp.log(jax.random.uniform(ks[6], (n_even, SSD_HEADS), f32, 1.0, 16.0)),
        "ev_d_skip": 1.0 + normal(ks[7], (n_even, SSD_HEADS), 0.1),
        "ev_norm_w": 1.0 + normal(ks[8], (n_even, SSD_INNER), 0.02),
        "ev_w_out": normal(ks[9], (n_even, MIX_WIDTH, D_MODEL), MIX_WIDTH ** -0.5 * DEEPNORM_BETA),
        "od_w_in": normal(ks[10], (n_odd, D_MODEL, ODD_IN), D_MODEL ** -0.5),
        "od_cmp_pe": normal(ks[11], (n_odd, 2, NSA_CMP_BLOCK, ATTN_HEAD_DIM), 0.02),
        "od_cmp_w1": normal(ks[12], (n_odd, 2, NSA_CMP_BLOCK * ATTN_HEAD_DIM, ATTN_HEAD_DIM), (NSA_CMP_BLOCK * ATTN_HEAD_DIM) ** -0.5),
        "od_cmp_w2": normal(ks[13], (n_odd, 2, ATTN_HEAD_DIM, ATTN_HEAD_DIM), ATTN_HEAD_DIM ** -0.5),
        "od_w_out": normal(ks[14], (n_odd, MIX_WIDTH, D_MODEL), MIX_WIDTH ** -0.5 * DEEPNORM_BETA),
        "ffn_w_up": normal(ks[15], (DEPTH, D_MODEL, 2 * FFN_DIM), D_MODEL ** -0.5),
        "ffn_conv_w": normal(ks[16], (DEPTH, FFN_CONV, FFN_DIM), FFN_CONV ** -0.5),
        "ffn_conv_b": normal(ks[17], (DEPTH, FFN_DIM), 0.02),
        "ffn_w_down": normal(ks[18], (DEPTH, FFN_DIM, D_MODEL), FFN_DIM ** -0.5 * DEEPNORM_BETA),
        "ln_g": 1.0 + normal(ks[19], (DEPTH, 2, D_MODEL), 0.02),
        "ln_b": normal(ks[20], (DEPTH, 2, D_MODEL), 0.02),
    }


def reference(x, rel_bias, ev_w_in, ev_conv_w, ev_conv_b, ev_dt_bias, ev_a_log, ev_d_skip, ev_norm_w, ev_w_out,
              od_w_in, od_cmp_pe, od_cmp_w1, od_cmp_w2, od_w_out,
              ffn_w_up, ffn_conv_w, ffn_conv_b, ffn_w_down, ln_g, ln_b):
    B_, T0, _ = x.shape
    T = ((T0 + SEQ_MULTIPLE - 1) // SEQ_MULTIPLE) * SEQ_MULTIPLE
    h = jnp.pad(x, ((0, 0), (0, T - T0), (0, 0)))
    for layer in range(DEPTH):
        i = layer // 2
        if layer % 2 == 0:
            mix = even_mixer(h, ev_w_in[i], ev_conv_w[i], ev_conv_b[i], ev_dt_bias[i], ev_a_log[i],
                             ev_d_skip[i], ev_norm_w[i], ev_w_out[i], rel_bias)
        else:
            mix = odd_mixer(h, od_w_in[i], od_cmp_pe[i], od_cmp_w1[i], od_cmp_w2[i], od_w_out[i], rel_bias)
        h = layer_norm(DEEPNORM_ALPHA * h + mix, ln_g[layer, 0], ln_b[layer, 0])
        ffn = conv_ffn(h, ffn_w_up[layer], ffn_conv_w[layer], ffn_conv_b[layer], ffn_w_down[layer])
        h = layer_norm(DEEPNORM_ALPHA * h + ffn, ln_g[layer, 1], ln_b[layer, 1])
    return h[:, :T0]
```

```python
import functools
import math

import numpy as np
import jax
import jax.numpy as jnp
from jax import lax
from jax.experimental import pallas as pl
from jax.experimental.pallas import tpu as pltpu

F32 = jnp.float32
BF16 = jnp.bfloat16
HIGHEST = lax.Precision.HIGHEST

D_MODEL = 2048
DEPTH = 4
HEAD_DIM = 128
ATTN_HEADS = 8
REL_BUCKETS = 32
REL_MAX_DISTANCE = 2048
SSD_HEADS = 16
SSD_HEAD_DIM = 64
SSD_INNER = 1024
SSD_GROUPS = 2
SSD_STATE = 128
SSD_CONV = 4
MOBA_BLOCK = 256
MOBA_TOPK = 3
NSA_KV_HEADS = 2
NSA_GROUP = 4
NSA_CMP_BLOCK = 32
NSA_CMP_STRIDE = 16
NSA_SEL_BLOCK = 64
NSA_SEL_TOPN = 16
NSA_WINDOW = 512
RET_HEADS = 8
ROPE_BASE = 10000.0
FFN_DIM = 5632
FFN_CONV = 3
LN_EPS = 1e-5
NEG = -1e30
BIG = 3e38
DEEPNORM_ALPHA = (2 * DEPTH) ** 0.25

CHUNK = 256
LANES = 128
VMEM_LIMIT = 56 * 1024 * 1024


def _params(*sem):
    return pltpu.CompilerParams(dimension_semantics=sem, vmem_limit_bytes=VMEM_LIMIT)


def _silu(x):
    return x * (1.0 / (1.0 + jnp.exp(-x)))


def _dot(a, b):
    return jnp.dot(a, b, preferred_element_type=F32)


def _dot_nt(a, b):
    return lax.dot_general(a, b, (((1,), (1,)), ((), ())), preferred_element_type=F32)


def _dot_tn(a, b):
    return lax.dot_general(a, b, (((0,), (0,)), ((), ())), preferred_element_type=F32)


def _shift_rows(a, carry, k):
    rolled = pltpu.roll(a, k, 0)
    head = jnp.where(lax.broadcasted_iota(jnp.int32, carry.shape, 0) < k,
                     pltpu.roll(carry, k, 0), rolled[:8])
    return jnp.concatenate([head, rolled[8:]], axis=0)


def _mm_kernel(x_ref, w_ref, o_ref):
    o_ref[...] = _dot(x_ref[...], w_ref[...]).astype(o_ref.dtype)


def _matmul(x, w, tn, out_dtype=F32):
    m, k = x.shape
    n = w.shape[1]
    tm = min(1024, m)
    return pl.pallas_call(
        _mm_kernel,
        out_shape=jax.ShapeDtypeStruct((m, n), out_dtype),
        grid=(m // tm, n // tn),
        in_specs=[pl.BlockSpec((tm, k), lambda i, j: (i, 0)),
                  pl.BlockSpec((k, tn), lambda i, j: (0, j))],
        out_specs=pl.BlockSpec((tm, tn), lambda i, j: (i, j)),
        compiler_params=_params("parallel", "arbitrary"),
        name="proj_matmul",
    )(x, w)


def _mm_ln_kernel(a_ref, w_ref, h_ref, g_ref, b_ref, o_ref, ob_ref, acc_ref):
    k = pl.program_id(1)

    @pl.when(k == 0)
    def _():
        acc_ref[...] = jnp.zeros_like(acc_ref)

    acc_ref[...] += _dot(a_ref[...], w_ref[...])

    @pl.when(k == pl.num_programs(1) - 1)
    def _():
        y = DEEPNORM_ALPHA * h_ref[...] + acc_ref[...]
        mu = jnp.mean(y, axis=-1, keepdims=True)
        yc = y - mu
        var = jnp.mean(yc * yc, axis=-1, keepdims=True)
        out = yc * lax.rsqrt(var + LN_EPS) * g_ref[...] + b_ref[...]
        o_ref[...] = out
        ob_ref[...] = out.astype(BF16)


def _matmul_res_ln(a, w, h, g, b, tk):
    m, k = a.shape
    n = w.shape[1]
    tm = min(512, m)
    return pl.pallas_call(
        _mm_ln_kernel,
        out_shape=(jax.ShapeDtypeStruct((m, n), F32), jax.ShapeDtypeStruct((m, n), BF16)),
        grid=(m // tm, k // tk),
        in_specs=[pl.BlockSpec((tm, tk), lambda i, j: (i, j)),
                  pl.BlockSpec((tk, n), lambda i, j: (j, 0)),
                  pl.BlockSpec((tm, n), lambda i, j: (i, 0)),
                  pl.BlockSpec((1, n), lambda i, j: (0, 0)),
                  pl.BlockSpec((1, n), lambda i, j: (0, 0))],
        out_specs=(pl.BlockSpec((tm, n), lambda i, j: (i, 0)),
                   pl.BlockSpec((tm, n), lambda i, j: (i, 0))),
        scratch_shapes=[pltpu.VMEM((tm, n), F32)],
        compiler_params=_params("parallel", "arbitrary"),
        name="matmul_res_ln",
    )(a, w, h, g.reshape(1, n), b.reshape(1, n))


def _ffn_up_kernel(x_ref, wa_ref, wu_ref, cw_ref, cb_ref, o_ref, carry_ref):
    @pl.when(pl.program_id(1) == 0)
    def _():
        carry_ref[...] = jnp.zeros_like(carry_ref)

    x = x_ref[...]
    a = _dot(x, wa_ref[...])
    u = _dot(x, wu_ref[...])
    carry = carry_ref[...]
    cw = cw_ref[...]
    conv = (cw[0:1] * _shift_rows(a, carry, 2) + cw[1:2] * _shift_rows(a, carry, 1)
            + cw[2:3] * a + cb_ref[...])
    carry_ref[...] = a[a.shape[0] - 8:]
    o_ref[...] = (_silu(conv) * u).astype(BF16)


def _ffn_up(x, w_up, conv_w, conv_b):
    m, k = x.shape
    tn = 512
    tm = min(1024, m)
    nj = FFN_DIM // tn
    return pl.pallas_call(
        _ffn_up_kernel,
        out_shape=jax.ShapeDtypeStruct((m, FFN_DIM), BF16),
        grid=(nj, m // tm),
        in_specs=[pl.BlockSpec((tm, k), lambda j, i: (i, 0)),
                  pl.BlockSpec((k, tn), lambda j, i: (0, j)),
                  pl.BlockSpec((k, tn), lambda j, i: (0, j + nj)),
                  pl.BlockSpec((FFN_CONV, tn), lambda j, i: (0, j)),
                  pl.BlockSpec((1, tn), lambda j, i: (0, j))],
        out_specs=pl.BlockSpec((tm, tn), lambda j, i: (i, j)),
        scratch_shapes=[pltpu.VMEM((8, tn), F32)],
        compiler_params=_params("parallel", "arbitrary"),
        name="ffn_up_conv",
    )(x, w_up, w_up, conv_w, conv_b.reshape(1, FFN_DIM))


def _ssd_kernel(z_ref, x_ref, bc_ref, dt_ref, dtt_ref, cwx_ref, cbx_ref, cwb_ref, cbb_ref,
                dtb_ref, alog_ref, dtbc_ref, alogc_ref, dskip_ref, nw_ref, expand_ref,
                o_ref, cx_ref, cbc_ref, state_ref):
    L = CHUNK
    hpg = SSD_HEADS // SSD_GROUPS
    gw = hpg * SSD_HEAD_DIM

    @pl.when(pl.program_id(0) == 0)
    def _():
        cx_ref[...] = jnp.zeros_like(cx_ref)
        cbc_ref[...] = jnp.zeros_like(cbc_ref)
        state_ref[...] = jnp.zeros_like(state_ref)

    def conv(raw, carry_ref, w_ref, b_ref):
        carry = carry_ref[...]
        w = w_ref[...]
        y = (w[0:1] * _shift_rows(raw, carry, 3) + w[1:2] * _shift_rows(raw, carry, 2)
             + w[2:3] * _shift_rows(raw, carry, 1) + w[3:4] * raw + b_ref[...])
        carry_ref[...] = raw[L - 8:]
        return _silu(y)

    xs = conv(x_ref[...], cx_ref, cwx_ref, cbx_ref)
    bcs = conv(bc_ref[...], cbc_ref, cwb_ref, cbb_ref)

    def softplus(v):
        return jnp.maximum(v, 0.0) + jnp.log(1.0 + jnp.exp(-jnp.abs(v)))

    dt = softplus(dt_ref[...] + dtb_ref[...])
    da = dt * (-jnp.exp(alog_ref[...]))
    row = lax.broadcasted_iota(jnp.int32, (L, L), 0)
    col = lax.broadcasted_iota(jnp.int32, (L, L), 1)
    causal = row >= col
    tril = causal.astype(F32)
    a_cum = jnp.dot(tril, da, precision=HIGHEST, preferred_element_type=F32)
    dt_r = softplus(dtt_ref[...] + dtbc_ref[...])
    da_r = dt_r * (-jnp.exp(alogc_ref[...]))
    a_cum_r = jnp.dot(da_r, (row <= col).astype(F32), precision=HIGHEST,
                      preferred_element_type=F32)

    expand = expand_ref[...]

    def widen(v):
        return jnp.dot(v, expand, precision=HIGHEST, preferred_element_type=F32)

    a_last = a_cum[L - 1:L]
    dt_w = widen(dt)
    decay_in_w = widen(jnp.exp(a_cum))
    decay_end_w = widen(jnp.exp(a_last - a_cum))
    chunk_decay_w = widen(jnp.broadcast_to(jnp.exp(a_last), (8, LANES)))[0:1]
    xdt = xs * dt_w
    xdt_b = xdt.astype(BF16)
    xdt_end_b = (xdt * decay_end_w).astype(BF16)
    lane_lo = lax.broadcasted_iota(jnp.int32, (L, LANES), 1) < SSD_HEAD_DIM

    ys = []
    for g in range(SSD_GROUPS):
        b_g = bcs[:, g * SSD_STATE:(g + 1) * SSD_STATE].astype(BF16)
        c_g = bcs[:, (SSD_GROUPS + g) * SSD_STATE:(SSD_GROUPS + g + 1) * SSD_STATE].astype(BF16)
        cb = _dot_nt(c_g, b_g)
        pieces = []
        for pair in range(hpg // 2):
            outs = []
            for k in range(2):
                h = g * hpg + pair * 2 + k
                diff = a_cum[:, h:h + 1] - a_cum_r[h:h + 1, :]
                w = (cb * jnp.exp(jnp.where(causal, diff, NEG))).astype(BF16)
                lo = (g * hpg + pair * 2) * SSD_HEAD_DIM
                outs.append(_dot(w, xdt_b[:, lo:lo + LANES]))
            pieces.append(jnp.where(lane_lo, outs[0], outs[1]))
        y_diag = jnp.concatenate(pieces, axis=1)
        state = state_ref[g]
        sl = slice(g * gw, (g + 1) * gw)
        y_off = _dot(c_g, state.astype(BF16)) * decay_in_w[:, sl]
        state_ref[g] = state * chunk_decay_w[:, sl] + _dot_tn(b_g, xdt_end_b[:, sl])
        ys.append(y_diag + y_off)
    y = jnp.concatenate(ys, axis=1) + xs * dskip_ref[...]
    yg = y * _silu(z_ref[...])
    half = SSD_INNER // SSD_GROUPS
    outs = []
    for g in range(SSD_GROUPS):
        part = yg[:, g * half:(g + 1) * half]
        outs.append(part * lax.rsqrt(jnp.mean(part * part, axis=-1, keepdims=True) + LN_EPS))
    o_ref[...] = (jnp.concatenate(outs, axis=1) * nw_ref[...]).astype(BF16)


def _ssd(p, dt_t, conv_w, conv_b, dt_bias, a_log, d_skip, norm_w, col):
    t = p.shape[0]
    L = CHUNK
    pad16 = lambda v: jnp.pad(v.astype(F32), (0, LANES - SSD_HEADS)).reshape(1, LANES)
    expand = (jnp.arange(LANES)[:, None] == (jnp.arange(SSD_INNER) // SSD_HEAD_DIM)[None, :]).astype(F32)
    full = lambda shape: pl.BlockSpec(shape, lambda c: (0,) * len(shape))
    return pl.pallas_call(
        _ssd_kernel,
        out_shape=jax.ShapeDtypeStruct((t, SSD_INNER), BF16),
        grid=(t // L,),
        in_specs=[pl.BlockSpec((L, 1024), lambda c: (c, col["z"])),
                  pl.BlockSpec((L, 1024), lambda c: (c, col["x"])),
                  pl.BlockSpec((L, 512), lambda c: (c, col["bc"])),
                  pl.BlockSpec((L, LANES), lambda c: (c, col["dt"])),
                  pl.BlockSpec((SSD_HEADS, L), lambda c: (0, c)),
                  full((SSD_CONV, 1024)), full((1, 1024)), full((SSD_CONV, 512)), full((1, 512)),
                  full((1, LANES)), full((1, LANES)), full((SSD_HEADS, 1)), full((SSD_HEADS, 1)),
                  full((1, 1024)), full((1, 1024)), full((LANES, 1024))],
        out_specs=pl.BlockSpec((L, SSD_INNER), lambda c: (c, 0)),
        scratch_shapes=[pltpu.VMEM((8, 1024), F32), pltpu.VMEM((8, 512), F32),
                        pltpu.VMEM((SSD_GROUPS, SSD_STATE, 512), F32)],
        compiler_params=_params("arbitrary"),
        name="ssd_mixer",
    )(p, p, p, p, dt_t,
      conv_w[:, :SSD_INNER], conv_b[:SSD_INNER].reshape(1, -1),
      conv_w[:, SSD_INNER:], conv_b[SSD_INNER:].reshape(1, -1),
      pad16(dt_bias), pad16(a_log), dt_bias.reshape(-1, 1), a_log.reshape(-1, 1),
      jnp.repeat(d_skip, SSD_HEAD_DIM).reshape(1, -1), norm_w.reshape(1, -1), expand)


def _ret_kernel(logg_ref, q_ref, k_ref, v_ref, g_ref, cos_ref, sin_ref, o_ref, state_ref):
    C = CHUNK
    h = pl.program_id(0)

    @pl.when(pl.program_id(1) == 0)
    def _():
        state_ref[...] = jnp.zeros_like(state_ref)

    lg = logg_ref[h]
    cos = cos_ref[...]
    sin = sin_ref[...]

    def rot(x):
        return x * cos + pltpu.roll(x, HEAD_DIM // 2, 1) * sin

    q = rot(q_ref[...])
    k = rot(k_ref[...]) * (HEAD_DIM ** -0.5)
    v = v_ref[...].astype(BF16)
    row = lax.broadcasted_iota(jnp.int32, (C, C), 0)
    col = lax.broadcasted_iota(jnp.int32, (C, C), 1)
    rel = row - col
    decay = jnp.where(rel >= 0, jnp.exp(lg * jnp.maximum(rel, 0).astype(F32)), 0.0)
    inner = _dot_nt(q.astype(BF16), k.astype(BF16)) * decay
    y = _dot(inner.astype(BF16), v)
    pos = lax.broadcasted_iota(jnp.int32, (C, 1), 0).astype(F32)
    q_dec = jnp.exp(lg * (pos + 1.0))
    k_dec = jnp.exp(lg * (C - 1.0 - pos))
    state = state_ref[...]
    y = y + _dot((q * q_dec).astype(BF16), state.astype(BF16))
    chunk_dec = jnp.exp(lg * jnp.full((1, HEAD_DIM), float(C), F32))
    state_ref[...] = state * chunk_dec + _dot_tn((k * k_dec).astype(BF16), v)
    mu = jnp.mean(y, axis=-1, keepdims=True)
    yc = y - mu
    var = jnp.mean(yc * yc, axis=-1, keepdims=True)
    o_ref[...] = (_silu(g_ref[...]) * (yc * lax.rsqrt(var + LN_EPS))).astype(BF16)


def _retention(p, col):
    t = p.shape[0]
    C = CHUNK
    half = HEAD_DIM // 2
    inv = 1.0 / (ROPE_BASE ** (jnp.arange(0, HEAD_DIM, 2, dtype=F32) / HEAD_DIM))
    ang = jnp.arange(t, dtype=F32)[:, None] * inv[None, :]
    cos = jnp.concatenate([jnp.cos(ang), jnp.cos(ang)], axis=1)
    sin = jnp.concatenate([-jnp.sin(ang), jnp.sin(ang)], axis=1)
    log_g = jnp.log(1.0 - 2.0 ** (-5.0 - jnp.arange(RET_HEADS, dtype=F32)))
    blk = lambda name: pl.BlockSpec((C, HEAD_DIM), lambda h, c, lg: (c, col[name] + h))
    tab = pl.BlockSpec((C, HEAD_DIM), lambda h, c, lg: (c, 0))
    return pl.pallas_call(
        _ret_kernel,
        out_shape=jax.ShapeDtypeStruct((t, RET_HEADS * HEAD_DIM), BF16),
        grid_spec=pltpu.PrefetchScalarGridSpec(
            num_scalar_prefetch=1,
            grid=(RET_HEADS, t // C),
            in_specs=[blk("rq"), blk("rk"), blk("rv"), blk("rg"), tab, tab],
            out_specs=pl.BlockSpec((C, HEAD_DIM), lambda h, c, lg: (c, h)),
            scratch_shapes=[pltpu.VMEM((HEAD_DIM, HEAD_DIM), F32)]),
        compiler_params=_params("parallel", "arbitrary"),
        name="retention_mixer",
    )(log_g, p, p, p, p, cos, sin)


def _rel_bucket(dist):
    n = jnp.maximum(dist, 0)
    max_exact = REL_BUCKETS // 2
    nf = jnp.maximum(n, max_exact).astype(F32)
    large = max_exact + (jnp.log(nf / max_exact) / math.log(REL_MAX_DISTANCE / max_exact)
                         * (REL_BUCKETS - max_exact)).astype(jnp.int32)
    large = jnp.minimum(large, REL_BUCKETS - 1)
    return jnp.where(n < max_exact, n, large)


def _bias_lookup(tab_ref, bucket, h):
    acc = jnp.full(bucket.shape, tab_ref[h], F32)
    for b in range(1, REL_BUCKETS):
        acc = jnp.where(bucket == b, tab_ref[b * ATTN_HEADS + h], acc)
    return acc


def _bias_tile_kernel(tab_ref, o_ref, *, window):
    h = pl.program_id(0)
    o = pl.program_id(1)
    row = lax.broadcasted_iota(jnp.int32, (CHUNK, CHUNK), 0)
    col = lax.broadcasted_iota(jnp.int32, (CHUNK, CHUNK), 1)
    d = o * CHUNK + row - col
    valid = d >= 0
    if window is not None:
        valid = valid & (d < window)
    o_ref[0, 0] = jnp.where(valid, _bias_lookup(tab_ref, _rel_bucket(d), h), NEG)


def _bias_tiles(rel_bias, n_off, window=None):
    return pl.pallas_call(
        functools.partial(_bias_tile_kernel, window=window),
        out_shape=jax.ShapeDtypeStruct((ATTN_HEADS, n_off, CHUNK, CHUNK), F32),
        grid_spec=pltpu.PrefetchScalarGridSpec(
            num_scalar_prefetch=1, grid=(ATTN_HEADS, n_off), in_specs=[],
            out_specs=pl.BlockSpec((1, 1, CHUNK, CHUNK), lambda h, o, tab: (h, o, 0, 0))),
        compiler_params=_params("parallel", "parallel"),
        name="rel_bias_tiles",
    )(rel_bias.reshape(-1))


def _cmp_bias_kernel(tab_ref, o_ref, *, width):
    h = pl.program_id(0)
    row = lax.broadcasted_iota(jnp.int32, (CHUNK, 2 * width), 0)
    col = lax.broadcasted_iota(jnp.int32, (CHUNK, 2 * width), 1)
    d = row - NSA_CMP_STRIDE * (col - width) - (NSA_CMP_BLOCK - 1)
    o_ref[0] = jnp.where(d >= 0, _bias_lookup(tab_ref, _rel_bucket(d), h), NEG)


def _cmp_bias_base(rel_bias, width):
    return pl.pallas_call(
        functools.partial(_cmp_bias_kernel, width=width),
        out_shape=jax.ShapeDtypeStruct((ATTN_HEADS, CHUNK, 2 * width), F32),
        grid_spec=pltpu.PrefetchScalarGridSpec(
            num_scalar_prefetch=1, grid=(ATTN_HEADS,), in_specs=[],
            out_specs=pl.BlockSpec((1, CHUNK, 2 * width), lambda h, tab: (h, 0, 0))),
        compiler_params=_params("parallel"),
        name="cmp_bias_base",
    )(rel_bias.reshape(-1))


def _kprep_kernel(k_ref, v_ref, ka_ref, vb_ref, *rest, heads, block, with_mean):
    i = pl.program_id(0)
    k = k_ref[...]
    v = v_ref[...]
    row = lax.broadcasted_iota(jnp.int32, (CHUNK, LANES), 0)
    lane = lax.broadcasted_iota(jnp.int32, (CHUNK, LANES), 1)
    onehot = (lane == (i * CHUNK + row) // block).astype(BF16)
    for h in range(heads):
        sl = slice(h * HEAD_DIM, (h + 1) * HEAD_DIM)
        ka_ref[h] = jnp.concatenate([k[:, sl].astype(BF16), onehot], axis=1)
        vb_ref[h] = v[:, sl].astype(BF16)
    if with_mean:
        km_ref = rest[0]

        @pl.when(i == 0)
        def _():
            km_ref[...] = jnp.zeros_like(km_ref)

        km_ref[pl.ds(i, 1), :] = jnp.mean(k, axis=0, keepdims=True)


def _kprep(p, kcol, vcol, heads, block, with_mean):
    t = p.shape[0]
    w = heads * HEAD_DIM
    out_shape = [jax.ShapeDtypeStruct((heads, t, 2 * HEAD_DIM), BF16),
                 jax.ShapeDtypeStruct((heads, t, HEAD_DIM), BF16)]
    out_specs = [pl.BlockSpec((heads, CHUNK, 2 * HEAD_DIM), lambda i: (0, i, 0)),
                 pl.BlockSpec((heads, CHUNK, HEAD_DIM), lambda i: (0, i, 0))]
    if with_mean:
        out_shape.append(jax.ShapeDtypeStruct((LANES, w), F32))
        out_specs.append(pl.BlockSpec((LANES, w), lambda i: (0, 0)))
    return pl.pallas_call(
        functools.partial(_kprep_kernel, heads=heads, block=block, with_mean=with_mean),
        out_shape=tuple(out_shape),
        grid=(t // CHUNK,),
        in_specs=[pl.BlockSpec((CHUNK, w), lambda i: (i, kcol)),
                  pl.BlockSpec((CHUNK, w), lambda i: (i, vcol))],
        out_specs=tuple(out_specs),
        compiler_params=_params("arbitrary"),
        name="key_prep",
    )(p, p)


def _topk_mask(score, k):
    lane = lax.broadcasted_iota(jnp.int32, score.shape, 1)

    def body(_, carry):
        g, sel = carry
        m = jnp.max(g, axis=1, keepdims=True)
        idx = jnp.min(jnp.where(g == m, lane, LANES), axis=1, keepdims=True)
        hit = lane == idx
        sel = jnp.where(hit & (m > -0.5 * BIG), 1.0, sel)
        return jnp.where(hit, -BIG, g), sel

    _, sel = lax.fori_loop(0, k, body, (score, jnp.zeros(score.shape, F32)))
    return sel > 0.5


def _moba_select_kernel(q_ref, km_ref, qa_ref):
    i = pl.program_id(0)
    q = q_ref[...]
    km = km_ref[...]
    lane = lax.broadcasted_iota(jnp.int32, (CHUNK, LANES), 1)
    for h in range(ATTN_HEADS):
        sl = slice(h * HEAD_DIM, (h + 1) * HEAD_DIM)
        gate = lax.dot_general(q[:, sl], km[:, sl], (((1,), (1,)), ((), ())),
                               precision=HIGHEST, preferred_element_type=F32)
        sel = _topk_mask(jnp.where(lane < i, gate, -BIG), MOBA_TOPK)
        pen = jnp.where(sel | (lane == i), 0.0, NEG)
        qa_ref[h] = jnp.concatenate([q[:, sl].astype(BF16), pen.astype(BF16)], axis=1)


def _moba_select(p, qcol, kmean):
    t = p.shape[0]
    w = ATTN_HEADS * HEAD_DIM
    return pl.pallas_call(
        _moba_select_kernel,
        out_shape=jax.ShapeDtypeStruct((ATTN_HEADS, t, 2 * HEAD_DIM), BF16),
        grid=(t // CHUNK,),
        in_specs=[pl.BlockSpec((CHUNK, w), lambda i: (i, qcol)),
                  pl.BlockSpec((LANES, w), lambda i: (0, 0))],
        out_specs=pl.BlockSpec((ATTN_HEADS, CHUNK, 2 * HEAD_DIM), lambda i: (0, i, 0)),
        compiler_params=_params("parallel"),
        name="moba_select",
    )(p, kmean)


def _flash_kernel(q_ref, k_ref, v_ref, bias_ref, *rest, span, combine):
    if combine:
        ocmp_ref, osel_ref, gate_ref, o_ref, m_ref, l_ref, acc_ref = rest
    else:
        o_ref, m_ref, l_ref, acc_ref = rest
    h = pl.program_id(0)
    i = pl.program_id(1)
    n_off = bias_ref.shape[0]
    scale = HEAD_DIM ** -0.5
    q = q_ref[...].astype(BF16)
    m_ref[...] = jnp.full_like(m_ref, -BIG)
    l_ref[...] = jnp.zeros_like(l_ref)
    acc_ref[...] = jnp.zeros_like(acc_ref)

    def body(j, carry):
        rows = pl.ds(pl.multiple_of(j * CHUNK, CHUNK), CHUNK)
        s = _dot_nt(q, k_ref[rows, :].astype(BF16)) * scale + bias_ref[jnp.minimum(i - j, n_off - 1)]
        m_prev = m_ref[...]
        m_new = jnp.maximum(m_prev, jnp.max(s, axis=1, keepdims=True))
        alpha = jnp.exp(m_prev - m_new)
        pr = jnp.exp(s - m_new)
        l_ref[...] = alpha * l_ref[...] + jnp.sum(pr, axis=1, keepdims=True)
        acc_ref[...] = alpha * acc_ref[...] + _dot(pr.astype(BF16), v_ref[rows, :].astype(BF16))
        m_ref[...] = m_new
        return carry

    j_lo = 0 if span is None else jnp.maximum(i - (span - 1), 0)
    lax.fori_loop(j_lo, i + 1, body, 0)
    out = acc_ref[...] * (1.0 / l_ref[...])
    if combine:
        sig = 1.0 / (1.0 + jnp.exp(-gate_ref[...]))
        lane = lax.broadcasted_iota(jnp.int32, sig.shape, 1)
        pick = lambda c: jnp.sum(jnp.where(lane == 3 * h + c, sig, 0.0), axis=1, keepdims=True)
        out = pick(0) * ocmp_ref[...] + pick(1) * osel_ref[...] + pick(2) * out
    o_ref[...] = out.astype(o_ref.dtype)


def _flash(q, k, v, bias, *, heads, rep, out_dtype, span=None, combine=None):
    nh = heads
    n_off = bias.shape[1]

    def spec(x, rows, hmap):
        if isinstance(x, tuple):
            arr, base = x
            if rows is None:
                return arr, pl.BlockSpec((arr.shape[0], HEAD_DIM), lambda h, i: (0, base + hmap(h)))
            return arr, pl.BlockSpec((rows, HEAD_DIM), lambda h, i: (i, base + hmap(h)))
        if rows is None:
            return x, pl.BlockSpec((None,) + x.shape[1:], lambda h, i: (hmap(h), 0, 0))
        return x, pl.BlockSpec((None, rows, x.shape[2]), lambda h, i: (hmap(h), i, 0))

    qa, q_spec = spec(q, CHUNK, lambda h: h)
    ka, k_spec = spec(k, None, lambda h: h // rep)
    va, v_spec = spec(v, None, lambda h: h // rep)
    t = ka.shape[0] if isinstance(k, tuple) else ka.shape[1]
    in_specs = [q_spec, k_spec, v_spec,
                pl.BlockSpec((None, n_off, CHUNK, CHUNK), lambda h, i: (h, 0, 0, 0))]
    args = [qa, ka, va, bias]
    if combine is not None:
        ocmp, osel, p, gcol = combine
        tile = pl.BlockSpec((CHUNK, HEAD_DIM), lambda h, i: (i, h))
        in_specs += [tile, tile, pl.BlockSpec((CHUNK, LANES), lambda h, i: (i, gcol))]
        args += [ocmp, osel, p]
    return pl.pallas_call(
        functools.partial(_flash_kernel, span=span, combine=combine is not None),
        out_shape=jax.ShapeDtypeStruct((t, nh * HEAD_DIM), out_dtype),
        grid=(nh, t // CHUNK),
        in_specs=in_specs,
        out_specs=pl.BlockSpec((CHUNK, HEAD_DIM), lambda h, i: (i, h)),
        scratch_shapes=[pltpu.VMEM((CHUNK, 1), F32), pltpu.VMEM((CHUNK, 1), F32),
                        pltpu.VMEM((CHUNK, HEAD_DIM), F32)],
        compiler_params=_params("parallel", "arbitrary"),
        name="flash_attention",
    )(*args)


def _compress_kernel(x_ref, pe_ref, w1_ref, w2_ref, o_ref):
    x = x_ref[0, 0]
    pe = pe_ref[0]
    half = NSA_CMP_STRIDE * HEAD_DIM
    a = _dot((x + pe[0:1]).astype(BF16), w1_ref[0, :half, :])
    b = _dot((x + pe[1:2]).astype(BF16), w1_ref[0, half:, :])
    hid = a + pltpu.roll(b, x.shape[0] - 1, 0)
    gelu = 0.5 * hid * (1.0 + jnp.tanh(math.sqrt(2.0 / math.pi) * (hid + 0.044715 * hid * hid * hid)))
    o_ref[0, 0] = _dot(gelu.astype(BF16), w2_ref[0])


def _compress(x2, pe, w1, w2):
    _, hkv, w, kd = x2.shape
    return pl.pallas_call(
        _compress_kernel,
        out_shape=jax.ShapeDtypeStruct((2, hkv, w, HEAD_DIM), F32),
        grid=(2, hkv),
        in_specs=[pl.BlockSpec((1, 1, w, kd), lambda a, h: (a, h, 0, 0)),
                  pl.BlockSpec((1, 2, kd), lambda a, h: (a, 0, 0)),
                  pl.BlockSpec((1, 2 * kd, HEAD_DIM), lambda a, h: (a, 0, 0)),
                  pl.BlockSpec((1, HEAD_DIM, HEAD_DIM), lambda a, h: (a, 0, 0))],
        out_specs=pl.BlockSpec((1, 1, w, HEAD_DIM), lambda a, h: (a, h, 0, 0)),
        compiler_params=_params("parallel", "parallel"),
        name="nsa_compress",
    )(x2, pe, w1, w2)


def _nsa_cmp_kernel(q_ref, kv_ref, base_ref, ov_ref, ocmp_ref, qa_ref):
    i = pl.program_id(0)
    width = kv_ref.shape[2]
    scale = HEAD_DIM ** -0.5
    q = q_ref[...]
    shift = (NSA_CMP_STRIDE * i + width) % (2 * width)
    lane = lax.broadcasted_iota(jnp.int32, (CHUNK, LANES), 1)
    cur = (i * CHUNK + lax.broadcasted_iota(jnp.int32, (CHUNK, LANES), 0)) // NSA_SEL_BLOCK
    forced = (lane == 0) | (lane == cur) | (lane == cur - 1)
    ov = ov_ref[...]
    for kvh in range(NSA_KV_HEADS):
        kc = kv_ref[0, kvh].astype(BF16)
        vc = kv_ref[1, kvh].astype(BF16)
        imp = jnp.zeros((CHUNK, LANES), F32)
        for g in range(NSA_GROUP):
            h = kvh * NSA_GROUP + g
            sl = slice(h * HEAD_DIM, (h + 1) * HEAD_DIM)
            bias = pltpu.roll(base_ref[h], shift, 1)[:, :width]
            s = _dot_nt(q[:, sl].astype(BF16), kc) * scale + bias
            m = jnp.max(s, axis=1, keepdims=True)
            e = jnp.exp(s - m)
            l = jnp.sum(e, axis=1, keepdims=True)
            pr = e * jnp.where(m > 0.1 * NEG, 1.0 / l, 0.0)
            ocmp_ref[:, sl] = _dot(pr.astype(BF16), vc)
            imp = imp + jnp.dot(pr, ov, precision=HIGHEST, preferred_element_type=F32)
        score = jnp.where(forced, BIG, jnp.where(lane <= cur, imp, -BIG))
        pen = jnp.where(_topk_mask(score, NSA_SEL_TOPN), 0.0, NEG).astype(BF16)
        for g in range(NSA_GROUP):
            h = kvh * NSA_GROUP + g
            sl = slice(h * HEAD_DIM, (h + 1) * HEAD_DIM)
            qa_ref[h] = jnp.concatenate([q[:, sl].astype(BF16), pen], axis=1)


def _nsa_cmp(p, qcol, kv_cmp, base):
    t = p.shape[0]
    width = kv_cmp.shape[2]
    w = ATTN_HEADS * HEAD_DIM
    n = jnp.arange(width)[:, None] * NSA_CMP_STRIDE
    j = jnp.arange(LANES)[None, :] * NSA_SEL_BLOCK
    overlap = ((n < j + NSA_SEL_BLOCK) & (n + NSA_CMP_BLOCK > j)
               & (jnp.arange(width)[:, None] < width - 1)).astype(F32)
    return pl.pallas_call(
        _nsa_cmp_kernel,
        out_shape=(jax.ShapeDtypeStruct((t, w), F32),
                   jax.ShapeDtypeStruct((ATTN_HEADS, t, 2 * HEAD_DIM), BF16)),
        grid=(t // CHUNK,),
        in_specs=[pl.BlockSpec((CHUNK, w), lambda i: (i, qcol)),
                  pl.BlockSpec(kv_cmp.shape, lambda i: (0, 0, 0, 0)),
                  pl.BlockSpec(base.shape, lambda i: (0, 0, 0)),
                  pl.BlockSpec((width, LANES), lambda i: (0, 0))],
        out_specs=(pl.BlockSpec((CHUNK, w), lambda i: (i, 0)),
                   pl.BlockSpec((ATTN_HEADS, CHUNK, 2 * HEAD_DIM), lambda i: (0, i, 0))),
        compiler_params=_params("parallel"),
        name="nsa_cmp_select",
    )(p, kv_cmp, base, overlap)


_EVEN_ORDER = np.concatenate([np.arange(0, 2048), np.arange(2576, 5648), np.arange(2048, 2576)])
_EVEN_WIDTH = 5760
_EVEN_TN = 640
_DEINTERLEAVE = np.concatenate([np.arange(0, HEAD_DIM, 2), np.arange(1, HEAD_DIM, 2)])
_RET_PERM = (np.arange(RET_HEADS)[:, None] * HEAD_DIM + _DEINTERLEAVE[None, :]).reshape(-1)
_ODD_ORDER = np.concatenate([np.arange(0, 1024), 2584 + _RET_PERM, 3608 + _RET_PERM, np.arange(4632, 6680),
                             np.arange(1024, 2584)])
_ODD_WIDTH = 6912
_ODD_TN = 768


def _reorder_cols(w, order, width):
    idx = np.concatenate([order, np.full(width - len(order), w.shape[1])]).astype(np.int32)
    return jnp.pad(w, ((0, 0), (0, 1)))[:, idx].astype(BF16)


def _even_mixer(hb, w_in, conv_w, conv_b, dt_bias, a_log, d_skip, norm_w, bias_c):
    p = _matmul(hb, _reorder_cols(w_in, _EVEN_ORDER, _EVEN_WIDTH), _EVEN_TN)
    dt_t = p[:, 5632:5632 + SSD_HEADS].T
    y_ssd = _ssd(p, dt_t, conv_w, conv_b, dt_bias, a_log, d_skip, norm_w,
                 {"z": 0, "x": 1, "bc": 10, "dt": 44})
    k_aug, v_b, kmean = _kprep(p, 3, 4, ATTN_HEADS, MOBA_BLOCK, True)
    q_aug = _moba_select(p, 2, kmean)
    y_moba = _flash(q_aug, k_aug, v_b, bias_c, heads=ATTN_HEADS, rep=1, out_dtype=BF16)
    return jnp.concatenate([y_ssd, y_moba], axis=1)


def _odd_mixer(hb, w_in, cmp_pe, cmp_w1, cmp_w2, bias_c, bias_w, cmp_base):
    p = _matmul(hb, _reorder_cols(w_in, _ODD_ORDER, _ODD_WIDTH), _ODD_TN)
    t = p.shape[0]
    x2 = p[:, 5120:5632].reshape(t, 2, NSA_KV_HEADS, HEAD_DIM).transpose(1, 2, 0, 3)
    x2 = x2.reshape(2, NSA_KV_HEADS, t // NSA_CMP_STRIDE, NSA_CMP_STRIDE * HEAD_DIM)
    kv_cmp = _compress(x2, cmp_pe.reshape(2, 2, NSA_CMP_STRIDE * HEAD_DIM),
                       cmp_w1.astype(BF16), cmp_w2.astype(BF16))
    o_cmp, q_aug = _nsa_cmp(p, 0, kv_cmp, cmp_base)
    k_aug, v_b = _kprep(p, 22, 23, NSA_KV_HEADS, NSA_SEL_BLOCK, False)
    o_sel = _flash(q_aug, k_aug, v_b, bias_c, heads=ATTN_HEADS, rep=NSA_GROUP, out_dtype=F32)
    y_nsa = _flash((p, 0), (p, 48), (p, 50), bias_w, heads=ATTN_HEADS, rep=NSA_GROUP, out_dtype=BF16,
                   span=NSA_WINDOW // CHUNK + 1, combine=(o_cmp, o_sel, p, 52))
    y_ret = _retention(p, {"rq": 8, "rk": 16, "rv": 24, "rg": 32})
    return jnp.concatenate([y_nsa, y_ret], axis=1)


def kernel(x, rel_bias, ev_w_in, ev_conv_w, ev_conv_b, ev_dt_bias, ev_a_log, ev_d_skip, ev_norm_w, ev_w_out,
           od_w_in, od_cmp_pe, od_cmp_w1, od_cmp_w2, od_w_out,
           ffn_w_up, ffn_conv_w, ffn_conv_b, ffn_w_down, ln_g, ln_b):
    _, t0, _ = x.shape
    t = -(-t0 // CHUNK) * CHUNK
    h = jnp.pad(x[0], ((0, t - t0), (0, 0)))
    hb = h.astype(BF16)
    bias_c = _bias_tiles(rel_bias, 8)
    bias_w = _bias_tiles(rel_bias, NSA_WINDOW // CHUNK + 1, window=NSA_WINDOW)
    cmp_base = _cmp_bias_base(rel_bias, t // NSA_CMP_STRIDE)
    for layer in range(DEPTH):
        i = layer // 2
        if layer % 2 == 0:
            mix = _even_mixer(hb, ev_w_in[i], ev_conv_w[i], ev_conv_b[i], ev_dt_bias[i], ev_a_log[i],
                              ev_d_skip[i], ev_norm_w[i], bias_c)
            w_out = ev_w_out[i]
        else:
            mix = _odd_mixer(hb, od_w_in[i], od_cmp_pe[i], od_cmp_w1[i], od_cmp_w2[i], bias_c, bias_w, cmp_base)
            w_out = od_w_out[i]
        h, hb = _matmul_res_ln(mix, w_out.astype(BF16), h, ln_g[layer, 0], ln_b[layer, 0], 512)
        gated = _ffn_up(hb, ffn_w_up[layer].astype(BF16), ffn_conv_w[layer], ffn_conv_b[layer])
        h, hb = _matmul_res_ln(gated, ffn_w_down[layer].astype(BF16), h, ln_g[layer, 1], ln_b[layer, 1], 512)
    return h[None, :t0]
```

```python
import functools
import math

import numpy as np
import jax
import jax.numpy as jnp
from jax import lax
from jax.experimental import pallas as pl
from jax.experimental.pallas import tpu as pltpu

F32 = jnp.float32
BF16 = jnp.bfloat16
HIGHEST = lax.Precision.HIGHEST

D_MODEL = 2048
DEPTH = 4
HEAD_DIM = 128
ATTN_HEADS = 8
REL_BUCKETS = 32
REL_MAX_DISTANCE = 2048
SSD_HEADS = 16
SSD_HEAD_DIM = 64
SSD_INNER = 1024
SSD_GROUPS = 2
SSD_STATE = 128
SSD_CONV = 4
MOBA_BLOCK = 256
MOBA_TOPK = 3
NSA_KV_HEADS = 2
NSA_GROUP = 4
NSA_CMP_BLOCK = 32
NSA_CMP_STRIDE = 16
NSA_SEL_BLOCK = 64
NSA_SEL_TOPN = 16
NSA_WINDOW = 512
RET_HEADS = 8
ROPE_BASE = 10000.0
FFN_DIM = 5632
FFN_CONV = 3
LN_EPS = 1e-5
NEG = -1e30
BIG = 3e38
DEEPNORM_ALPHA = (2 * DEPTH) ** 0.25

CHUNK = 256
LANES = 128
VMEM_LIMIT = 56 * 1024 * 1024


def _params(*sem):
    return pltpu.CompilerParams(dimension_semantics=sem, vmem_limit_bytes=VMEM_LIMIT)


def _silu(x):
    return x * (1.0 / (1.0 + jnp.exp(-x)))


def _dot(a, b):
    return jnp.dot(a, b, preferred_element_type=F32)


def _dot_nt(a, b):
    return lax.dot_general(a, b, (((1,), (1,)), ((), ())), preferred_element_type=F32)


def _dot_tn(a, b):
    return lax.dot_general(a, b, (((0,), (0,)), ((), ())), preferred_element_type=F32)


def _shift_rows(a, carry, k):
    rolled = pltpu.roll(a, k, 0)
    head = jnp.where(lax.broadcasted_iota(jnp.int32, carry.shape, 0) < k,
                     pltpu.roll(carry, k, 0), rolled[:8])
    return jnp.concatenate([head, rolled[8:]], axis=0)


def _mm_kernel(x_ref, w_ref, o_ref):
    o_ref[...] = _dot(x_ref[...], w_ref[...]).astype(o_ref.dtype)


def _matmul(x, w, tn, out_dtype=F32):
    m, k = x.shape
    n = w.shape[1]
    tm = min(1024, m)
    return pl.pallas_call(
        _mm_kernel,
        out_shape=jax.ShapeDtypeStruct((m, n), out_dtype),
        grid=(m // tm, n // tn),
        in_specs=[pl.BlockSpec((tm, k), lambda i, j: (i, 0)),
                  pl.BlockSpec((k, tn), lambda i, j: (0, j))],
        out_specs=pl.BlockSpec((tm, tn), lambda i, j: (i, j)),
        compiler_params=_params("parallel", "arbitrary"),
        name="proj_matmul",
    )(x, w)


def _mm_ln_kernel(a_ref, w_ref, h_ref, g_ref, b_ref, o_ref, ob_ref, acc_ref):
    k = pl.program_id(1)

    @pl.when(k == 0)
    def _():
        acc_ref[...] = jnp.zeros_like(acc_ref)

    acc_ref[...] += _dot(a_ref[...], w_ref[...])

    @pl.when(k == pl.num_programs(1) - 1)
    def _():
        y = DEEPNORM_ALPHA * h_ref[...] + acc_ref[...]
        mu = jnp.mean(y, axis=-1, keepdims=True)
        yc = y - mu
        var = jnp.mean(yc * yc, axis=-1, keepdims=True)
        out = yc * lax.rsqrt(var + LN_EPS) * g_ref[...] + b_ref[...]
        o_ref[...] = out
        ob_ref[...] = out.astype(BF16)


def _matmul_res_ln(a, w, h, g, b, tk):
    m, k = a.shape
    n = w.shape[1]
    tm = min(512, m)
    return pl.pallas_call(
        _mm_ln_kernel,
        out_shape=(jax.ShapeDtypeStruct((m, n), F32), jax.ShapeDtypeStruct((m, n), BF16)),
        grid=(m // tm, k // tk),
        in_specs=[pl.BlockSpec((tm, tk), lambda i, j: (i, j)),
                  pl.BlockSpec((tk, n), lambda i, j: (j, 0)),
                  pl.BlockSpec((tm, n), lambda i, j: (i, 0)),
                  pl.BlockSpec((1, n), lambda i, j: (0, 0)),
                  pl.BlockSpec((1, n), lambda i, j: (0, 0))],
        out_specs=(pl.BlockSpec((tm, n), lambda i, j: (i, 0)),
                   pl.BlockSpec((tm, n), lambda i, j: (i, 0))),
        scratch_shapes=[pltpu.VMEM((tm, n), F32)],
        compiler_params=_params("parallel", "arbitrary"),
        name="matmul_res_ln",
    )(a, w, h, g.reshape(1, n), b.reshape(1, n))


def _ffn_up_kernel(x_ref, wa_ref, wu_ref, cw_ref, cb_ref, o_ref, carry_ref):
    @pl.when(pl.program_id(1) == 0)
    def _():
        carry_ref[...] = jnp.zeros_like(carry_ref)

    x = x_ref[...]
    a = _dot(x, wa_ref[...])
    u = _dot(x, wu_ref[...])
    carry = carry_ref[...]
    cw = cw_ref[...]
    conv = (cw[0:1] * _shift_rows(a, carry, 2) + cw[1:2] * _shift_rows(a, carry, 1)
            + cw[2:3] * a + cb_ref[...])
    carry_ref[...] = a[a.shape[0] - 8:]
    o_ref[...] = (_silu(conv) * u).astype(BF16)


def _ffn_up(x, w_up, conv_w, conv_b):
    m, k = x.shape
    tn = 512
    tm = min(1024, m)
    nj = FFN_DIM // tn
    return pl.pallas_call(
        _ffn_up_kernel,
        out_shape=jax.ShapeDtypeStruct((m, FFN_DIM), BF16),
        grid=(nj, m // tm),
        in_specs=[pl.BlockSpec((tm, k), lambda j, i: (i, 0)),
                  pl.BlockSpec((k, tn), lambda j, i: (0, j)),
                  pl.BlockSpec((k, tn), lambda j, i: (0, j + nj)),
                  pl.BlockSpec((FFN_CONV, tn), lambda j, i: (0, j)),
                  pl.BlockSpec((1, tn), lambda j, i: (0, j))],
        out_specs=pl.BlockSpec((tm, tn), lambda j, i: (i, j)),
        scratch_shapes=[pltpu.VMEM((8, tn), F32)],
        compiler_params=_params("parallel", "arbitrary"),
        name="ffn_up_conv",
    )(x, w_up, w_up, conv_w, conv_b.reshape(1, FFN_DIM))


def _ssd_kernel(z_ref, x_ref, bc_ref, dt_ref, dtt_ref, cwx_ref, cbx_ref, cwb_ref, cbb_ref,
                dtb_ref, alog_ref, dtbc_ref, alogc_ref, dskip_ref, nw_ref, expand_ref,
                o_ref, cx_ref, cbc_ref, state_ref):
    L = CHUNK
    hpg = SSD_HEADS // SSD_GROUPS
    gw = hpg * SSD_HEAD_DIM

    @pl.when(pl.program_id(0) == 0)
    def _():
        cx_ref[...] = jnp.zeros_like(cx_ref)
        cbc_ref[...] = jnp.zeros_like(cbc_ref)
        state_ref[...] = jnp.zeros_like(state_ref)

    def conv(raw, carry_ref, w_ref, b_ref):
        carry = carry_ref[...]
        w = w_ref[...]
        y = (w[0:1] * _shift_rows(raw, carry, 3) + w[1:2] * _shift_rows(raw, carry, 2)
             + w[2:3] * _shift_rows(raw, carry, 1) + w[3:4] * raw + b_ref[...])
        carry_ref[...] = raw[L - 8:]
        return _silu(y)

    xs = conv(x_ref[...], cx_ref, cwx_ref, cbx_ref)
    bcs = conv(bc_ref[...], cbc_ref, cwb_ref, cbb_ref)

    def softplus(v):
        return jnp.maximum(v, 0.0) + jnp.log(1.0 + jnp.exp(-jnp.abs(v)))

    dt = softplus(dt_ref[...] + dtb_ref[...])
    da = dt * (-jnp.exp(alog_ref[...]))
    row = lax.broadcasted_iota(jnp.int32, (L, L), 0)
    col = lax.broadcasted_iota(jnp.int32, (L, L), 1)
    causal = row >= col
    tril = causal.astype(F32)
    a_cum = jnp.dot(tril, da, precision=HIGHEST, preferred_element_type=F32)
    dt_r = softplus(dtt_ref[...] + dtbc_ref[...])
    da_r = dt_r * (-jnp.exp(alogc_ref[...]))
    a_cum_r = jnp.dot(da_r, (row <= col).astype(F32), precision=HIGHEST,
                      preferred_element_type=F32)

    expand = expand_ref[...]

    def widen(v):
        return jnp.dot(v, expand, precision=HIGHEST, preferred_element_type=F32)

    a_last = a_cum[L - 1:L]
    dt_w = widen(dt)
    decay_in_w = widen(jnp.exp(a_cum))
    decay_end_w = widen(jnp.exp(a_last - a_cum))
    chunk_decay_w = widen(jnp.broadcast_to(jnp.exp(a_last), (8, LANES)))[0:1]
    xdt = xs * dt_w
    xdt_b = xdt.astype(BF16)
    xdt_end_b = (xdt * decay_end_w).astype(BF16)
    lane_lo = lax.broadcasted_iota(jnp.int32, (L, LANES), 1) < SSD_HEAD_DIM

    ys = []
    for g in range(SSD_GROUPS):
        b_g = bcs[:, g * SSD_STATE:(g + 1) * SSD_STATE].astype(BF16)
        c_g = bcs[:, (SSD_GROUPS + g) * SSD_STATE:(SSD_GROUPS + g + 1) * SSD_STATE].astype(BF16)
        cb = _dot_nt(c_g, b_g)
        pieces = []
        for pair in range(hpg // 2):
            outs = []
            for k in range(2):
                h = g * hpg + pair * 2 + k
                diff = a_cum[:, h:h + 1] - a_cum_r[h:h + 1, :]
                w = (cb * jnp.exp(jnp.where(causal, diff, NEG))).astype(BF16)
                lo = (g * hpg + pair * 2) * SSD_HEAD_DIM
                outs.append(_dot(w, xdt_b[:, lo:lo + LANES]))
            pieces.append(jnp.where(lane_lo, outs[0], outs[1]))
        y_diag = jnp.concatenate(pieces, axis=1)
        state = state_ref[g]
        sl = slice(g * gw, (g + 1) * gw)
        y_off = _dot(c_g, state.astype(BF16)) * decay_in_w[:, sl]
        state_ref[g] = state * chunk_decay_w[:, sl] + _dot_tn(b_g, xdt_end_b[:, sl])
        ys.append(y_diag + y_off)
    y = jnp.concatenate(ys, axis=1) + xs * dskip_ref[...]
    yg = y * _silu(z_ref[...])
    half = SSD_INNER // SSD_GROUPS
    outs = []
    for g in range(SSD_GROUPS):
        part = yg[:, g * half:(g + 1) * half]
        outs.append(part * lax.rsqrt(jnp.mean(part * part, axis=-1, keepdims=True) + LN_EPS))
    o_ref[...] = (jnp.concatenate(outs, axis=1) * nw_ref[...]).astype(BF16)


def _ssd(p, dt_t, conv_w, conv_b, dt_bias, a_log, d_skip, norm_w, col):
    t = p.shape[0]
    L = CHUNK
    pad16 = lambda v: jnp.pad(v.astype(F32), (0, LANES - SSD_HEADS)).reshape(1, LANES)
    expand = (jnp.arange(LANES)[:, None] == (jnp.arange(SSD_INNER) // SSD_HEAD_DIM)[None, :]).astype(F32)
    full = lambda shape: pl.BlockSpec(shape, lambda c: (0,) * len(shape))
    return pl.pallas_call(
        _ssd_kernel,
        out_shape=jax.ShapeDtypeStruct((t, SSD_INNER), BF16),
        grid=(t // L,),
        in_specs=[pl.BlockSpec((L, 1024), lambda c: (c, col["z"])),
                  pl.BlockSpec((L, 1024), lambda c: (c, col["x"])),
                  pl.BlockSpec((L, 512), lambda c: (c, col["bc"])),
                  pl.BlockSpec((L, LANES), lambda c: (c, col["dt"])),
                  pl.BlockSpec((SSD_HEADS, L), lambda c: (0, c)),
                  full((SSD_CONV, 1024)), full((1, 1024)), full((SSD_CONV, 512)), full((1, 512)),
                  full((1, LANES)), full((1, LANES)), full((SSD_HEADS, 1)), full((SSD_HEADS, 1)),
                  full((1, 1024)), full((1, 1024)), full((LANES, 1024))],
        out_specs=pl.BlockSpec((L, SSD_INNER), lambda c: (c, 0)),
        scratch_shapes=[pltpu.VMEM((8, 1024), F32), pltpu.VMEM((8, 512), F32),
                        pltpu.VMEM((SSD_GROUPS, SSD_STATE, 512), F32)],
        compiler_params=_params("arbitrary"),
        name="ssd_mixer",
    )(p, p, p, p, dt_t,
      conv_w[:, :SSD_INNER], conv_b[:SSD_INNER].reshape(1, -1),
      conv_w[:, SSD_INNER:], conv_b[SSD_INNER:].reshape(1, -1),
      pad16(dt_bias), pad16(a_log), dt_bias.reshape(-1, 1), a_log.reshape(-1, 1),
      jnp.repeat(d_skip, SSD_HEAD_DIM).reshape(1, -1), norm_w.reshape(1, -1), expand)


def _ret_kernel(logg_ref, q_ref, k_ref, v_ref, g_ref, cos_ref, sin_ref, o_ref, state_ref):
    C = CHUNK
    h = pl.program_id(0)

    @pl.when(pl.program_id(1) == 0)
    def _():
        state_ref[...] = jnp.zeros_like(state_ref)

    lg = logg_ref[h]
    cos = cos_ref[...]
    sin = sin_ref[...]

    def rot(x):
        return x * cos + pltpu.roll(x, HEAD_DIM // 2, 1) * sin

    q = rot(q_ref[...])
    k = rot(k_ref[...]) * (HEAD_DIM ** -0.5)
    v = v_ref[...].astype(BF16)
    row = lax.broadcasted_iota(jnp.int32, (C, C), 0)
    col = lax.broadcasted_iota(jnp.int32, (C, C), 1)
    rel = row - col
    decay = jnp.where(rel >= 0, jnp.exp(lg * jnp.maximum(rel, 0).astype(F32)), 0.0)
    inner = _dot_nt(q.astype(BF16), k.astype(BF16)) * decay
    y = _dot(inner.astype(BF16), v)
    pos = lax.broadcasted_iota(jnp.int32, (C, 1), 0).astype(F32)
    q_dec = jnp.exp(lg * (pos + 1.0))
    k_dec = jnp.exp(lg * (C - 1.0 - pos))
    state = state_ref[...]
    y = y + _dot((q * q_dec).astype(BF16), state.astype(BF16))
    chunk_dec = jnp.exp(lg * jnp.full((1, HEAD_DIM), float(C), F32))
    state_ref[...] = state * chunk_dec + _dot_tn((k * k_dec).astype(BF16), v)
    mu = jnp.mean(y, axis=-1, keepdims=True)
    yc = y - mu
    var = jnp.mean(yc * yc, axis=-1, keepdims=True)
    o_ref[...] = (_silu(g_ref[...]) * (yc * lax.rsqrt(var + LN_EPS))).astype(BF16)


def _retention(p, col):
    t = p.shape[0]
    C = CHUNK
    half = HEAD_DIM // 2
    inv = 1.0 / (ROPE_BASE ** (jnp.arange(0, HEAD_DIM, 2, dtype=F32) / HEAD_DIM))
    ang = jnp.arange(t, dtype=F32)[:, None] * inv[None, :]
    cos = jnp.concatenate([jnp.cos(ang), jnp.cos(ang)], axis=1)
    sin = jnp.concatenate([-jnp.sin(ang), jnp.sin(ang)], axis=1)
    log_g = jnp.log(1.0 - 2.0 ** (-5.0 - jnp.arange(RET_HEADS, dtype=F32)))
    blk = lambda name: pl.BlockSpec((C, HEAD_DIM), lambda h, c, lg: (c, col[name] + h))
    tab = pl.BlockSpec((C, HEAD_DIM), lambda h, c, lg: (c, 0))
    return pl.pallas_call(
        _ret_kernel,
        out_shape=jax.ShapeDtypeStruct((t, RET_HEADS * HEAD_DIM), BF16),
        grid_spec=pltpu.PrefetchScalarGridSpec(
            num_scalar_prefetch=1,
            grid=(RET_HEADS, t // C),
            in_specs=[blk("rq"), blk("rk"), blk("rv"), blk("rg"), tab, tab],
            out_specs=pl.BlockSpec((C, HEAD_DIM), lambda h, c, lg: (c, h)),
            scratch_shapes=[pltpu.VMEM((HEAD_DIM, HEAD_DIM), F32)]),
        compiler_params=_params("parallel", "arbitrary"),
        name="retention_mixer",
    )(log_g, p, p, p, p, cos, sin)


def _rel_bucket(dist):
    n = jnp.maximum(dist, 0)
    max_exact = REL_BUCKETS // 2
    nf = jnp.maximum(n, max_exact).astype(F32)
    large = max_exact + (jnp.log(nf / max_exact) / math.log(REL_MAX_DISTANCE / max_exact)
                         * (REL_BUCKETS - max_exact)).astype(jnp.int32)
    large = jnp.minimum(large, REL_BUCKETS - 1)
    return jnp.where(n < max_exact, n, large)


def _bias_lookup(tab_ref, bucket, h):
    acc = jnp.full(bucket.shape, tab_ref[h], F32)
    for b in range(1, REL_BUCKETS):
        acc = jnp.where(bucket == b, tab_ref[b * ATTN_HEADS + h], acc)
    return acc


def _bias_tile_kernel(tab_ref, o_ref, *, window):
    h = pl.program_id(0)
    o = pl.program_id(1)
    row = lax.broadcasted_iota(jnp.int32, (CHUNK, CHUNK), 0)
    col = lax.broadcasted_iota(jnp.int32, (CHUNK, CHUNK), 1)
    d = o * CHUNK + col - row
    valid = d >= 0
    if window is not None:
        valid = valid & (d < window)
    o_ref[0, 0] = jnp.where(valid, _bias_lookup(tab_ref, _rel_bucket(d), h), NEG)


def _bias_tiles(rel_bias, n_off, window=None):
    return pl.pallas_call(
        functools.partial(_bias_tile_kernel, window=window),
        out_shape=jax.ShapeDtypeStruct((ATTN_HEADS, n_off, CHUNK, CHUNK), F32),
        grid_spec=pltpu.PrefetchScalarGridSpec(
            num_scalar_prefetch=1, grid=(ATTN_HEADS, n_off), in_specs=[],
            out_specs=pl.BlockSpec((1, 1, CHUNK, CHUNK), lambda h, o, tab: (h, o, 0, 0))),
        compiler_params=_params("parallel", "parallel"),
        name="rel_bias_tiles",
    )(rel_bias.reshape(-1))


def _cmp_bias_kernel(tab_ref, o_ref, *, width):
    h = pl.program_id(0)
    row = lax.broadcasted_iota(jnp.int32, (CHUNK, 2 * width), 0)
    col = lax.broadcasted_iota(jnp.int32, (CHUNK, 2 * width), 1)
    d = row - NSA_CMP_STRIDE * (col - width) - (NSA_CMP_BLOCK - 1)
    o_ref[0] = jnp.where(d >= 0, _bias_lookup(tab_ref, _rel_bucket(d), h), NEG)


def _cmp_bias_base(rel_bias, width):
    return pl.pallas_call(
        functools.partial(_cmp_bias_kernel, width=width),
        out_shape=jax.ShapeDtypeStruct((ATTN_HEADS, CHUNK, 2 * width), F32),
        grid_spec=pltpu.PrefetchScalarGridSpec(
            num_scalar_prefetch=1, grid=(ATTN_HEADS,), in_specs=[],
            out_specs=pl.BlockSpec((1, CHUNK, 2 * width), lambda h, tab: (h, 0, 0))),
        compiler_params=_params("parallel"),
        name="cmp_bias_base",
    )(rel_bias.reshape(-1))


def _kprep_kernel(k_ref, v_ref, ka_ref, vt_ref, *rest, heads, block, with_mean):
    i = pl.program_id(0)
    k = k_ref[...]
    v = v_ref[...]
    if block is not None:
        row = lax.broadcasted_iota(jnp.int32, (CHUNK, LANES), 0)
        lane = lax.broadcasted_iota(jnp.int32, (CHUNK, LANES), 1)
        onehot = (lane == (i * CHUNK + row) // block).astype(BF16)
    for h in range(heads):
        sl = slice(h * HEAD_DIM, (h + 1) * HEAD_DIM)
        kh = k[:, sl].astype(BF16)
        ka_ref[h, 0] = kh if block is None else jnp.concatenate([kh, onehot], axis=1)
        vt_ref[h, 0] = v[:, sl].T.astype(BF16)
    if with_mean:
        km_ref = rest[0]

        @pl.when(i == 0)
        def _():
            km_ref[...] = jnp.zeros_like(km_ref)

        km_ref[pl.ds(i, 1), :] = jnp.mean(k, axis=0, keepdims=True)


def _kprep(p, kcol, vcol, heads, block, with_mean=False):
    t = p.shape[0]
    nt = t // CHUNK
    w = heads * HEAD_DIM
    kw = HEAD_DIM if block is None else 2 * HEAD_DIM
    out_shape = [jax.ShapeDtypeStruct((heads, nt, CHUNK, kw), BF16),
                 jax.ShapeDtypeStruct((heads, nt, HEAD_DIM, CHUNK), BF16)]
    out_specs = [pl.BlockSpec((heads, 1, CHUNK, kw), lambda i: (0, i, 0, 0)),
                 pl.BlockSpec((heads, 1, HEAD_DIM, CHUNK), lambda i: (0, i, 0, 0))]
    if with_mean:
        out_shape.append(jax.ShapeDtypeStruct((LANES, w), F32))
        out_specs.append(pl.BlockSpec((LANES, w), lambda i: (0, 0)))
    return pl.pallas_call(
        functools.partial(_kprep_kernel, heads=heads, block=block, with_mean=with_mean),
        out_shape=tuple(out_shape),
        grid=(t // CHUNK,),
        in_specs=[pl.BlockSpec((CHUNK, w), lambda i: (i, kcol)),
                  pl.BlockSpec((CHUNK, w), lambda i: (i, vcol))],
        out_specs=tuple(out_specs),
        compiler_params=_params("arbitrary"),
        name="key_prep",
    )(p, p)


def _topk_mask(score, k):
    lane = lax.broadcasted_iota(jnp.int32, score.shape, 1)

    def body(_, carry):
        g, sel = carry
        m = jnp.max(g, axis=1, keepdims=True)
        idx = jnp.min(jnp.where(g == m, lane, LANES), axis=1, keepdims=True)
        hit = lane == idx
        sel = jnp.where(hit & (m > -0.5 * BIG), 1.0, sel)
        return jnp.where(hit, -BIG, g), sel

    _, sel = lax.fori_loop(0, k, body, (score, jnp.zeros(score.shape, F32)))
    return sel > 0.5


def _moba_select_kernel(q_ref, km_ref, qa_ref):
    i = pl.program_id(0)
    q = q_ref[...]
    km = km_ref[...]
    lane = lax.broadcasted_iota(jnp.int32, (CHUNK, LANES), 1)
    for h in range(ATTN_HEADS):
        sl = slice(h * HEAD_DIM, (h + 1) * HEAD_DIM)
        gate = lax.dot_general(q[:, sl], km[:, sl], (((1,), (1,)), ((), ())),
                               precision=HIGHEST, preferred_element_type=F32)
        sel = _topk_mask(jnp.where(lane < i, gate, -BIG), MOBA_TOPK)
        pen = jnp.where(sel | (lane == i), 0.0, NEG)
        qa_ref[h, 0] = jnp.concatenate([q[:, sl], pen], axis=1).T.astype(BF16)


def _moba_select(p, qcol, kmean):
    t = p.shape[0]
    w = ATTN_HEADS * HEAD_DIM
    return pl.pallas_call(
        _moba_select_kernel,
        out_shape=jax.ShapeDtypeStruct((ATTN_HEADS, t // CHUNK, 2 * HEAD_DIM, CHUNK), BF16),
        grid=(t // CHUNK,),
        in_specs=[pl.BlockSpec((CHUNK, w), lambda i: (i, qcol)),
                  pl.BlockSpec((LANES, w), lambda i: (0, 0))],
        out_specs=pl.BlockSpec((ATTN_HEADS, 1, 2 * HEAD_DIM, CHUNK), lambda i: (0, i, 0, 0)),
        compiler_params=_params("parallel"),
        name="moba_select",
    )(p, kmean)


def _flash_kernel(q_ref, k_ref, vt_ref, bias_ref, *rest, group, shared_kv, span, combine):
    if combine:
        ocmp_ref, osel_ref, gate_ref, o_ref, m_ref, l_ref, acc_ref = rest
    else:
        o_ref, m_ref, l_ref, acc_ref = rest
    hg = pl.program_id(0)
    i = pl.program_id(1)
    n_off = bias_ref.shape[1]
    scale = HEAD_DIM ** -0.5
    head_cols = lambda g: slice(g * HEAD_DIM, (g + 1) * HEAD_DIM)
    if len(q_ref.shape) == 3:
        qts = [q_ref[g] for g in range(group)]
    else:
        qts = [q_ref[:, head_cols(g)].T.astype(BF16) for g in range(group)]
    m_ref[...] = jnp.full_like(m_ref, -BIG)
    l_ref[...] = jnp.zeros_like(l_ref)
    acc_ref[...] = jnp.zeros_like(acc_ref)

    def body(j, carry):
        off = jnp.minimum(i - j, n_off - 1)
        sts = [_dot(k_ref[j] if shared_kv else k_ref[g, j], qts[g]) for g in range(group)]
        pts, alphas = [], []
        for g in range(group):
            st = sts[g] * scale + bias_ref[g, off]
            m_prev = m_ref[g]
            m_new = jnp.maximum(m_prev, jnp.max(st, axis=0, keepdims=True))
            alpha = jnp.exp(m_prev - m_new)
            pt = jnp.exp(st - m_new)
            l_ref[g] = alpha * l_ref[g] + jnp.sum(pt, axis=0, keepdims=True)
            m_ref[g] = m_new
            pts.append(pt.astype(BF16))
            alphas.append(alpha)
        for g in range(group):
            vt = vt_ref[j] if shared_kv else vt_ref[g, j]
            acc_ref[g] = alphas[g] * acc_ref[g] + _dot(vt, pts[g])
        return carry

    j_lo = 0 if span is None else jnp.maximum(i - (span - 1), 0)
    lax.fori_loop(j_lo, i + 1, body, 0)
    if combine:
        sig = 1.0 / (1.0 + jnp.exp(-gate_ref[...]))
        lane = lax.broadcasted_iota(jnp.int32, sig.shape, 1)
    for g in range(group):
        out = (acc_ref[g] * (1.0 / l_ref[g])).T
        if combine:
            h = hg * group + g
            pick = lambda c: jnp.sum(jnp.where(lane == 3 * h + c, sig, 0.0), axis=1, keepdims=True)
            out = pick(0) * ocmp_ref[:, head_cols(g)] + pick(1) * osel_ref[:, head_cols(g)] + pick(2) * out
        o_ref[:, head_cols(g)] = out.astype(o_ref.dtype)


def _flash(q, k, vt, bias, *, heads, group, shared_kv, out_dtype, span=None, combine=None):
    n_off = bias.shape[1]
    nt = k.shape[1]
    gw = group * HEAD_DIM
    if isinstance(q, tuple):
        q, base = q
        q_spec = pl.BlockSpec((CHUNK, gw), lambda hg, i: (i, base // group + hg))
    else:
        q_spec = pl.BlockSpec((group, None) + q.shape[2:], lambda hg, i: (hg, i, 0, 0))
    kv_block = (None,) if shared_kv else (group,)
    resident = lambda shape: pl.BlockSpec(shape, lambda hg, i: (hg, 0, 0, 0), pipeline_mode=pl.Buffered(1))
    in_specs = [q_spec, resident(kv_block + k.shape[1:]), resident(kv_block + vt.shape[1:]),
                resident((group, n_off, CHUNK, CHUNK))]
    args = [q, k, vt, bias]
    tile = pl.BlockSpec((CHUNK, gw), lambda hg, i: (i, hg))
    if combine is not None:
        ocmp, osel, p, gcol = combine
        in_specs += [tile, tile, pl.BlockSpec((CHUNK, LANES), lambda hg, i: (i, gcol))]
        args += [ocmp, osel, p]
    return pl.pallas_call(
        functools.partial(_flash_kernel, group=group, shared_kv=shared_kv, span=span,
                          combine=combine is not None),
        out_shape=jax.ShapeDtypeStruct((nt * CHUNK, heads * HEAD_DIM), out_dtype),
        grid=(heads // group, nt),
        in_specs=in_specs,
        out_specs=tile,
        scratch_shapes=[pltpu.VMEM((group, 1, CHUNK), F32), pltpu.VMEM((group, 1, CHUNK), F32),
                        pltpu.VMEM((group, HEAD_DIM, CHUNK), F32)],
        compiler_params=_params("parallel", "arbitrary"),
        name="flash_attention",
    )(*args)


def _compress_kernel(x_ref, pe_ref, w1_ref, w2_ref, o_ref):
    x = x_ref[0, 0]
    pe = pe_ref[0]
    half = NSA_CMP_STRIDE * HEAD_DIM
    a = _dot((x + pe[0:1]).astype(BF16), w1_ref[0, :half, :])
    b = _dot((x + pe[1:2]).astype(BF16), w1_ref[0, half:, :])
    hid = a + pltpu.roll(b, x.shape[0] - 1, 0)
    gelu = 0.5 * hid * (1.0 + jnp.tanh(math.sqrt(2.0 / math.pi) * (hid + 0.044715 * hid * hid * hid)))
    o_ref[0, 0] = _dot(gelu.astype(BF16), w2_ref[0])


def _compress(x2, pe, w1, w2):
    _, hkv, w, kd = x2.shape
    return pl.pallas_call(
        _compress_kernel,
        out_shape=jax.ShapeDtypeStruct((2, hkv, w, HEAD_DIM), F32),
        grid=(2, hkv),
        in_specs=[pl.BlockSpec((1, 1, w, kd), lambda a, h: (a, h, 0, 0)),
                  pl.BlockSpec((1, 2, kd), lambda a, h: (a, 0, 0)),
                  pl.BlockSpec((1, 2 * kd, HEAD_DIM), lambda a, h: (a, 0, 0)),
                  pl.BlockSpec((1, HEAD_DIM, HEAD_DIM), lambda a, h: (a, 0, 0))],
        out_specs=pl.BlockSpec((1, 1, w, HEAD_DIM), lambda a, h: (a, h, 0, 0)),
        compiler_params=_params("parallel", "parallel"),
        name="nsa_compress",
    )(x2, pe, w1, w2)


def _nsa_cmp_kernel(q_ref, kv_ref, base_ref, ov_ref, ocmp_ref, qa_ref):
    i = pl.program_id(0)
    width = kv_ref.shape[2]
    scale = HEAD_DIM ** -0.5
    q = q_ref[...]
    shift = (NSA_CMP_STRIDE * i + width) % (2 * width)
    lane = lax.broadcasted_iota(jnp.int32, (CHUNK, LANES), 1)
    cur = (i * CHUNK + lax.broadcasted_iota(jnp.int32, (CHUNK, LANES), 0)) // NSA_SEL_BLOCK
    forced = (lane == 0) | (lane == cur) | (lane == cur - 1)
    ov = ov_ref[...]
    for kvh in range(NSA_KV_HEADS):
        kc = kv_ref[0, kvh].astype(BF16)
        vc = kv_ref[1, kvh].astype(BF16)
        imp = jnp.zeros((CHUNK, LANES), F32)
        for g in range(NSA_GROUP):
            h = kvh * NSA_GROUP + g
            sl = slice(h * HEAD_DIM, (h + 1) * HEAD_DIM)
            bias = pltpu.roll(base_ref[h], shift, 1)[:, :width]
            s = _dot_nt(q[:, sl].astype(BF16), kc) * scale + bias
            m = jnp.max(s, axis=1, keepdims=True)
            e = jnp.exp(s - m)
            l = jnp.sum(e, axis=1, keepdims=True)
            pr = e * jnp.where(m > 0.1 * NEG, 1.0 / l, 0.0)
            ocmp_ref[:, sl] = _dot(pr.astype(BF16), vc)
            imp = imp + jnp.dot(pr, ov, precision=HIGHEST, preferred_element_type=F32)
        score = jnp.where(forced, BIG, jnp.where(lane <= cur, imp, -BIG))
        pen_t = jnp.where(_topk_mask(score, NSA_SEL_TOPN), 0.0, NEG).T.astype(BF16)
        for g in range(NSA_GROUP):
            h = kvh * NSA_GROUP + g
            sl = slice(h * HEAD_DIM, (h + 1) * HEAD_DIM)
            qa_ref[h, 0] = jnp.concatenate([q[:, sl].T.astype(BF16), pen_t], axis=0)


def _nsa_cmp(p, qcol, kv_cmp, base):
    t = p.shape[0]
    width = kv_cmp.shape[2]
    w = ATTN_HEADS * HEAD_DIM
    n = jnp.arange(width)[:, None] * NSA_CMP_STRIDE
    j = jnp.arange(LANES)[None, :] * NSA_SEL_BLOCK
    overlap = ((n < j + NSA_SEL_BLOCK) & (n + NSA_CMP_BLOCK > j)
               & (jnp.arange(width)[:, None] < width - 1)).astype(F32)
    return pl.pallas_call(
        _nsa_cmp_kernel,
        out_shape=(jax.ShapeDtypeStruct((t, w), F32),
                   jax.ShapeDtypeStruct((ATTN_HEADS, t // CHUNK, 2 * HEAD_DIM, CHUNK), BF16)),
        grid=(t // CHUNK,),
        in_specs=[pl.BlockSpec((CHUNK, w), lambda i: (i, qcol)),
                  pl.BlockSpec(kv_cmp.shape, lambda i: (0, 0, 0, 0)),
                  pl.BlockSpec(base.shape, lambda i: (0, 0, 0)),
                  pl.BlockSpec((width, LANES), lambda i: (0, 0))],
        out_specs=(pl.BlockSpec((CHUNK, w), lambda i: (i, 0)),
                   pl.BlockSpec((ATTN_HEADS, 1, 2 * HEAD_DIM, CHUNK), lambda i: (0, i, 0, 0))),
        compiler_params=_params("parallel"),
        name="nsa_cmp_select",
    )(p, kv_cmp, base, overlap)


_EVEN_ORDER = np.concatenate([np.arange(0, 2048), np.arange(2576, 5648), np.arange(2048, 2576)])
_EVEN_WIDTH = 5760
_EVEN_TN = 640
_DEINTERLEAVE = np.concatenate([np.arange(0, HEAD_DIM, 2), np.arange(1, HEAD_DIM, 2)])
_RET_PERM = (np.arange(RET_HEADS)[:, None] * HEAD_DIM + _DEINTERLEAVE[None, :]).reshape(-1)
_ODD_ORDER = np.concatenate([np.arange(0, 1024), 2584 + _RET_PERM, 3608 + _RET_PERM, np.arange(4632, 6680),
                             np.arange(1024, 2584)])
_ODD_WIDTH = 6912
_ODD_TN = 768


def _reorder_cols(w, order, width):
    idx = np.concatenate([order, np.full(width - len(order), w.shape[1])]).astype(np.int32)
    return jnp.pad(w, ((0, 0), (0, 1)))[:, idx].astype(BF16)


def _even_mixer(hb, w_in, conv_w, conv_b, dt_bias, a_log, d_skip, norm_w, bias_c):
    p = _matmul(hb, _reorder_cols(w_in, _EVEN_ORDER, _EVEN_WIDTH), _EVEN_TN)
    dt_t = p[:, 5632:5632 + SSD_HEADS].T
    y_ssd = _ssd(p, dt_t, conv_w, conv_b, dt_bias, a_log, d_skip, norm_w,
                 {"z": 0, "x": 1, "bc": 10, "dt": 44})
    k_aug, v_t, kmean = _kprep(p, 3, 4, ATTN_HEADS, MOBA_BLOCK, True)
    q_aug = _moba_select(p, 2, kmean)
    y_moba = _flash(q_aug, k_aug, v_t, bias_c, heads=ATTN_HEADS, group=4, shared_kv=False, out_dtype=BF16)
    return jnp.concatenate([y_ssd, y_moba], axis=1)


def _odd_mixer(hb, w_in, cmp_pe, cmp_w1, cmp_w2, bias_c, bias_w, cmp_base):
    p = _matmul(hb, _reorder_cols(w_in, _ODD_ORDER, _ODD_WIDTH), _ODD_TN)
    t = p.shape[0]
    x2 = p[:, 5120:5632].reshape(t, 2, NSA_KV_HEADS, HEAD_DIM).transpose(1, 2, 0, 3)
    x2 = x2.reshape(2, NSA_KV_HEADS, t // NSA_CMP_STRIDE, NSA_CMP_STRIDE * HEAD_DIM)
    kv_cmp = _compress(x2, cmp_pe.reshape(2, 2, NSA_CMP_STRIDE * HEAD_DIM),
                       cmp_w1.astype(BF16), cmp_w2.astype(BF16))
    o_cmp, q_aug = _nsa_cmp(p, 0, kv_cmp, cmp_base)
    k_aug, v_t = _kprep(p, 22, 23, NSA_KV_HEADS, NSA_SEL_BLOCK)
    o_sel = _flash(q_aug, k_aug, v_t, bias_c, heads=ATTN_HEADS, group=NSA_GROUP, shared_kv=True, out_dtype=F32)
    k_win, v_win_t = _kprep(p, 24, 25, NSA_KV_HEADS, None)
    y_nsa = _flash((p, 0), k_win, v_win_t, bias_w, heads=ATTN_HEADS, group=NSA_GROUP, shared_kv=True,
                   out_dtype=BF16, span=NSA_WINDOW // CHUNK + 1, combine=(o_cmp, o_sel, p, 52))
    y_ret = _retention(p, {"rq": 8, "rk": 16, "rv": 24, "rg": 32})
    return jnp.concatenate([y_nsa, y_ret], axis=1)


def kernel(x, rel_bias, ev_w_in, ev_conv_w, ev_conv_b, ev_dt_bias, ev_a_log, ev_d_skip, ev_norm_w, ev_w_out,
           od_w_in, od_cmp_pe, od_cmp_w1, od_cmp_w2, od_w_out,
           ffn_w_up, ffn_conv_w, ffn_conv_b, ffn_w_down, ln_g, ln_b):
    _, t0, _ = x.shape
    t = -(-t0 // CHUNK) * CHUNK
    h = jnp.pad(x[0], ((0, t - t0), (0, 0)))
    hb = h.astype(BF16)
    bias_c = _bias_tiles(rel_bias, 8)
    bias_w = _bias_tiles(rel_bias, NSA_WINDOW // CHUNK + 1, window=NSA_WINDOW)
    cmp_base = _cmp_bias_base(rel_bias, t // NSA_CMP_STRIDE)
    for layer in range(DEPTH):
        i = layer // 2
        if layer % 2 == 0:
            mix = _even_mixer(hb, ev_w_in[i], ev_conv_w[i], ev_conv_b[i], ev_dt_bias[i], ev_a_log[i],
                              ev_d_skip[i], ev_norm_w[i], bias_c)
            w_out = ev_w_out[i]
        else:
            mix = _odd_mixer(hb, od_w_in[i], od_cmp_pe[i], od_cmp_w1[i], od_cmp_w2[i], bias_c, bias_w, cmp_base)
            w_out = od_w_out[i]
        h, hb = _matmul_res_ln(mix, w_out.astype(BF16), h, ln_g[layer, 0], ln_b[layer, 0], 512)
        gated = _ffn_up(hb, ffn_w_up[layer].astype(BF16), ffn_conv_w[layer], ffn_conv_b[layer])
        h, hb = _matmul_res_ln(gated, ffn_w_down[layer].astype(BF16), h, ln_g[layer, 1], ln_b[layer, 1], 512)
    return h[None, :t0]
```

```python
import functools
import math

import numpy as np
import jax
import jax.numpy as jnp
from jax import lax
from jax.experimental import pallas as pl
from jax.experimental.pallas import tpu as pltpu

F32 = jnp.float32
BF16 = jnp.bfloat16
HIGHEST = lax.Precision.HIGHEST

D_MODEL = 2048
DEPTH = 4
HEAD_DIM = 128
ATTN_HEADS = 8
REL_BUCKETS = 32
REL_MAX_DISTANCE = 2048
SSD_HEADS = 16
SSD_HEAD_DIM = 64
SSD_INNER = 1024
SSD_GROUPS = 2
SSD_STATE = 128
SSD_CONV = 4
MOBA_BLOCK = 256
MOBA_TOPK = 3
NSA_KV_HEADS = 2
NSA_GROUP = 4
NSA_CMP_BLOCK = 32
NSA_CMP_STRIDE = 16
NSA_SEL_BLOCK = 64
NSA_SEL_TOPN = 16
NSA_WINDOW = 512
RET_HEADS = 8
ROPE_BASE = 10000.0
FFN_DIM = 5632
FFN_CONV = 3
LN_EPS = 1e-5
NEG = -1e30
BIG = 3e38
LOG2E = math.log2(math.e)
_QK_SCALE = HEAD_DIM ** -0.5 * LOG2E
DEEPNORM_ALPHA = (2 * DEPTH) ** 0.25

CHUNK = 256
LANES = 128
VMEM_LIMIT = 56 * 1024 * 1024


def _params(*sem):
    return pltpu.CompilerParams(dimension_semantics=sem, vmem_limit_bytes=VMEM_LIMIT)


def _silu(x):
    return x * (1.0 / (1.0 + jnp.exp(-x)))


def _dot(a, b):
    return jnp.dot(a, b, preferred_element_type=F32)


def _dot_nt(a, b):
    return lax.dot_general(a, b, (((1,), (1,)), ((), ())), preferred_element_type=F32)


def _dot_tn(a, b):
    return lax.dot_general(a, b, (((0,), (0,)), ((), ())), preferred_element_type=F32)


def _split_bf16(a):
    hi = a.astype(BF16)
    return hi, (a - hi.astype(F32)).astype(BF16)


def _dot_nt_split(a, b):
    ah, al = _split_bf16(a)
    bh, bl = _split_bf16(b)
    return _dot_nt(ah, bh) + (_dot_nt(ah, bl) + _dot_nt(al, bh))


def _shift_rows(a, carry, k):
    rolled = pltpu.roll(a, k, 0)
    head = jnp.where(lax.broadcasted_iota(jnp.int32, carry.shape, 0) < k,
                     pltpu.roll(carry, k, 0), rolled[:8])
    return jnp.concatenate([head, rolled[8:]], axis=0)


def _mm_kernel(x_ref, w_ref, o_ref):
    o_ref[...] = _dot(x_ref[...], w_ref[...]).astype(o_ref.dtype)


def _matmul(x, w, tn, out_dtype=F32):
    m, k = x.shape
    n = w.shape[1]
    tm = min(1024, m)
    return pl.pallas_call(
        _mm_kernel,
        out_shape=jax.ShapeDtypeStruct((m, n), out_dtype),
        grid=(m // tm, n // tn),
        in_specs=[pl.BlockSpec((tm, k), lambda i, j: (i, 0)),
                  pl.BlockSpec((k, tn), lambda i, j: (0, j))],
        out_specs=pl.BlockSpec((tm, tn), lambda i, j: (i, j)),
        compiler_params=_params("parallel", "arbitrary"),
        name="proj_matmul",
    )(x, w)


def _mm_ln_kernel(a_ref, w_ref, h_ref, g_ref, b_ref, o_ref, ob_ref, acc_ref):
    k = pl.program_id(1)

    @pl.when(k == 0)
    def _():
        acc_ref[...] = jnp.zeros_like(acc_ref)

    acc_ref[...] += _dot(a_ref[...], w_ref[...])

    @pl.when(k == pl.num_programs(1) - 1)
    def _():
        y = DEEPNORM_ALPHA * h_ref[...] + acc_ref[...]
        mu = jnp.mean(y, axis=-1, keepdims=True)
        yc = y - mu
        var = jnp.mean(yc * yc, axis=-1, keepdims=True)
        out = yc * lax.rsqrt(var + LN_EPS) * g_ref[...] + b_ref[...]
        o_ref[...] = out
        ob_ref[...] = out.astype(BF16)


def _matmul_res_ln(a, w, h, g, b, tk):
    m, k = a.shape
    n = w.shape[1]
    tm = min(512, m)
    return pl.pallas_call(
        _mm_ln_kernel,
        out_shape=(jax.ShapeDtypeStruct((m, n), F32), jax.ShapeDtypeStruct((m, n), BF16)),
        grid=(m // tm, k // tk),
        in_specs=[pl.BlockSpec((tm, tk), lambda i, j: (i, j)),
                  pl.BlockSpec((tk, n), lambda i, j: (j, 0)),
                  pl.BlockSpec((tm, n), lambda i, j: (i, 0)),
                  pl.BlockSpec((1, n), lambda i, j: (0, 0)),
                  pl.BlockSpec((1, n), lambda i, j: (0, 0))],
        out_specs=(pl.BlockSpec((tm, n), lambda i, j: (i, 0)),
                   pl.BlockSpec((tm, n), lambda i, j: (i, 0))),
        scratch_shapes=[pltpu.VMEM((tm, n), F32)],
        compiler_params=_params("parallel", "arbitrary"),
        name="matmul_res_ln",
    )(a, w, h, g.reshape(1, n), b.reshape(1, n))


def _ffn_up_kernel(x_ref, wa_ref, wu_ref, cw_ref, cb_ref, o_ref, carry_ref):
    @pl.when(pl.program_id(1) == 0)
    def _():
        carry_ref[...] = jnp.zeros_like(carry_ref)

    x = x_ref[...]
    a = _dot(x, wa_ref[...])
    u = _dot(x, wu_ref[...])
    carry = carry_ref[...]
    cw = cw_ref[...]
    conv = (cw[0:1] * _shift_rows(a, carry, 2) + cw[1:2] * _shift_rows(a, carry, 1)
            + cw[2:3] * a + cb_ref[...])
    carry_ref[...] = a[a.shape[0] - 8:]
    o_ref[...] = (_silu(conv) * u).astype(BF16)


def _ffn_up(x, w_up, conv_w, conv_b):
    m, k = x.shape
    tn = 512
    tm = min(1024, m)
    nj = FFN_DIM // tn
    return pl.pallas_call(
        _ffn_up_kernel,
        out_shape=jax.ShapeDtypeStruct((m, FFN_DIM), BF16),
        grid=(nj, m // tm),
        in_specs=[pl.BlockSpec((tm, k), lambda j, i: (i, 0)),
                  pl.BlockSpec((k, tn), lambda j, i: (0, j)),
                  pl.BlockSpec((k, tn), lambda j, i: (0, j + nj)),
                  pl.BlockSpec((FFN_CONV, tn), lambda j, i: (0, j)),
                  pl.BlockSpec((1, tn), lambda j, i: (0, j))],
        out_specs=pl.BlockSpec((tm, tn), lambda j, i: (i, j)),
        scratch_shapes=[pltpu.VMEM((8, tn), F32)],
        compiler_params=_params("parallel", "arbitrary"),
        name="ffn_up_conv",
    )(x, w_up, w_up, conv_w, conv_b.reshape(1, FFN_DIM))


def _ssd_kernel(z_ref, x_ref, bc_ref, dt_ref, dtt_ref, cwx_ref, cbx_ref, cwb_ref, cbb_ref,
                dtb_ref, alog_ref, dtbc_ref, alogc_ref, dskip_ref, nw_ref, expand_ref,
                o_ref, cx_ref, cbc_ref, state_ref):
    L = CHUNK
    hpg = SSD_HEADS // SSD_GROUPS
    gw = hpg * SSD_HEAD_DIM

    @pl.when(pl.program_id(0) == 0)
    def _():
        cx_ref[...] = jnp.zeros_like(cx_ref)
        cbc_ref[...] = jnp.zeros_like(cbc_ref)
        state_ref[...] = jnp.zeros_like(state_ref)

    def conv(raw, carry_ref, w_ref, b_ref):
        carry = carry_ref[...]
        w = w_ref[...]
        y = (w[0:1] * _shift_rows(raw, carry, 3) + w[1:2] * _shift_rows(raw, carry, 2)
             + w[2:3] * _shift_rows(raw, carry, 1) + w[3:4] * raw + b_ref[...])
        carry_ref[...] = raw[L - 8:]
        return _silu(y)

    xs = conv(x_ref[...], cx_ref, cwx_ref, cbx_ref)
    bcs = conv(bc_ref[...], cbc_ref, cwb_ref, cbb_ref)

    def softplus(v):
        return jnp.maximum(v, 0.0) + jnp.log(1.0 + jnp.exp(-jnp.abs(v)))

    dt = softplus(dt_ref[...] + dtb_ref[...])
    da = dt * (-jnp.exp(alog_ref[...]))
    row = lax.broadcasted_iota(jnp.int32, (L, L), 0)
    col = lax.broadcasted_iota(jnp.int32, (L, L), 1)
    causal = row >= col
    tril = causal.astype(F32)
    a_cum = jnp.dot(tril, da, precision=HIGHEST, preferred_element_type=F32)
    dt_r = softplus(dtt_ref[...] + dtbc_ref[...])
    da_r = dt_r * (-jnp.exp(alogc_ref[...]))
    a_cum_r = jnp.dot(da_r, (row <= col).astype(F32), precision=HIGHEST,
                      preferred_element_type=F32)

    expand = expand_ref[...]

    def widen(v):
        return jnp.dot(v, expand, precision=HIGHEST, preferred_element_type=F32)

    a_last = a_cum[L - 1:L]
    dt_w = widen(dt)
    decay_in_w = widen(jnp.exp(a_cum))
    decay_end_w = widen(jnp.exp(a_last - a_cum))
    chunk_decay_w = widen(jnp.broadcast_to(jnp.exp(a_last), (8, LANES)))[0:1]
    xdt = xs * dt_w
    xdt_b = xdt.astype(BF16)
    xdt_end_b = (xdt * decay_end_w).astype(BF16)
    lane_lo = lax.broadcasted_iota(jnp.int32, (L, LANES), 1) < SSD_HEAD_DIM

    ys = []
    for g in range(SSD_GROUPS):
        b_g = bcs[:, g * SSD_STATE:(g + 1) * SSD_STATE].astype(BF16)
        c_g = bcs[:, (SSD_GROUPS + g) * SSD_STATE:(SSD_GROUPS + g + 1) * SSD_STATE].astype(BF16)
        cb = _dot_nt(c_g, b_g)
        pieces = []
        for pair in range(hpg // 2):
            outs = []
            for k in range(2):
                h = g * hpg + pair * 2 + k
                diff = a_cum[:, h:h + 1] - a_cum_r[h:h + 1, :]
                w = (cb * jnp.exp(jnp.where(causal, diff, NEG))).astype(BF16)
                lo = (g * hpg + pair * 2) * SSD_HEAD_DIM
                outs.append(_dot(w, xdt_b[:, lo:lo + LANES]))
            pieces.append(jnp.where(lane_lo, outs[0], outs[1]))
        y_diag = jnp.concatenate(pieces, axis=1)
        state = state_ref[g]
        sl = slice(g * gw, (g + 1) * gw)
        y_off = _dot(c_g, state.astype(BF16)) * decay_in_w[:, sl]
        state_ref[g] = state * chunk_decay_w[:, sl] + _dot_tn(b_g, xdt_end_b[:, sl])
        ys.append(y_diag + y_off)
    y = jnp.concatenate(ys, axis=1) + xs * dskip_ref[...]
    yg = y * _silu(z_ref[...])
    half = SSD_INNER // SSD_GROUPS
    outs = []
    for g in range(SSD_GROUPS):
        part = yg[:, g * half:(g + 1) * half]
        outs.append(part * lax.rsqrt(jnp.mean(part * part, axis=-1, keepdims=True) + LN_EPS))
    o_ref[...] = (jnp.concatenate(outs, axis=1) * nw_ref[...]).astype(BF16)


def _ssd(p, dt_t, conv_w, conv_b, dt_bias, a_log, d_skip, norm_w, col):
    t = p.shape[0]
    L = CHUNK
    pad16 = lambda v: jnp.pad(v.astype(F32), (0, LANES - SSD_HEADS)).reshape(1, LANES)
    expand = (jnp.arange(LANES)[:, None] == (jnp.arange(SSD_INNER) // SSD_HEAD_DIM)[None, :]).astype(F32)
    full = lambda shape: pl.BlockSpec(shape, lambda c: (0,) * len(shape))
    return pl.pallas_call(
        _ssd_kernel,
        out_shape=jax.ShapeDtypeStruct((t, SSD_INNER), BF16),
        grid=(t // L,),
        in_specs=[pl.BlockSpec((L, 1024), lambda c: (c, col["z"])),
                  pl.BlockSpec((L, 1024), lambda c: (c, col["x"])),
                  pl.BlockSpec((L, 512), lambda c: (c, col["bc"])),
                  pl.BlockSpec((L, LANES), lambda c: (c, col["dt"])),
                  pl.BlockSpec((SSD_HEADS, L), lambda c: (0, c)),
                  full((SSD_CONV, 1024)), full((1, 1024)), full((SSD_CONV, 512)), full((1, 512)),
                  full((1, LANES)), full((1, LANES)), full((SSD_HEADS, 1)), full((SSD_HEADS, 1)),
                  full((1, 1024)), full((1, 1024)), full((LANES, 1024))],
        out_specs=pl.BlockSpec((L, SSD_INNER), lambda c: (c, 0)),
        scratch_shapes=[pltpu.VMEM((8, 1024), F32), pltpu.VMEM((8, 512), F32),
                        pltpu.VMEM((SSD_GROUPS, SSD_STATE, 512), F32)],
        compiler_params=_params("arbitrary"),
        name="ssd_mixer",
    )(p, p, p, p, dt_t,
      conv_w[:, :SSD_INNER], conv_b[:SSD_INNER].reshape(1, -1),
      conv_w[:, SSD_INNER:], conv_b[SSD_INNER:].reshape(1, -1),
      pad16(dt_bias), pad16(a_log), dt_bias.reshape(-1, 1), a_log.reshape(-1, 1),
      jnp.repeat(d_skip, SSD_HEAD_DIM).reshape(1, -1), norm_w.reshape(1, -1), expand)


def _ret_kernel(logg_ref, q_ref, k_ref, v_ref, g_ref, cos_ref, sin_ref, o_ref, state_ref):
    C = CHUNK
    h = pl.program_id(0)

    @pl.when(pl.program_id(1) == 0)
    def _():
        state_ref[...] = jnp.zeros_like(state_ref)

    lg = logg_ref[h]
    cos = cos_ref[...]
    sin = sin_ref[...]

    def rot(x):
        return x * cos + pltpu.roll(x, HEAD_DIM // 2, 1) * sin

    q = rot(q_ref[...])
    k = rot(k_ref[...]) * (HEAD_DIM ** -0.5)
    v = v_ref[...].astype(BF16)
    row = lax.broadcasted_iota(jnp.int32, (C, C), 0)
    col = lax.broadcasted_iota(jnp.int32, (C, C), 1)
    rel = row - col
    decay = jnp.where(rel >= 0, jnp.exp(lg * jnp.maximum(rel, 0).astype(F32)), 0.0)
    inner = _dot_nt(q.astype(BF16), k.astype(BF16)) * decay
    y = _dot(inner.astype(BF16), v)
    pos = lax.broadcasted_iota(jnp.int32, (C, 1), 0).astype(F32)
    q_dec = jnp.exp(lg * (pos + 1.0))
    k_dec = jnp.exp(lg * (C - 1.0 - pos))
    state = state_ref[...]
    y = y + _dot((q * q_dec).astype(BF16), state.astype(BF16))
    chunk_dec = jnp.exp(lg * jnp.full((1, HEAD_DIM), float(C), F32))
    state_ref[...] = state * chunk_dec + _dot_tn((k * k_dec).astype(BF16), v)
    mu = jnp.mean(y, axis=-1, keepdims=True)
    yc = y - mu
    var = jnp.mean(yc * yc, axis=-1, keepdims=True)
    o_ref[...] = (_silu(g_ref[...]) * (yc * lax.rsqrt(var + LN_EPS))).astype(BF16)


def _retention(p, col):
    t = p.shape[0]
    C = CHUNK
    half = HEAD_DIM // 2
    inv = 1.0 / (ROPE_BASE ** (jnp.arange(0, HEAD_DIM, 2, dtype=F32) / HEAD_DIM))
    ang = jnp.arange(t, dtype=F32)[:, None] * inv[None, :]
    cos = jnp.concatenate([jnp.cos(ang), jnp.cos(ang)], axis=1)
    sin = jnp.concatenate([-jnp.sin(ang), jnp.sin(ang)], axis=1)
    log_g = jnp.log(1.0 - 2.0 ** (-5.0 - jnp.arange(RET_HEADS, dtype=F32)))
    blk = lambda name: pl.BlockSpec((C, HEAD_DIM), lambda h, c, lg: (c, col[name] + h))
    tab = pl.BlockSpec((C, HEAD_DIM), lambda h, c, lg: (c, 0))
    return pl.pallas_call(
        _ret_kernel,
        out_shape=jax.ShapeDtypeStruct((t, RET_HEADS * HEAD_DIM), BF16),
        grid_spec=pltpu.PrefetchScalarGridSpec(
            num_scalar_prefetch=1,
            grid=(RET_HEADS, t // C),
            in_specs=[blk("rq"), blk("rk"), blk("rv"), blk("rg"), tab, tab],
            out_specs=pl.BlockSpec((C, HEAD_DIM), lambda h, c, lg: (c, h)),
            scratch_shapes=[pltpu.VMEM((HEAD_DIM, HEAD_DIM), F32)]),
        compiler_params=_params("parallel", "arbitrary"),
        name="retention_mixer",
    )(log_g, p, p, p, p, cos, sin)


def _rel_bucket(dist):
    n = jnp.maximum(dist, 0)
    max_exact = REL_BUCKETS // 2
    nf = jnp.maximum(n, max_exact).astype(F32)
    large = max_exact + (jnp.log(nf / max_exact) / math.log(REL_MAX_DISTANCE / max_exact)
                         * (REL_BUCKETS - max_exact)).astype(jnp.int32)
    large = jnp.minimum(large, REL_BUCKETS - 1)
    return jnp.where(n < max_exact, n, large)


def _bias_lookup(tab_ref, bucket, h):
    acc = jnp.full(bucket.shape, tab_ref[h], F32)
    for b in range(1, REL_BUCKETS):
        acc = jnp.where(bucket == b, tab_ref[b * ATTN_HEADS + h], acc)
    return acc


def _bias_tile_kernel(tab_ref, o_ref, *, window):
    h = pl.program_id(0)
    o = pl.program_id(1)
    row = lax.broadcasted_iota(jnp.int32, (CHUNK, CHUNK), 0)
    col = lax.broadcasted_iota(jnp.int32, (CHUNK, CHUNK), 1)
    d = o * CHUNK + col - row
    valid = d >= 0
    if window is not None:
        valid = valid & (d < window)
    o_ref[0, 0] = jnp.where(valid, _bias_lookup(tab_ref, _rel_bucket(d), h) * LOG2E, NEG)


def _bias_tiles(rel_bias, n_off, window=None):
    return pl.pallas_call(
        functools.partial(_bias_tile_kernel, window=window),
        out_shape=jax.ShapeDtypeStruct((ATTN_HEADS, n_off, CHUNK, CHUNK), F32),
        grid_spec=pltpu.PrefetchScalarGridSpec(
            num_scalar_prefetch=1, grid=(ATTN_HEADS, n_off), in_specs=[],
            out_specs=pl.BlockSpec((1, 1, CHUNK, CHUNK), lambda h, o, tab: (h, o, 0, 0))),
        compiler_params=_params("parallel", "parallel"),
        name="rel_bias_tiles",
    )(rel_bias.reshape(-1))


_CMP_BAND_PERIOD = LANES * NSA_CMP_STRIDE // CHUNK


def _cmp_band_kernel(tab_ref, o_ref):
    h = pl.program_id(0)
    b = pl.program_id(1)
    row = lax.broadcasted_iota(jnp.int32, (CHUNK, 2 * LANES), 0)
    col = lax.broadcasted_iota(jnp.int32, (CHUNK, 2 * LANES), 1)
    d = b * CHUNK + row - NSA_CMP_STRIDE * (col - LANES) - (NSA_CMP_BLOCK - 1)
    o_ref[0, 0] = jnp.where(d >= 0, _bias_lookup(tab_ref, _rel_bucket(d), h), NEG)


def _cmp_band(rel_bias):
    return pl.pallas_call(
        _cmp_band_kernel,
        out_shape=jax.ShapeDtypeStruct((ATTN_HEADS, _CMP_BAND_PERIOD, CHUNK, 2 * LANES), F32),
        grid_spec=pltpu.PrefetchScalarGridSpec(
            num_scalar_prefetch=1, grid=(ATTN_HEADS, _CMP_BAND_PERIOD), in_specs=[],
            out_specs=pl.BlockSpec((1, 1, CHUNK, 2 * LANES), lambda h, b, tab: (h, b, 0, 0))),
        compiler_params=_params("parallel", "parallel"),
        name="cmp_bias_band",
    )(rel_bias.reshape(-1))


def _kprep_kernel(k_ref, v_ref, ka_ref, vt_ref, *rest, heads, block, with_mean):
    i = pl.program_id(0)
    k = k_ref[...]
    v = v_ref[...]
    if block is not None:
        row = lax.broadcasted_iota(jnp.int32, (CHUNK, LANES), 0)
        lane = lax.broadcasted_iota(jnp.int32, (CHUNK, LANES), 1)
        onehot = (lane == (i * CHUNK + row) // block).astype(BF16)
    for h in range(heads):
        sl = slice(h * HEAD_DIM, (h + 1) * HEAD_DIM)
        kh = k[:, sl].astype(BF16)
        ka_ref[h, 0] = kh if block is None else jnp.concatenate([kh, onehot], axis=1)
        vt_ref[h, 0] = v[:, sl].T.astype(BF16)
    if with_mean:
        km_ref = rest[0]

        @pl.when(i == 0)
        def _():
            km_ref[...] = jnp.zeros_like(km_ref)

        km_ref[pl.ds(i, 1), :] = jnp.mean(k, axis=0, keepdims=True)


def _kprep(p, kcol, vcol, heads, block, with_mean=False):
    t = p.shape[0]
    nt = t // CHUNK
    w = heads * HEAD_DIM
    kw = HEAD_DIM if block is None else 2 * HEAD_DIM
    out_shape = [jax.ShapeDtypeStruct((heads, nt, CHUNK, kw), BF16),
                 jax.ShapeDtypeStruct((heads, nt, HEAD_DIM, CHUNK), BF16)]
    out_specs = [pl.BlockSpec((heads, 1, CHUNK, kw), lambda i: (0, i, 0, 0)),
                 pl.BlockSpec((heads, 1, HEAD_DIM, CHUNK), lambda i: (0, i, 0, 0))]
    if with_mean:
        out_shape.append(jax.ShapeDtypeStruct((LANES, w), F32))
        out_specs.append(pl.BlockSpec((LANES, w), lambda i: (0, 0)))
    return pl.pallas_call(
        functools.partial(_kprep_kernel, heads=heads, block=block, with_mean=with_mean),
        out_shape=tuple(out_shape),
        grid=(t // CHUNK,),
        in_specs=[pl.BlockSpec((CHUNK, w), lambda i: (i, kcol)),
                  pl.BlockSpec((CHUNK, w), lambda i: (i, vcol))],
        out_specs=tuple(out_specs),
        compiler_params=_params("arbitrary"),
        name="key_prep",
    )(p, p)


def _topk_mask(score, k):
    lane = lax.broadcasted_iota(jnp.int32, score.shape, 1)

    def body(_, carry):
        g, sel = carry
        m = jnp.max(g, axis=1, keepdims=True)
        idx = jnp.min(jnp.where(g == m, lane, LANES), axis=1, keepdims=True)
        hit = lane == idx
        sel = jnp.where(hit & (m > -0.5 * BIG), 1.0, sel)
        return jnp.where(hit, -BIG, g), sel

    _, sel = lax.fori_loop(0, k, body, (score, jnp.zeros(score.shape, F32)))
    return sel > 0.5


def _moba_select_kernel(q_ref, km_ref, qa_ref):
    i = pl.program_id(0)
    q = q_ref[...]
    km = km_ref[...]
    lane = lax.broadcasted_iota(jnp.int32, (CHUNK, LANES), 1)
    head_cols = lambda h: slice(h * HEAD_DIM, (h + 1) * HEAD_DIM)
    gates = [jnp.where(lane < i, _dot_nt_split(q[:, head_cols(h)], km[:, head_cols(h)]), -BIG)
             for h in range(ATTN_HEADS)]
    sel = _topk_mask(jnp.concatenate(gates, axis=0), MOBA_TOPK)
    own = lane == i
    for h in range(ATTN_HEADS):
        pen = jnp.where(sel[h * CHUNK:(h + 1) * CHUNK] | own, 0.0, NEG)
        qa_ref[h, 0] = jnp.concatenate([q[:, head_cols(h)] * _QK_SCALE, pen], axis=1).T.astype(BF16)


def _moba_select(p, qcol, kmean):
    t = p.shape[0]
    w = ATTN_HEADS * HEAD_DIM
    return pl.pallas_call(
        _moba_select_kernel,
        out_shape=jax.ShapeDtypeStruct((ATTN_HEADS, t // CHUNK, 2 * HEAD_DIM, CHUNK), BF16),
        grid=(t // CHUNK,),
        in_specs=[pl.BlockSpec((CHUNK, w), lambda i: (i, qcol)),
                  pl.BlockSpec((LANES, w), lambda i: (0, 0))],
        out_specs=pl.BlockSpec((ATTN_HEADS, 1, 2 * HEAD_DIM, CHUNK), lambda i: (0, i, 0, 0)),
        compiler_params=_params("parallel"),
        name="moba_select",
    )(p, kmean)


def _flash_kernel(q_ref, k_ref, vt_ref, bias_ref, *rest, group, shared_kv, span, combine):
    if combine:
        ocmp_ref, osel_ref, gate_ref, o_ref = rest[:4]
    else:
        o_ref = rest[0]
    m_ref, l_ref, acc_ref, alpha_ref, pt_ref, s_even_ref, s_odd_ref = rest[-7:]
    hg = pl.program_id(0)
    i = pl.program_id(1)
    n_off = bias_ref.shape[1]
    nt = k_ref.shape[-3]
    head_cols = lambda g: slice(g * HEAD_DIM, (g + 1) * HEAD_DIM)
    if len(q_ref.shape) == 3:
        qts = [q_ref[g] for g in range(group)]
    else:
        qts = [(q_ref[:, head_cols(g)] * _QK_SCALE).T.astype(BF16) for g in range(group)]
    m_ref[...] = jnp.full_like(m_ref, -BIG)
    l_ref[...] = jnp.zeros_like(l_ref)
    acc_ref[...] = jnp.zeros_like(acc_ref)
    k_tile = lambda g, j: k_ref[j] if shared_kv else k_ref[g, j]
    vt_tile = lambda g, j: vt_ref[j] if shared_kv else vt_ref[g, j]
    alpha_ref[...] = jnp.ones_like(alpha_ref)
    pt_ref[...] = jnp.zeros_like(pt_ref)

    def scores(j, s_ref):
        for g in range(group):
            s_ref[g] = _dot(k_tile(g, j), qts[g])

    def value_update(j):
        for g in range(group):
            acc_ref[g] = alpha_ref[g] * acc_ref[g] + _dot(vt_tile(g, j), pt_ref[g])

    def step(j, s_ref, s_next_ref, const_bias):
        scores(jnp.minimum(j + 1, nt - 1), s_next_ref)
        value_update(jnp.maximum(j - 1, 0))
        for g in range(group):
            m_prev = m_ref[g]
            if const_bias:
                far = bias_ref[g, n_off - 1, 0:1, :]
                st = s_ref[g]
                m_new = jnp.maximum(m_prev, jnp.max(st, axis=0, keepdims=True) + far)
                pt = jnp.exp2(st - (m_new - far))
            else:
                st = s_ref[g] + bias_ref[g, i - j]
                m_new = jnp.maximum(m_prev, jnp.max(st, axis=0, keepdims=True))
                pt = jnp.exp2(st - m_new)
            alpha = jnp.exp2(m_prev - m_new)
            l_ref[g] = alpha * l_ref[g] + jnp.sum(pt, axis=0, keepdims=True)
            m_ref[g] = m_new
            alpha_ref[g] = alpha
            pt_ref[g] = pt.astype(BF16)

    def pairs(start, count, const_bias):
        def body(t, carry):
            j = start + 2 * t
            step(j, s_even_ref, s_odd_ref, const_bias)
            step(j + 1, s_odd_ref, s_even_ref, const_bias)
            return carry
        lax.fori_loop(0, count, body, 0)

    if span is None:
        n_far = jnp.maximum(i - (n_off - 2), 0) // 2 * 2
        j_lo = 0
    else:
        n_far = 0
        j_lo = jnp.maximum(i - (span - 1), 0)
    scores(j_lo, s_even_ref)
    if span is None:
        pairs(j_lo, n_far // 2, True)
    near_lo = j_lo + n_far
    n_near = i + 1 - near_lo
    pairs(near_lo, n_near // 2, False)

    @pl.when(n_near % 2 == 1)
    def _():
        step(i, s_even_ref, s_odd_ref, False)

    value_update(i)
    if combine:
        sig = 1.0 / (1.0 + jnp.exp(-gate_ref[...]))
        lane = lax.broadcasted_iota(jnp.int32, sig.shape, 1)
    for g in range(group):
        out = (acc_ref[g] * (1.0 / l_ref[g])).T
        if combine:
            h = hg * group + g
            pick = lambda c: jnp.sum(jnp.where(lane == 3 * h + c, sig, 0.0), axis=1, keepdims=True)
            out = pick(0) * ocmp_ref[:, head_cols(g)] + pick(1) * osel_ref[:, head_cols(g)] + pick(2) * out
        o_ref[:, head_cols(g)] = out.astype(o_ref.dtype)


def _flash(q, k, vt, bias, *, heads, group, shared_kv, out_dtype, span=None, combine=None):
    n_off = bias.shape[1]
    nt = k.shape[1]
    gw = group * HEAD_DIM
    if isinstance(q, tuple):
        q, base = q
        q_spec = pl.BlockSpec((CHUNK, gw), lambda hg, i: (i, base // group + hg))
    else:
        q_spec = pl.BlockSpec((group, None) + q.shape[2:], lambda hg, i: (hg, i, 0, 0))
    kv_block = (None,) if shared_kv else (group,)
    resident = lambda shape: pl.BlockSpec(shape, lambda hg, i: (hg, 0, 0, 0), pipeline_mode=pl.Buffered(1))
    in_specs = [q_spec, resident(kv_block + k.shape[1:]), resident(kv_block + vt.shape[1:]),
                resident((group, n_off, CHUNK, CHUNK))]
    args = [q, k, vt, bias]
    tile = pl.BlockSpec((CHUNK, gw), lambda hg, i: (i, hg))
    if combine is not None:
        ocmp, osel, p, gcol = combine
        in_specs += [tile, tile, pl.BlockSpec((CHUNK, LANES), lambda hg, i: (i, gcol))]
        args += [ocmp, osel, p]
    return pl.pallas_call(
        functools.partial(_flash_kernel, group=group, shared_kv=shared_kv, span=span,
                          combine=combine is not None),
        out_shape=jax.ShapeDtypeStruct((nt * CHUNK, heads * HEAD_DIM), out_dtype),
        grid=(heads // group, nt),
        in_specs=in_specs,
        out_specs=tile,
        scratch_shapes=[pltpu.VMEM((group, 1, CHUNK), F32), pltpu.VMEM((group, 1, CHUNK), F32),
                        pltpu.VMEM((group, HEAD_DIM, CHUNK), F32), pltpu.VMEM((group, 1, CHUNK), F32),
                        pltpu.VMEM((group, CHUNK, CHUNK), BF16),
                        pltpu.VMEM((group, CHUNK, CHUNK), F32), pltpu.VMEM((group, CHUNK, CHUNK), F32)],
        compiler_params=_params("parallel", "arbitrary"),
        name="flash_attention",
    )(*args)


def _compress_kernel(x_ref, pe_ref, w1_ref, w2_ref, o_ref):
    x = x_ref[0, 0]
    pe = pe_ref[0]
    half = NSA_CMP_STRIDE * HEAD_DIM
    a = _dot((x + pe[0:1]).astype(BF16), w1_ref[0, :half, :])
    b = _dot((x + pe[1:2]).astype(BF16), w1_ref[0, half:, :])
    hid = a + pltpu.roll(b, x.shape[0] - 1, 0)
    gelu = 0.5 * hid * (1.0 + jnp.tanh(math.sqrt(2.0 / math.pi) * (hid + 0.044715 * hid * hid * hid)))
    o_ref[0, 0] = _dot(gelu.astype(BF16), w2_ref[0])


def _compress(x2, pe, w1, w2):
    _, hkv, w, kd = x2.shape
    return pl.pallas_call(
        _compress_kernel,
        out_shape=jax.ShapeDtypeStruct((2, hkv, w, HEAD_DIM), F32),
        grid=(2, hkv),
        in_specs=[pl.BlockSpec((1, 1, w, kd), lambda a, h: (a, h, 0, 0)),
                  pl.BlockSpec((1, 2, kd), lambda a, h: (a, 0, 0)),
                  pl.BlockSpec((1, 2 * kd, HEAD_DIM), lambda a, h: (a, 0, 0)),
                  pl.BlockSpec((1, HEAD_DIM, HEAD_DIM), lambda a, h: (a, 0, 0))],
        out_specs=pl.BlockSpec((1, 1, w, HEAD_DIM), lambda a, h: (a, h, 0, 0)),
        compiler_params=_params("parallel", "parallel"),
        name="nsa_compress",
    )(x2, pe, w1, w2)


def _nsa_cmp_kernel(tab_ref, q_ref, kv_ref, band_ref, ov_ref, ocmp_ref, qa_ref):
    i = pl.program_id(0)
    width = kv_ref.shape[2]
    scale = HEAD_DIM ** -0.5
    q = q_ref[...]
    lane = lax.broadcasted_iota(jnp.int32, (CHUNK, LANES), 1)
    cur = (i * CHUNK + lax.broadcasted_iota(jnp.int32, (CHUNK, LANES), 0)) // NSA_SEL_BLOCK
    forced = (lane == 0) | (lane == cur) | (lane == cur - 1)
    ov = ov_ref[...]
    diag = i // _CMP_BAND_PERIOD
    head_cols = lambda h: slice(h * HEAD_DIM, (h + 1) * HEAD_DIM)
    scores = []
    for kvh in range(NSA_KV_HEADS):
        kc = kv_ref[0, kvh].astype(BF16)
        vc = kv_ref[1, kvh].astype(BF16)
        psum = jnp.zeros((CHUNK, width), F32)
        for g in range(NSA_GROUP):
            h = kvh * NSA_GROUP + g
            far = tab_ref[(REL_BUCKETS - 1) * ATTN_HEADS + h]
            band = band_ref[h]
            tiles = [jnp.where(t == diag - 1, band[:, :LANES],
                               jnp.where(t == diag, band[:, LANES:], jnp.where(t < diag, far, NEG)))
                     for t in range(width // LANES)]
            s = _dot_nt(q[:, head_cols(h)].astype(BF16), kc) * scale + jnp.concatenate(tiles, axis=1)
            m = jnp.max(s, axis=1, keepdims=True)
            e = jnp.exp(s - m)
            l = jnp.sum(e, axis=1, keepdims=True)
            pr = e * jnp.where(m > 0.1 * NEG, 1.0 / l, 0.0)
            ocmp_ref[:, head_cols(h)] = _dot(pr.astype(BF16), vc)
            psum = psum + pr
        hi, lo = _split_bf16(psum)
        imp = _dot(hi, ov) + _dot(lo, ov)
        scores.append(jnp.where(forced, BIG, jnp.where(lane <= cur, imp, -BIG)))
    sel = _topk_mask(jnp.concatenate(scores, axis=0), NSA_SEL_TOPN)
    for kvh in range(NSA_KV_HEADS):
        pen_t = jnp.where(sel[kvh * CHUNK:(kvh + 1) * CHUNK], 0.0, NEG).T.astype(BF16)
        for g in range(NSA_GROUP):
            h = kvh * NSA_GROUP + g
            qa_ref[h, 0] = jnp.concatenate([(q[:, head_cols(h)] * _QK_SCALE).T.astype(BF16), pen_t], axis=0)


def _nsa_cmp(p, qcol, kv_cmp, band, rel_bias):
    t = p.shape[0]
    width = kv_cmp.shape[2]
    w = ATTN_HEADS * HEAD_DIM
    n = jnp.arange(width)[:, None] * NSA_CMP_STRIDE
    j = jnp.arange(LANES)[None, :] * NSA_SEL_BLOCK
    overlap = ((n < j + NSA_SEL_BLOCK) & (n + NSA_CMP_BLOCK > j)
               & (jnp.arange(width)[:, None] < width - 1)).astype(BF16)
    return pl.pallas_call(
        _nsa_cmp_kernel,
        out_shape=(jax.ShapeDtypeStruct((t, w), F32),
                   jax.ShapeDtypeStruct((ATTN_HEADS, t // CHUNK, 2 * HEAD_DIM, CHUNK), BF16)),
        grid_spec=pltpu.PrefetchScalarGridSpec(
            num_scalar_prefetch=1,
            grid=(t // CHUNK,),
            in_specs=[pl.BlockSpec((CHUNK, w), lambda i, tab: (i, qcol)),
                      pl.BlockSpec(kv_cmp.shape, lambda i, tab: (0, 0, 0, 0)),
                      pl.BlockSpec((ATTN_HEADS, None, CHUNK, 2 * LANES),
                                   lambda i, tab: (0, i % _CMP_BAND_PERIOD, 0, 0)),
                      pl.BlockSpec((width, LANES), lambda i, tab: (0, 0))],
            out_specs=(pl.BlockSpec((CHUNK, w), lambda i, tab: (i, 0)),
                       pl.BlockSpec((ATTN_HEADS, 1, 2 * HEAD_DIM, CHUNK), lambda i, tab: (0, i, 0, 0)))),
        compiler_params=_params("parallel"),
        name="nsa_cmp_select",
    )(rel_bias.reshape(-1), p, kv_cmp, band, overlap)


_EVEN_ORDER = np.concatenate([np.arange(0, 2048), np.arange(2576, 5648), np.arange(2048, 2576)])
_EVEN_WIDTH = 5760
_EVEN_TN = 640
_DEINTERLEAVE = np.concatenate([np.arange(0, HEAD_DIM, 2), np.arange(1, HEAD_DIM, 2)])
_RET_PERM = (np.arange(RET_HEADS)[:, None] * HEAD_DIM + _DEINTERLEAVE[None, :]).reshape(-1)
_ODD_ORDER = np.concatenate([np.arange(0, 1024), 2584 + _RET_PERM, 3608 + _RET_PERM, np.arange(4632, 6680),
                             np.arange(1024, 2584)])
_ODD_WIDTH = 6912
_ODD_TN = 768


def _reorder_cols(w, order, width):
    idx = np.concatenate([order, np.full(width - len(order), w.shape[1])]).astype(np.int32)
    return jnp.pad(w, ((0, 0), (0, 1)))[:, idx].astype(BF16)


def _even_mixer(hb, w_in, conv_w, conv_b, dt_bias, a_log, d_skip, norm_w, bias_c):
    p = _matmul(hb, _reorder_cols(w_in, _EVEN_ORDER, _EVEN_WIDTH), _EVEN_TN)
    dt_t = p[:, 5632:5632 + SSD_HEADS].T
    y_ssd = _ssd(p, dt_t, conv_w, conv_b, dt_bias, a_log, d_skip, norm_w,
                 {"z": 0, "x": 1, "bc": 10, "dt": 44})
    k_aug, v_t, kmean = _kprep(p, 3, 4, ATTN_HEADS, MOBA_BLOCK, True)
    q_aug = _moba_select(p, 2, kmean)
    y_moba = _flash(q_aug, k_aug, v_t, bias_c, heads=ATTN_HEADS, group=4, shared_kv=False, out_dtype=BF16)
    return jnp.concatenate([y_ssd, y_moba], axis=1)


def _odd_mixer(hb, w_in, cmp_pe, cmp_w1, cmp_w2, rel_bias, bias_c, bias_w, cmp_band):
    p = _matmul(hb, _reorder_cols(w_in, _ODD_ORDER, _ODD_WIDTH), _ODD_TN)
    t = p.shape[0]
    x2 = p[:, 5120:5632].reshape(t, 2, NSA_KV_HEADS, HEAD_DIM).transpose(1, 2, 0, 3)
    x2 = x2.reshape(2, NSA_KV_HEADS, t // NSA_CMP_STRIDE, NSA_CMP_STRIDE * HEAD_DIM)
    kv_cmp = _compress(x2, cmp_pe.reshape(2, 2, NSA_CMP_STRIDE * HEAD_DIM),
                       cmp_w1.astype(BF16), cmp_w2.astype(BF16))
    o_cmp, q_aug = _nsa_cmp(p, 0, kv_cmp, cmp_band, rel_bias)
    k_aug, v_t = _kprep(p, 22, 23, NSA_KV_HEADS, NSA_SEL_BLOCK)
    o_sel = _flash(q_aug, k_aug, v_t, bias_c, heads=ATTN_HEADS, group=NSA_GROUP, shared_kv=True, out_dtype=F32)
    k_win, v_win_t = _kprep(p, 24, 25, NSA_KV_HEADS, None)
    y_nsa = _flash((p, 0), k_win, v_win_t, bias_w, heads=ATTN_HEADS, group=NSA_GROUP, shared_kv=True,
                   out_dtype=BF16, span=NSA_WINDOW // CHUNK + 1, combine=(o_cmp, o_sel, p, 52))
    y_ret = _retention(p, {"rq": 8, "rk": 16, "rv": 24, "rg": 32})
    return jnp.concatenate([y_nsa, y_ret], axis=1)


def kernel(x, rel_bias, ev_w_in, ev_conv_w, ev_conv_b, ev_dt_bias, ev_a_log, ev_d_skip, ev_norm_w, ev_w_out,
           od_w_in, od_cmp_pe, od_cmp_w1, od_cmp_w2, od_w_out,
           ffn_w_up, ffn_conv_w, ffn_conv_b, ffn_w_down, ln_g, ln_b):
    _, t0, _ = x.shape
    t = -(-t0 // CHUNK) * CHUNK
    h = jnp.pad(x[0], ((0, t - t0), (0, 0)))
    hb = h.astype(BF16)
    bias_c = _bias_tiles(rel_bias, 8)
    bias_w = _bias_tiles(rel_bias, NSA_WINDOW // CHUNK + 1, window=NSA_WINDOW)
    cmp_band = _cmp_band(rel_bias)
    for layer in range(DEPTH):
        i = layer // 2
        if layer % 2 == 0:
            mix = _even_mixer(hb, ev_w_in[i], ev_conv_w[i], ev_conv_b[i], ev_dt_bias[i], ev_a_log[i],
                              ev_d_skip[i], ev_norm_w[i], bias_c)
            w_out = ev_w_out[i]
        else:
            mix = _odd_mixer(hb, od_w_in[i], od_cmp_pe[i], od_cmp_w1[i], od_cmp_w2[i], rel_bias,
                             bias_c, bias_w, cmp_band)
            w_out = od_w_out[i]
        h, hb = _matmul_res_ln(mix, w_out.astype(BF16), h, ln_g[layer, 0], ln_b[layer, 0], 512)
        gated = _ffn_up(hb, ffn_w_up[layer].astype(BF16), ffn_conv_w[layer], ffn_conv_b[layer])
        h, hb = _matmul_res_ln(gated, ffn_w_down[layer].astype(BF16), h, ln_g[layer, 1], ln_b[layer, 1], 512)
    return h[None, :t0]
```

```python
import functools
import math

import numpy as np
import jax
import jax.numpy as jnp
from jax import lax
from jax.experimental import pallas as pl
from jax.experimental.pallas import tpu as pltpu

F32 = jnp.float32
BF16 = jnp.bfloat16
HIGHEST = lax.Precision.HIGHEST

D_MODEL = 2048
DEPTH = 4
HEAD_DIM = 128
ATTN_HEADS = 8
REL_BUCKETS = 32
REL_MAX_DISTANCE = 2048
SSD_HEADS = 16
SSD_HEAD_DIM = 64
SSD_INNER = 1024
SSD_GROUPS = 2
SSD_STATE = 128
SSD_CONV = 4
MOBA_BLOCK = 256
MOBA_TOPK = 3
NSA_KV_HEADS = 2
NSA_GROUP = 4
NSA_CMP_BLOCK = 32
NSA_CMP_STRIDE = 16
NSA_SEL_BLOCK = 64
NSA_SEL_TOPN = 16
NSA_WINDOW = 512
RET_HEADS = 8
ROPE_BASE = 10000.0
FFN_DIM = 5632
FFN_CONV = 3
LN_EPS = 1e-5
NEG = -1e30
BIG = 3e38
LOG2E = math.log2(math.e)
_QK_SCALE = HEAD_DIM ** -0.5 * LOG2E
DEEPNORM_ALPHA = (2 * DEPTH) ** 0.25

CHUNK = 256
LANES = 128
VMEM_LIMIT = 56 * 1024 * 1024


def _params(*sem):
    return pltpu.CompilerParams(dimension_semantics=sem, vmem_limit_bytes=VMEM_LIMIT)


def _silu(x):
    return x * (1.0 / (1.0 + jnp.exp(-x)))


def _dot(a, b):
    return jnp.dot(a, b, preferred_element_type=F32)


def _dot_nt(a, b):
    return lax.dot_general(a, b, (((1,), (1,)), ((), ())), preferred_element_type=F32)


def _dot_tn(a, b):
    return lax.dot_general(a, b, (((0,), (0,)), ((), ())), preferred_element_type=F32)


def _split_bf16(a):
    hi = a.astype(BF16)
    return hi, (a - hi.astype(F32)).astype(BF16)


def _dot_nt_split(a, b):
    ah, al = _split_bf16(a)
    bh, bl = _split_bf16(b)
    return _dot_nt(ah, bh) + (_dot_nt(ah, bl) + _dot_nt(al, bh))


def _shift_rows(a, carry, k):
    rolled = pltpu.roll(a, k, 0)
    head = jnp.where(lax.broadcasted_iota(jnp.int32, carry.shape, 0) < k,
                     pltpu.roll(carry, k, 0), rolled[:8])
    return jnp.concatenate([head, rolled[8:]], axis=0)


def _mm_kernel(x_ref, w_ref, o_ref):
    o_ref[...] = _dot(x_ref[...], w_ref[...]).astype(o_ref.dtype)


def _matmul(x, w_all, layer, tn, out_dtype=F32):
    m, k = x.shape
    n = w_all.shape[2]
    tm = min(1024, m)
    return pl.pallas_call(
        _mm_kernel,
        out_shape=jax.ShapeDtypeStruct((m, n), out_dtype),
        grid=(m // tm, n // tn),
        in_specs=[pl.BlockSpec((tm, k), lambda i, j: (i, 0)),
                  pl.BlockSpec((None, k, tn), lambda i, j: (layer, 0, j))],
        out_specs=pl.BlockSpec((tm, tn), lambda i, j: (i, j)),
        compiler_params=_params("parallel", "arbitrary"),
        name="proj_matmul",
    )(x, w_all)


def _mm_ln_kernel(*refs, steps):
    a_refs = refs[:len(steps)]
    w_ref, h_ref, g_ref, b_ref, o_ref, ob_ref, acc_ref = refs[len(steps):]
    k = pl.program_id(1)

    @pl.when(k == 0)
    def _():
        acc_ref[...] = jnp.zeros_like(acc_ref)

    first = 0
    for a_ref, n in zip(a_refs, steps):
        @pl.when((k >= first) & (k < first + n))
        def _(a_ref=a_ref):
            acc_ref[...] += _dot(a_ref[...], w_ref[...])
        first += n

    @pl.when(k == pl.num_programs(1) - 1)
    def _():
        y = DEEPNORM_ALPHA * h_ref[...] + acc_ref[...]
        mu = jnp.mean(y, axis=-1, keepdims=True)
        yc = y - mu
        var = jnp.mean(yc * yc, axis=-1, keepdims=True)
        out = yc * lax.rsqrt(var + LN_EPS) * g_ref[...] + b_ref[...]
        o_ref[...] = out
        ob_ref[...] = out.astype(BF16)


def _matmul_res_ln(parts, w_all, layer, h, g, b, tk):
    m = h.shape[0]
    n = w_all.shape[2]
    tm = min(512, m)
    steps = tuple(a.shape[1] // tk for a in parts)
    starts = tuple(sum(steps[:p]) for p in range(len(parts)))

    def part_spec(p):
        return pl.BlockSpec((tm, tk), lambda i, j: (i, jnp.clip(j - starts[p], 0, steps[p] - 1)))

    return pl.pallas_call(
        functools.partial(_mm_ln_kernel, steps=steps),
        out_shape=(jax.ShapeDtypeStruct((m, n), F32), jax.ShapeDtypeStruct((m, n), BF16)),
        grid=(m // tm, sum(steps)),
        in_specs=[part_spec(p) for p in range(len(parts))] + [
            pl.BlockSpec((None, tk, n), lambda i, j: (layer, j, 0)),
            pl.BlockSpec((tm, n), lambda i, j: (i, 0)),
            pl.BlockSpec((1, n), lambda i, j: (0, 0)),
            pl.BlockSpec((1, n), lambda i, j: (0, 0))],
        out_specs=(pl.BlockSpec((tm, n), lambda i, j: (i, 0)),
                   pl.BlockSpec((tm, n), lambda i, j: (i, 0))),
        scratch_shapes=[pltpu.VMEM((tm, n), F32)],
        compiler_params=_params("parallel", "arbitrary"),
        name="matmul_res_ln",
    )(*parts, w_all, h, g.reshape(1, n), b.reshape(1, n))


def _ffn_up_kernel(x_ref, wa_ref, wu_ref, cw_ref, cb_ref, o_ref, carry_ref, wab_ref, wub_ref):
    @pl.when(pl.program_id(1) == 0)
    def _():
        carry_ref[...] = jnp.zeros_like(carry_ref)
        wab_ref[...] = wa_ref[...].astype(BF16)
        wub_ref[...] = wu_ref[...].astype(BF16)

    x = x_ref[...]
    a = _dot(x, wab_ref[...])
    u = _dot(x, wub_ref[...])
    carry = carry_ref[...]
    cw = cw_ref[...]
    conv = (cw[0:1] * _shift_rows(a, carry, 2) + cw[1:2] * _shift_rows(a, carry, 1)
            + cw[2:3] * a + cb_ref[...])
    carry_ref[...] = a[a.shape[0] - 8:]
    o_ref[...] = (_silu(conv) * u).astype(BF16)


def _ffn_up(x, w_up_all, layer, conv_w, conv_b):
    m, k = x.shape
    tn = 512
    tm = min(1024, m)
    nj = FFN_DIM // tn
    return pl.pallas_call(
        _ffn_up_kernel,
        out_shape=jax.ShapeDtypeStruct((m, FFN_DIM), BF16),
        grid=(nj, m // tm),
        in_specs=[pl.BlockSpec((tm, k), lambda j, i: (i, 0)),
                  pl.BlockSpec((None, k, tn), lambda j, i: (layer, 0, j)),
                  pl.BlockSpec((None, k, tn), lambda j, i: (layer, 0, j + nj)),
                  pl.BlockSpec((FFN_CONV, tn), lambda j, i: (0, j)),
                  pl.BlockSpec((1, tn), lambda j, i: (0, j))],
        out_specs=pl.BlockSpec((tm, tn), lambda j, i: (i, j)),
        scratch_shapes=[pltpu.VMEM((8, tn), F32), pltpu.VMEM((k, tn), BF16), pltpu.VMEM((k, tn), BF16)],
        compiler_params=_params("parallel", "arbitrary"),
        name="ffn_up_conv",
    )(x, w_up_all, w_up_all, conv_w, conv_b.reshape(1, FFN_DIM))


def _ssd_kernel(z_ref, x_ref, bc_ref, dt_ref, dtt_ref, cwx_ref, cbx_ref, cwb_ref, cbb_ref,
                dtb_ref, alog_ref, dtbc_ref, alogc_ref, dskip_ref, nw_ref, expand_ref,
                o_ref, cx_ref, cbc_ref, state_ref):
    L = CHUNK
    hpg = SSD_HEADS // SSD_GROUPS
    gw = hpg * SSD_HEAD_DIM

    @pl.when(pl.program_id(0) == 0)
    def _():
        cx_ref[...] = jnp.zeros_like(cx_ref)
        cbc_ref[...] = jnp.zeros_like(cbc_ref)
        state_ref[...] = jnp.zeros_like(state_ref)

    def conv(raw, carry_ref, w_ref, b_ref):
        carry = carry_ref[...]
        w = w_ref[...]
        y = (w[0:1] * _shift_rows(raw, carry, 3) + w[1:2] * _shift_rows(raw, carry, 2)
             + w[2:3] * _shift_rows(raw, carry, 1) + w[3:4] * raw + b_ref[...])
        carry_ref[...] = raw[L - 8:]
        return _silu(y)

    xs = conv(x_ref[...], cx_ref, cwx_ref, cbx_ref)
    bcs = conv(bc_ref[...], cbc_ref, cwb_ref, cbb_ref)

    def softplus(v):
        return jnp.maximum(v, 0.0) + jnp.log(1.0 + jnp.exp(-jnp.abs(v)))

    dt = softplus(dt_ref[...] + dtb_ref[...])
    da = dt * (-jnp.exp(alog_ref[...]))
    row = lax.broadcasted_iota(jnp.int32, (L, L), 0)
    col = lax.broadcasted_iota(jnp.int32, (L, L), 1)
    causal = row >= col
    tril = causal.astype(F32)
    a_cum = jnp.dot(tril, da, precision=HIGHEST, preferred_element_type=F32)
    dt_r = softplus(dtt_ref[...] + dtbc_ref[...])
    da_r = dt_r * (-jnp.exp(alogc_ref[...]))
    a_cum_r = jnp.dot(da_r, (row <= col).astype(F32), precision=HIGHEST,
                      preferred_element_type=F32)

    expand = expand_ref[...]

    def widen(v):
        return jnp.dot(v, expand, precision=HIGHEST, preferred_element_type=F32)

    a_last = a_cum[L - 1:L]
    dt_w = widen(dt)
    decay_in_w = widen(jnp.exp(a_cum))
    decay_end_w = widen(jnp.exp(a_last - a_cum))
    chunk_decay_w = widen(jnp.broadcast_to(jnp.exp(a_last), (8, LANES)))[0:1]
    xdt = xs * dt_w
    xdt_b = xdt.astype(BF16)
    xdt_end_b = (xdt * decay_end_w).astype(BF16)
    lane_lo = lax.broadcasted_iota(jnp.int32, (L, LANES), 1) < SSD_HEAD_DIM

    ys = []
    for g in range(SSD_GROUPS):
        b_g = bcs[:, g * SSD_STATE:(g + 1) * SSD_STATE].astype(BF16)
        c_g = bcs[:, (SSD_GROUPS + g) * SSD_STATE:(SSD_GROUPS + g + 1) * SSD_STATE].astype(BF16)
        cb = _dot_nt(c_g, b_g)
        pieces = []
        for pair in range(hpg // 2):
            outs = []
            for k in range(2):
                h = g * hpg + pair * 2 + k
                diff = a_cum[:, h:h + 1] - a_cum_r[h:h + 1, :]
                w = (cb * jnp.exp(jnp.where(causal, diff, NEG))).astype(BF16)
                lo = (g * hpg + pair * 2) * SSD_HEAD_DIM
                outs.append(_dot(w, xdt_b[:, lo:lo + LANES]))
            pieces.append(jnp.where(lane_lo, outs[0], outs[1]))
        y_diag = jnp.concatenate(pieces, axis=1)
        state = state_ref[g]
        sl = slice(g * gw, (g + 1) * gw)
        y_off = _dot(c_g, state.astype(BF16)) * decay_in_w[:, sl]
        state_ref[g] = state * chunk_decay_w[:, sl] + _dot_tn(b_g, xdt_end_b[:, sl])
        ys.append(y_diag + y_off)
    y = jnp.concatenate(ys, axis=1) + xs * dskip_ref[...]
    yg = y * _silu(z_ref[...])
    half = SSD_INNER // SSD_GROUPS
    outs = []
    for g in range(SSD_GROUPS):
        part = yg[:, g * half:(g + 1) * half]
        outs.append(part * lax.rsqrt(jnp.mean(part * part, axis=-1, keepdims=True) + LN_EPS))
    o_ref[...] = (jnp.concatenate(outs, axis=1) * nw_ref[...]).astype(BF16)


def _ssd(p, dt_t, conv_w, conv_b, dt_bias, a_log, d_skip, norm_w, col):
    t = p.shape[0]
    L = CHUNK
    pad16 = lambda v: jnp.pad(v.astype(F32), (0, LANES - SSD_HEADS)).reshape(1, LANES)
    expand = (jnp.arange(LANES)[:, None] == (jnp.arange(SSD_INNER) // SSD_HEAD_DIM)[None, :]).astype(F32)
    full = lambda shape: pl.BlockSpec(shape, lambda c: (0,) * len(shape))
    return pl.pallas_call(
        _ssd_kernel,
        out_shape=jax.ShapeDtypeStruct((t, SSD_INNER), BF16),
        grid=(t // L,),
        in_specs=[pl.BlockSpec((L, 1024), lambda c: (c, col["z"])),
                  pl.BlockSpec((L, 1024), lambda c: (c, col["x"])),
                  pl.BlockSpec((L, 512), lambda c: (c, col["bc"])),
                  pl.BlockSpec((L, LANES), lambda c: (c, col["dt"])),
                  pl.BlockSpec((SSD_HEADS, L), lambda c: (0, c)),
                  full((SSD_CONV, 1024)), full((1, 1024)), full((SSD_CONV, 512)), full((1, 512)),
                  full((1, LANES)), full((1, LANES)), full((SSD_HEADS, 1)), full((SSD_HEADS, 1)),
                  full((1, 1024)), full((1, 1024)), full((LANES, 1024))],
        out_specs=pl.BlockSpec((L, SSD_INNER), lambda c: (c, 0)),
        scratch_shapes=[pltpu.VMEM((8, 1024), F32), pltpu.VMEM((8, 512), F32),
                        pltpu.VMEM((SSD_GROUPS, SSD_STATE, 512), F32)],
        compiler_params=_params("arbitrary"),
        name="ssd_mixer",
    )(p, p, p, p, dt_t,
      conv_w[:, :SSD_INNER], conv_b[:SSD_INNER].reshape(1, -1),
      conv_w[:, SSD_INNER:], conv_b[SSD_INNER:].reshape(1, -1),
      pad16(dt_bias), pad16(a_log), dt_bias.reshape(-1, 1), a_log.reshape(-1, 1),
      jnp.repeat(d_skip, SSD_HEAD_DIM).reshape(1, -1), norm_w.reshape(1, -1), expand)


def _ret_kernel(logg_ref, q_ref, k_ref, v_ref, g_ref, cos_ref, sin_ref, o_ref, state_ref):
    C = CHUNK
    h = pl.program_id(0)

    @pl.when(pl.program_id(1) == 0)
    def _():
        state_ref[...] = jnp.zeros_like(state_ref)

    lg = logg_ref[h]
    cos = cos_ref[...]
    sin = sin_ref[...]

    def rot(x):
        return x * cos + pltpu.roll(x, HEAD_DIM // 2, 1) * sin

    q = rot(q_ref[...])
    k = rot(k_ref[...]) * (HEAD_DIM ** -0.5)
    v = v_ref[...].astype(BF16)
    row = lax.broadcasted_iota(jnp.int32, (C, C), 0)
    col = lax.broadcasted_iota(jnp.int32, (C, C), 1)
    rel = row - col
    decay = jnp.where(rel >= 0, jnp.exp(lg * jnp.maximum(rel, 0).astype(F32)), 0.0)
    inner = _dot_nt(q.astype(BF16), k.astype(BF16)) * decay
    y = _dot(inner.astype(BF16), v)
    pos = lax.broadcasted_iota(jnp.int32, (C, 1), 0).astype(F32)
    q_dec = jnp.exp(lg * (pos + 1.0))
    k_dec = jnp.exp(lg * (C - 1.0 - pos))
    state = state_ref[...]
    y = y + _dot((q * q_dec).astype(BF16), state.astype(BF16))
    chunk_dec = jnp.exp(lg * jnp.full((1, HEAD_DIM), float(C), F32))
    state_ref[...] = state * chunk_dec + _dot_tn((k * k_dec).astype(BF16), v)
    mu = jnp.mean(y, axis=-1, keepdims=True)
    yc = y - mu
    var = jnp.mean(yc * yc, axis=-1, keepdims=True)
    o_ref[...] = (_silu(g_ref[...]) * (yc * lax.rsqrt(var + LN_EPS))).astype(BF16)


def _retention(p, col):
    t = p.shape[0]
    C = CHUNK
    half = HEAD_DIM // 2
    inv = 1.0 / (ROPE_BASE ** (jnp.arange(0, HEAD_DIM, 2, dtype=F32) / HEAD_DIM))
    ang = jnp.arange(t, dtype=F32)[:, None] * inv[None, :]
    cos = jnp.concatenate([jnp.cos(ang), jnp.cos(ang)], axis=1)
    sin = jnp.concatenate([-jnp.sin(ang), jnp.sin(ang)], axis=1)
    log_g = jnp.log(1.0 - 2.0 ** (-5.0 - jnp.arange(RET_HEADS, dtype=F32)))
    blk = lambda name: pl.BlockSpec((C, HEAD_DIM), lambda h, c, lg: (c, col[name] + h))
    tab = pl.BlockSpec((C, HEAD_DIM), lambda h, c, lg: (c, 0))
    return pl.pallas_call(
        _ret_kernel,
        out_shape=jax.ShapeDtypeStruct((t, RET_HEADS * HEAD_DIM), BF16),
        grid_spec=pltpu.PrefetchScalarGridSpec(
            num_scalar_prefetch=1,
            grid=(RET_HEADS, t // C),
            in_specs=[blk("rq"), blk("rk"), blk("rv"), blk("rg"), tab, tab],
            out_specs=pl.BlockSpec((C, HEAD_DIM), lambda h, c, lg: (c, h)),
            scratch_shapes=[pltpu.VMEM((HEAD_DIM, HEAD_DIM), F32)]),
        compiler_params=_params("parallel", "arbitrary"),
        name="retention_mixer",
    )(log_g, p, p, p, p, cos, sin)


def _rel_bucket(dist):
    n = jnp.maximum(dist, 0)
    max_exact = REL_BUCKETS // 2
    nf = jnp.maximum(n, max_exact).astype(F32)
    large = max_exact + (jnp.log(nf / max_exact) / math.log(REL_MAX_DISTANCE / max_exact)
                         * (REL_BUCKETS - max_exact)).astype(jnp.int32)
    large = jnp.minimum(large, REL_BUCKETS - 1)
    return jnp.where(n < max_exact, n, large)


def _bias_lookup(tab_ref, bucket, h):
    acc = jnp.full(bucket.shape, tab_ref[h], F32)
    for b in range(1, REL_BUCKETS):
        acc = jnp.where(bucket == b, tab_ref[b * ATTN_HEADS + h], acc)
    return acc


def _bias_tile_kernel(tab_ref, o_ref, *, window):
    h = pl.program_id(0)
    o = pl.program_id(1)
    row = lax.broadcasted_iota(jnp.int32, (CHUNK, CHUNK), 0)
    col = lax.broadcasted_iota(jnp.int32, (CHUNK, CHUNK), 1)
    d = o * CHUNK + col - row
    valid = d >= 0
    if window is not None:
        valid = valid & (d < window)
    o_ref[0, 0] = jnp.where(valid, _bias_lookup(tab_ref, _rel_bucket(d), h) * LOG2E, NEG)


def _bias_tiles(rel_bias, n_off, window=None):
    return pl.pallas_call(
        functools.partial(_bias_tile_kernel, window=window),
        out_shape=jax.ShapeDtypeStruct((ATTN_HEADS, n_off, CHUNK, CHUNK), F32),
        grid_spec=pltpu.PrefetchScalarGridSpec(
            num_scalar_prefetch=1, grid=(ATTN_HEADS, n_off), in_specs=[],
            out_specs=pl.BlockSpec((1, 1, CHUNK, CHUNK), lambda h, o, tab: (h, o, 0, 0))),
        compiler_params=_params("parallel", "parallel"),
        name="rel_bias_tiles",
    )(rel_bias.reshape(-1))


_CMP_BAND_PERIOD = LANES * NSA_CMP_STRIDE // CHUNK


def _cmp_band_kernel(tab_ref, o_ref):
    h = pl.program_id(0)
    b = pl.program_id(1)
    row = lax.broadcasted_iota(jnp.int32, (CHUNK, 2 * LANES), 0)
    col = lax.broadcasted_iota(jnp.int32, (CHUNK, 2 * LANES), 1)
    d = b * CHUNK + row - NSA_CMP_STRIDE * (col - LANES) - (NSA_CMP_BLOCK - 1)
    o_ref[0, 0] = jnp.where(d >= 0, _bias_lookup(tab_ref, _rel_bucket(d), h), NEG)


def _cmp_band(rel_bias):
    return pl.pallas_call(
        _cmp_band_kernel,
        out_shape=jax.ShapeDtypeStruct((ATTN_HEADS, _CMP_BAND_PERIOD, CHUNK, 2 * LANES), F32),
        grid_spec=pltpu.PrefetchScalarGridSpec(
            num_scalar_prefetch=1, grid=(ATTN_HEADS, _CMP_BAND_PERIOD), in_specs=[],
            out_specs=pl.BlockSpec((1, 1, CHUNK, 2 * LANES), lambda h, b, tab: (h, b, 0, 0))),
        compiler_params=_params("parallel", "parallel"),
        name="cmp_bias_band",
    )(rel_bias.reshape(-1))


def _kprep_kernel(k_ref, v_ref, ka_ref, vt_ref, *rest, heads, block, with_mean):
    i = pl.program_id(0)
    k = k_ref[...]
    v = v_ref[...]
    if block is not None:
        row = lax.broadcasted_iota(jnp.int32, (CHUNK, LANES), 0)
        lane = lax.broadcasted_iota(jnp.int32, (CHUNK, LANES), 1)
        onehot = (lane == (i * CHUNK + row) // block).astype(BF16)
    for h in range(heads):
        sl = slice(h * HEAD_DIM, (h + 1) * HEAD_DIM)
        kh = k[:, sl].astype(BF16)
        ka_ref[h, 0] = kh if block is None else jnp.concatenate([kh, onehot], axis=1)
        vt_ref[h, 0] = v[:, sl].T.astype(BF16)
    if with_mean:
        km_ref = rest[0]

        @pl.when(i == 0)
        def _():
            km_ref[...] = jnp.zeros_like(km_ref)

        km_ref[pl.ds(i, 1), :] = jnp.mean(k, axis=0, keepdims=True)


def _kprep(p, kcol, vcol, heads, block, with_mean=False):
    t = p.shape[0]
    nt = t // CHUNK
    w = heads * HEAD_DIM
    kw = HEAD_DIM if block is None else 2 * HEAD_DIM
    out_shape = [jax.ShapeDtypeStruct((heads, nt, CHUNK, kw), BF16),
                 jax.ShapeDtypeStruct((heads, nt, HEAD_DIM, CHUNK), BF16)]
    out_specs = [pl.BlockSpec((heads, 1, CHUNK, kw), lambda i: (0, i, 0, 0)),
                 pl.BlockSpec((heads, 1, HEAD_DIM, CHUNK), lambda i: (0, i, 0, 0))]
    if with_mean:
        out_shape.append(jax.ShapeDtypeStruct((LANES, w), F32))
        out_specs.append(pl.BlockSpec((LANES, w), lambda i: (0, 0)))
    return pl.pallas_call(
        functools.partial(_kprep_kernel, heads=heads, block=block, with_mean=with_mean),
        out_shape=tuple(out_shape),
        grid=(t // CHUNK,),
        in_specs=[pl.BlockSpec((CHUNK, w), lambda i: (i, kcol)),
                  pl.BlockSpec((CHUNK, w), lambda i: (i, vcol))],
        out_specs=tuple(out_specs),
        compiler_params=_params("arbitrary"),
        name="key_prep",
    )(p, p)


def _topk_mask(score, k):
    lane = lax.broadcasted_iota(jnp.int32, score.shape, 1)

    def body(_, carry):
        g, sel = carry
        m = jnp.max(g, axis=1, keepdims=True)
        idx = jnp.min(jnp.where(g == m, lane, LANES), axis=1, keepdims=True)
        hit = lane == idx
        sel = jnp.where(hit & (m > -0.5 * BIG), 1.0, sel)
        return jnp.where(hit, -BIG, g), sel

    _, sel = lax.fori_loop(0, k, body, (score, jnp.zeros(score.shape, F32)))
    return sel > 0.5


def _moba_select_kernel(q_ref, km_ref, qa_ref):
    i = pl.program_id(0)
    q = q_ref[...]
    km = km_ref[...]
    lane = lax.broadcasted_iota(jnp.int32, (CHUNK, LANES), 1)
    head_cols = lambda h: slice(h * HEAD_DIM, (h + 1) * HEAD_DIM)
    gates = [jnp.where(lane < i, _dot_nt_split(q[:, head_cols(h)], km[:, head_cols(h)]), -BIG)
             for h in range(ATTN_HEADS)]
    sel = _topk_mask(jnp.concatenate(gates, axis=0), MOBA_TOPK)
    own = lane == i
    for h in range(ATTN_HEADS):
        pen = jnp.where(sel[h * CHUNK:(h + 1) * CHUNK] | own, 0.0, NEG)
        qa_ref[h, 0] = jnp.concatenate([q[:, head_cols(h)] * _QK_SCALE, pen], axis=1).T.astype(BF16)


def _moba_select(p, qcol, kmean):
    t = p.shape[0]
    w = ATTN_HEADS * HEAD_DIM
    return pl.pallas_call(
        _moba_select_kernel,
        out_shape=jax.ShapeDtypeStruct((ATTN_HEADS, t // CHUNK, 2 * HEAD_DIM, CHUNK), BF16),
        grid=(t // CHUNK,),
        in_specs=[pl.BlockSpec((CHUNK, w), lambda i: (i, qcol)),
                  pl.BlockSpec((LANES, w), lambda i: (0, 0))],
        out_specs=pl.BlockSpec((ATTN_HEADS, 1, 2 * HEAD_DIM, CHUNK), lambda i: (0, i, 0, 0)),
        compiler_params=_params("parallel"),
        name="moba_select",
    )(p, kmean)


def _flash_kernel(q_ref, k_ref, vt_ref, bias_ref, *rest, group, shared_kv, span, combine):
    if combine:
        ocmp_ref, osel_ref, gate_ref, o_ref = rest[:4]
    else:
        o_ref = rest[0]
    m_ref, l_ref, acc_ref, alpha_ref, pt_ref, s_even_ref, s_odd_ref = rest[-7:]
    hg = pl.program_id(0)
    i = pl.program_id(1)
    n_off = bias_ref.shape[1]
    nt = k_ref.shape[-3]
    head_cols = lambda g: slice(g * HEAD_DIM, (g + 1) * HEAD_DIM)
    if len(q_ref.shape) == 3:
        qts = [q_ref[g] for g in range(group)]
    else:
        qts = [(q_ref[:, head_cols(g)] * _QK_SCALE).T.astype(BF16) for g in range(group)]
    m_ref[...] = jnp.full_like(m_ref, -BIG)
    l_ref[...] = jnp.zeros_like(l_ref)
    acc_ref[...] = jnp.zeros_like(acc_ref)
    k_tile = lambda g, j: k_ref[j] if shared_kv else k_ref[g, j]
    vt_tile = lambda g, j: vt_ref[j] if shared_kv else vt_ref[g, j]
    alpha_ref[...] = jnp.ones_like(alpha_ref)
    pt_ref[...] = jnp.zeros_like(pt_ref)

    def scores(j, s_ref):
        for g in range(group):
            s_ref[g] = _dot(k_tile(g, j), qts[g])

    def value_update(j):
        for g in range(group):
            acc_ref[g] = alpha_ref[g] * acc_ref[g] + _dot(vt_tile(g, j), pt_ref[g])

    def step(j, s_ref, s_next_ref, const_bias):
        scores(jnp.minimum(j + 1, nt - 1), s_next_ref)
        value_update(jnp.maximum(j - 1, 0))
        for g in range(group):
            m_prev = m_ref[g]
            if const_bias:
                far = bias_ref[g, n_off - 1, 0:1, :]
                st = s_ref[g]
                m_new = jnp.maximum(m_prev, jnp.max(st, axis=0, keepdims=True) + far)
                pt = jnp.exp2(st - (m_new - far))
            else:
                st = s_ref[g] + bias_ref[g, i - j]
                m_new = jnp.maximum(m_prev, jnp.max(st, axis=0, keepdims=True))
                pt = jnp.exp2(st - m_new)
            alpha = jnp.exp2(m_prev - m_new)
            l_ref[g] = alpha * l_ref[g] + jnp.sum(pt, axis=0, keepdims=True)
            m_ref[g] = m_new
            alpha_ref[g] = alpha
            pt_ref[g] = pt.astype(BF16)

    def pairs(start, count, const_bias):
        def body(t, carry):
            j = start + 2 * t
            step(j, s_even_ref, s_odd_ref, const_bias)
            step(j + 1, s_odd_ref, s_even_ref, const_bias)
            return carry
        lax.fori_loop(0, count, body, 0)

    if span is None:
        n_far = jnp.maximum(i - (n_off - 2), 0) // 2 * 2
        j_lo = 0
    else:
        n_far = 0
        j_lo = jnp.maximum(i - (span - 1), 0)
    scores(j_lo, s_even_ref)
    if span is None:
        pairs(j_lo, n_far // 2, True)
    near_lo = j_lo + n_far
    n_near = i + 1 - near_lo
    pairs(near_lo, n_near // 2, False)

    @pl.when(n_near % 2 == 1)
    def _():
        step(i, s_even_ref, s_odd_ref, False)

    value_update(i)
    if combine:
        sig = 1.0 / (1.0 + jnp.exp(-gate_ref[...]))
        lane = lax.broadcasted_iota(jnp.int32, sig.shape, 1)
    for g in range(group):
        out = (acc_ref[g] * (1.0 / l_ref[g])).T
        if combine:
            h = hg * group + g
            pick = lambda c: jnp.sum(jnp.where(lane == 3 * h + c, sig, 0.0), axis=1, keepdims=True)
            out = pick(0) * ocmp_ref[:, head_cols(g)] + pick(1) * osel_ref[:, head_cols(g)] + pick(2) * out
        o_ref[:, head_cols(g)] = out.astype(o_ref.dtype)


def _flash(q, k, vt, bias, *, heads, group, shared_kv, out_dtype, span=None, combine=None):
    n_off = bias.shape[1]
    nt = k.shape[1]
    gw = group * HEAD_DIM
    if isinstance(q, tuple):
        q, base = q
        q_spec = pl.BlockSpec((CHUNK, gw), lambda hg, i: (i, base // group + hg))
    else:
        q_spec = pl.BlockSpec((group, None) + q.shape[2:], lambda hg, i: (hg, i, 0, 0))
    kv_block = (None,) if shared_kv else (group,)
    resident = lambda shape: pl.BlockSpec(shape, lambda hg, i: (hg, 0, 0, 0), pipeline_mode=pl.Buffered(1))
    in_specs = [q_spec, resident(kv_block + k.shape[1:]), resident(kv_block + vt.shape[1:]),
                resident((group, n_off, CHUNK, CHUNK))]
    args = [q, k, vt, bias]
    tile = pl.BlockSpec((CHUNK, gw), lambda hg, i: (i, hg))
    if combine is not None:
        ocmp, osel, p, gcol = combine
        in_specs += [tile, tile, pl.BlockSpec((CHUNK, LANES), lambda hg, i: (i, gcol))]
        args += [ocmp, osel, p]
    return pl.pallas_call(
        functools.partial(_flash_kernel, group=group, shared_kv=shared_kv, span=span,
                          combine=combine is not None),
        out_shape=jax.ShapeDtypeStruct((nt * CHUNK, heads * HEAD_DIM), out_dtype),
        grid=(heads // group, nt),
        in_specs=in_specs,
        out_specs=tile,
        scratch_shapes=[pltpu.VMEM((group, 1, CHUNK), F32),
                        pltpu.VMEM((group, 1, CHUNK), F32),
                        pltpu.VMEM((group, HEAD_DIM, CHUNK), F32), pltpu.VMEM((group, 1, CHUNK), F32),
                        pltpu.VMEM((group, CHUNK, CHUNK), BF16),
                        pltpu.VMEM((group, CHUNK, CHUNK), F32), pltpu.VMEM((group, CHUNK, CHUNK), F32)],
        compiler_params=_params("parallel", "arbitrary"),
        name="flash_attention",
    )(*args)


def _compress_kernel(x_ref, pe_ref, w1_ref, w2_ref, o_ref):
    x = x_ref[0, 0]
    pe = pe_ref[0]
    half = NSA_CMP_STRIDE * HEAD_DIM
    a = _dot((x + pe[0:1]).astype(BF16), w1_ref[0, :half, :])
    b = _dot((x + pe[1:2]).astype(BF16), w1_ref[0, half:, :])
    hid = a + pltpu.roll(b, x.shape[0] - 1, 0)
    gelu = 0.5 * hid * (1.0 + jnp.tanh(math.sqrt(2.0 / math.pi) * (hid + 0.044715 * hid * hid * hid)))
    o_ref[0, 0] = _dot(gelu.astype(BF16), w2_ref[0])


def _compress(x2, pe, w1, w2):
    _, hkv, w, kd = x2.shape
    return pl.pallas_call(
        _compress_kernel,
        out_shape=jax.ShapeDtypeStruct((2, hkv, w, HEAD_DIM), F32),
        grid=(2, hkv),
        in_specs=[pl.BlockSpec((1, 1, w, kd), lambda a, h: (a, h, 0, 0)),
                  pl.BlockSpec((1, 2, kd), lambda a, h: (a, 0, 0)),
                  pl.BlockSpec((1, 2 * kd, HEAD_DIM), lambda a, h: (a, 0, 0)),
                  pl.BlockSpec((1, HEAD_DIM, HEAD_DIM), lambda a, h: (a, 0, 0))],
        out_specs=pl.BlockSpec((1, 1, w, HEAD_DIM), lambda a, h: (a, h, 0, 0)),
        compiler_params=_params("parallel", "parallel"),
        name="nsa_compress",
    )(x2, pe, w1, w2)


def _nsa_cmp_kernel(tab_ref, q_ref, kv_ref, band_ref, ov_ref, ocmp_ref, qa_ref):
    i = pl.program_id(0)
    width = kv_ref.shape[2]
    scale = HEAD_DIM ** -0.5
    q = q_ref[...]
    lane = lax.broadcasted_iota(jnp.int32, (CHUNK, LANES), 1)
    cur = (i * CHUNK + lax.broadcasted_iota(jnp.int32, (CHUNK, LANES), 0)) // NSA_SEL_BLOCK
    forced = (lane == 0) | (lane == cur) | (lane == cur - 1)
    ov = ov_ref[...]
    diag = i // _CMP_BAND_PERIOD
    head_cols = lambda h: slice(h * HEAD_DIM, (h + 1) * HEAD_DIM)
    scores = []
    for kvh in range(NSA_KV_HEADS):
        kc = kv_ref[0, kvh].astype(BF16)
        vc = kv_ref[1, kvh].astype(BF16)
        psum = jnp.zeros((CHUNK, width), F32)
        for g in range(NSA_GROUP):
            h = kvh * NSA_GROUP + g
            far = tab_ref[(REL_BUCKETS - 1) * ATTN_HEADS + h]
            band = band_ref[h]
            tiles = [jnp.where(t == diag - 1, band[:, :LANES],
                               jnp.where(t == diag, band[:, LANES:], jnp.where(t < diag, far, NEG)))
                     for t in range(width // LANES)]
            s = _dot_nt(q[:, head_cols(h)].astype(BF16), kc) * scale + jnp.concatenate(tiles, axis=1)
            m = jnp.max(s, axis=1, keepdims=True)
            e = jnp.exp(s - m)
            l = jnp.sum(e, axis=1, keepdims=True)
            pr = e * jnp.where(m > 0.1 * NEG, 1.0 / l, 0.0)
            ocmp_ref[:, head_cols(h)] = _dot(pr.astype(BF16), vc)
            psum = psum + pr
        hi, lo = _split_bf16(psum)
        imp = _dot(hi, ov) + _dot(lo, ov)
        scores.append(jnp.where(forced, BIG, jnp.where(lane <= cur, imp, -BIG)))
    sel = _topk_mask(jnp.concatenate(scores, axis=0), NSA_SEL_TOPN)
    for kvh in range(NSA_KV_HEADS):
        pen_t = jnp.where(sel[kvh * CHUNK:(kvh + 1) * CHUNK], 0.0, NEG).T.astype(BF16)
        for g in range(NSA_GROUP):
            h = kvh * NSA_GROUP + g
            qa_ref[h, 0] = jnp.concatenate([(q[:, head_cols(h)] * _QK_SCALE).T.astype(BF16), pen_t], axis=0)


def _nsa_cmp(p, qcol, kv_cmp, band, rel_bias):
    t = p.shape[0]
    width = kv_cmp.shape[2]
    w = ATTN_HEADS * HEAD_DIM
    n = jnp.arange(width)[:, None] * NSA_CMP_STRIDE
    j = jnp.arange(LANES)[None, :] * NSA_SEL_BLOCK
    overlap = ((n < j + NSA_SEL_BLOCK) & (n + NSA_CMP_BLOCK > j)
               & (jnp.arange(width)[:, None] < width - 1)).astype(BF16)
    return pl.pallas_call(
        _nsa_cmp_kernel,
        out_shape=(jax.ShapeDtypeStruct((t, w), F32),
                   jax.ShapeDtypeStruct((ATTN_HEADS, t // CHUNK, 2 * HEAD_DIM, CHUNK), BF16)),
        grid_spec=pltpu.PrefetchScalarGridSpec(
            num_scalar_prefetch=1,
            grid=(t // CHUNK,),
            in_specs=[pl.BlockSpec((CHUNK, w), lambda i, tab: (i, qcol)),
                      pl.BlockSpec(kv_cmp.shape, lambda i, tab: (0, 0, 0, 0)),
                      pl.BlockSpec((ATTN_HEADS, None, CHUNK, 2 * LANES),
                                   lambda i, tab: (0, i % _CMP_BAND_PERIOD, 0, 0)),
                      pl.BlockSpec((width, LANES), lambda i, tab: (0, 0))],
            out_specs=(pl.BlockSpec((CHUNK, w), lambda i, tab: (i, 0)),
                       pl.BlockSpec((ATTN_HEADS, 1, 2 * HEAD_DIM, CHUNK), lambda i, tab: (0, i, 0, 0)))),
        compiler_params=_params("parallel"),
        name="nsa_cmp_select",
    )(rel_bias.reshape(-1), p, kv_cmp, band, overlap)


_EVEN_ORDER = np.concatenate([np.arange(0, 2048), np.arange(2576, 5648), np.arange(2048, 2576)])
_EVEN_WIDTH = 5760
_EVEN_TN = 1920
_OUT_TK = 1024
_DOWN_TK = 1408
_DEINTERLEAVE = np.concatenate([np.arange(0, HEAD_DIM, 2), np.arange(1, HEAD_DIM, 2)])
_RET_PERM = (np.arange(RET_HEADS)[:, None] * HEAD_DIM + _DEINTERLEAVE[None, :]).reshape(-1)
_ODD_ORDER = np.concatenate([np.arange(0, 1024), 2584 + _RET_PERM, 3608 + _RET_PERM, np.arange(4632, 6680),
                             np.arange(1024, 2584)])
_ODD_WIDTH = 6912
_ODD_TN = 768


def _reorder_cols(w, order, width):
    idx = np.concatenate([order, np.full(width - len(order), w.shape[2])]).astype(np.int32)
    return jnp.take(w, idx, axis=2, mode="fill", fill_value=0).astype(BF16)


def _even_mixer(hb, w_in_all, layer, conv_w, conv_b, dt_bias, a_log, d_skip, norm_w, bias_c):
    p = _matmul(hb, w_in_all, layer, _EVEN_TN)
    dt_t = p[:, 5632:5632 + SSD_HEADS].T
    y_ssd = _ssd(p, dt_t, conv_w, conv_b, dt_bias, a_log, d_skip, norm_w,
                 {"z": 0, "x": 1, "bc": 10, "dt": 44})
    k_aug, v_t, kmean = _kprep(p, 3, 4, ATTN_HEADS, MOBA_BLOCK, True)
    q_aug = _moba_select(p, 2, kmean)
    y_moba = _flash(q_aug, k_aug, v_t, bias_c, heads=ATTN_HEADS, group=4, shared_kv=False, out_dtype=BF16)
    return y_ssd, y_moba


def _odd_mixer(hb, w_in_all, layer, cmp_pe, cmp_w1, cmp_w2, rel_bias, bias_c, bias_w, cmp_band):
    p = _matmul(hb, w_in_all, layer, _ODD_TN)
    t = p.shape[0]
    x2 = p[:, 5120:5632].reshape(t, 2, NSA_KV_HEADS, HEAD_DIM).transpose(1, 2, 0, 3)
    x2 = x2.reshape(2, NSA_KV_HEADS, t // NSA_CMP_STRIDE, NSA_CMP_STRIDE * HEAD_DIM)
    kv_cmp = _compress(x2, cmp_pe.reshape(2, 2, NSA_CMP_STRIDE * HEAD_DIM),
                       cmp_w1.astype(BF16), cmp_w2.astype(BF16))
    o_cmp, q_aug = _nsa_cmp(p, 0, kv_cmp, cmp_band, rel_bias)
    k_aug, v_t = _kprep(p, 22, 23, NSA_KV_HEADS, NSA_SEL_BLOCK)
    o_sel = _flash(q_aug, k_aug, v_t, bias_c, heads=ATTN_HEADS, group=NSA_GROUP, shared_kv=True, out_dtype=F32)
    k_win, v_win_t = _kprep(p, 24, 25, NSA_KV_HEADS, None)
    y_nsa = _flash((p, 0), k_win, v_win_t, bias_w, heads=ATTN_HEADS, group=NSA_GROUP, shared_kv=True,
                   out_dtype=BF16, span=NSA_WINDOW // CHUNK + 1, combine=(o_cmp, o_sel, p, 52))
    y_ret = _retention(p, {"rq": 8, "rk": 16, "rv": 24, "rg": 32})
    return y_nsa, y_ret


def kernel(x, rel_bias, ev_w_in, ev_conv_w, ev_conv_b, ev_dt_bias, ev_a_log, ev_d_skip, ev_norm_w, ev_w_out,
           od_w_in, od_cmp_pe, od_cmp_w1, od_cmp_w2, od_w_out,
           ffn_w_up, ffn_conv_w, ffn_conv_b, ffn_w_down, ln_g, ln_b):
    _, t0, _ = x.shape
    t = -(-t0 // CHUNK) * CHUNK
    h = jnp.pad(x[0], ((0, t - t0), (0, 0)))
    hb = h.astype(BF16)
    bias_c = _bias_tiles(rel_bias, 8)
    bias_w = _bias_tiles(rel_bias, NSA_WINDOW // CHUNK + 1, window=NSA_WINDOW)
    cmp_band = _cmp_band(rel_bias)
    ev_w_in_b = _reorder_cols(ev_w_in, _EVEN_ORDER, _EVEN_WIDTH)
    od_w_in_b = _reorder_cols(od_w_in, _ODD_ORDER, _ODD_WIDTH)
    ev_w_out_b, od_w_out_b = ev_w_out.astype(BF16), od_w_out.astype(BF16)
    w_down_b = ffn_w_down.astype(BF16)
    for layer in range(DEPTH):
        i = layer // 2
        if layer % 2 == 0:
            mix = _even_mixer(hb, ev_w_in_b, i, ev_conv_w[i], ev_conv_b[i], ev_dt_bias[i], ev_a_log[i],
                              ev_d_skip[i], ev_norm_w[i], bias_c)
            w_out_b = ev_w_out_b
        else:
            mix = _odd_mixer(hb, od_w_in_b, i, od_cmp_pe[i], od_cmp_w1[i], od_cmp_w2[i], rel_bias,
                             bias_c, bias_w, cmp_band)
            w_out_b = od_w_out_b
        h, hb = _matmul_res_ln(mix, w_out_b, i, h, ln_g[layer, 0], ln_b[layer, 0], _OUT_TK)
        gated = _ffn_up(hb, ffn_w_up, layer, ffn_conv_w[layer], ffn_conv_b[layer])
        h, hb = _matmul_res_ln((gated,), w_down_b, layer, h, ln_g[layer, 1], ln_b[layer, 1], _DOWN_TK)
    return h[None, :t0]
```

```python
import functools
import math

import numpy as np
import jax
import jax.numpy as jnp
from jax import lax
from jax.experimental import pallas as pl
from jax.experimental.pallas import tpu as pltpu

F32 = jnp.float32
BF16 = jnp.bfloat16
HIGHEST = lax.Precision.HIGHEST

D_MODEL = 2048
DEPTH = 4
HEAD_DIM = 128
ATTN_HEADS = 8
REL_BUCKETS = 32
REL_MAX_DISTANCE = 2048
SSD_HEADS = 16
SSD_HEAD_DIM = 64
SSD_INNER = 1024
SSD_GROUPS = 2
SSD_STATE = 128
SSD_CONV = 4
MOBA_BLOCK = 256
MOBA_TOPK = 3
NSA_KV_HEADS = 2
NSA_GROUP = 4
NSA_CMP_BLOCK = 32
NSA_CMP_STRIDE = 16
NSA_SEL_BLOCK = 64
NSA_SEL_TOPN = 16
NSA_WINDOW = 512
RET_HEADS = 8
ROPE_BASE = 10000.0
FFN_DIM = 5632
FFN_CONV = 3
LN_EPS = 1e-5
NEG = -1e30
BIG = 3e38
LOG2E = math.log2(math.e)
_QK_SCALE = HEAD_DIM ** -0.5 * LOG2E
DEEPNORM_ALPHA = (2 * DEPTH) ** 0.25

CHUNK = 256
LANES = 128
VMEM_LIMIT = 56 * 1024 * 1024


def _params(*sem):
    return pltpu.CompilerParams(dimension_semantics=sem, vmem_limit_bytes=VMEM_LIMIT)


def _silu(x):
    return x * (1.0 / (1.0 + jnp.exp(-x)))


def _dot(a, b):
    return jnp.dot(a, b, preferred_element_type=F32)


def _dot_nt(a, b):
    return lax.dot_general(a, b, (((1,), (1,)), ((), ())), preferred_element_type=F32)


def _dot_tn(a, b):
    return lax.dot_general(a, b, (((0,), (0,)), ((), ())), preferred_element_type=F32)


def _split_bf16(a):
    hi = a.astype(BF16)
    return hi, (a - hi.astype(F32)).astype(BF16)


def _dot_nt_split(a, b):
    ah, al = _split_bf16(a)
    bh, bl = _split_bf16(b)
    return _dot_nt(ah, bh) + (_dot_nt(ah, bl) + _dot_nt(al, bh))


def _shift_rows(a, carry, k):
    rolled = pltpu.roll(a, k, 0)
    head = jnp.where(lax.broadcasted_iota(jnp.int32, carry.shape, 0) < k,
                     pltpu.roll(carry, k, 0), rolled[:8])
    return jnp.concatenate([head, rolled[8:]], axis=0)


def _mm_kernel(x_ref, w_ref, o_ref):
    o_ref[...] = _dot(x_ref[...], w_ref[...]).astype(o_ref.dtype)


def _matmul(x, w_all, layer, tn, out_dtype=F32):
    m, k = x.shape
    n = w_all.shape[2]
    tm = min(1024, m)
    return pl.pallas_call(
        _mm_kernel,
        out_shape=jax.ShapeDtypeStruct((m, n), out_dtype),
        grid=(m // tm, n // tn),
        in_specs=[pl.BlockSpec((tm, k), lambda i, j: (i, 0)),
                  pl.BlockSpec((None, k, tn), lambda i, j: (layer, 0, j))],
        out_specs=pl.BlockSpec((tm, tn), lambda i, j: (i, j)),
        compiler_params=_params("parallel", "arbitrary"),
        name="proj_matmul",
    )(x, w_all)


def _mm_ln_kernel(*refs, steps):
    a_refs = refs[:len(steps)]
    w_ref, h_ref, g_ref, b_ref, o_ref, ob_ref, acc_ref = refs[len(steps):]
    k = pl.program_id(1)

    @pl.when(k == 0)
    def _():
        acc_ref[...] = jnp.zeros_like(acc_ref)

    first = 0
    for a_ref, n in zip(a_refs, steps):
        @pl.when((k >= first) & (k < first + n))
        def _(a_ref=a_ref):
            acc_ref[...] += _dot(a_ref[...], w_ref[...])
        first += n

    @pl.when(k == pl.num_programs(1) - 1)
    def _():
        y = DEEPNORM_ALPHA * h_ref[...] + acc_ref[...]
        mu = jnp.mean(y, axis=-1, keepdims=True)
        yc = y - mu
        var = jnp.mean(yc * yc, axis=-1, keepdims=True)
        out = yc * lax.rsqrt(var + LN_EPS) * g_ref[...] + b_ref[...]
        o_ref[...] = out
        ob_ref[...] = out.astype(BF16)


def _matmul_res_ln(parts, w_all, layer, h, g, b, tk):
    m = h.shape[0]
    n = w_all.shape[2]
    tm = min(512, m)
    steps = tuple(a.shape[1] // tk for a in parts)
    starts = tuple(sum(steps[:p]) for p in range(len(parts)))

    def part_spec(p):
        return pl.BlockSpec((tm, tk), lambda i, j: (i, jnp.clip(j - starts[p], 0, steps[p] - 1)))

    return pl.pallas_call(
        functools.partial(_mm_ln_kernel, steps=steps),
        out_shape=(jax.ShapeDtypeStruct((m, n), F32), jax.ShapeDtypeStruct((m, n), BF16)),
        grid=(m // tm, sum(steps)),
        in_specs=[part_spec(p) for p in range(len(parts))] + [
            pl.BlockSpec((None, tk, n), lambda i, j: (layer, j, 0)),
            pl.BlockSpec((tm, n), lambda i, j: (i, 0)),
            pl.BlockSpec((1, n), lambda i, j: (0, 0)),
            pl.BlockSpec((1, n), lambda i, j: (0, 0))],
        out_specs=(pl.BlockSpec((tm, n), lambda i, j: (i, 0)),
                   pl.BlockSpec((tm, n), lambda i, j: (i, 0))),
        scratch_shapes=[pltpu.VMEM((tm, n), F32)],
        compiler_params=_params("parallel", "arbitrary"),
        name="matmul_res_ln",
    )(*parts, w_all, h, g.reshape(1, n), b.reshape(1, n))


def _ffn_up_kernel(x_ref, wa_ref, wu_ref, cw_ref, cb_ref, o_ref, carry_ref, wab_ref, wub_ref):
    @pl.when(pl.program_id(1) == 0)
    def _():
        carry_ref[...] = jnp.zeros_like(carry_ref)
        wab_ref[...] = wa_ref[...].astype(BF16)
        wub_ref[...] = wu_ref[...].astype(BF16)

    x = x_ref[...]
    a = _dot(x, wab_ref[...])
    u = _dot(x, wub_ref[...])
    carry = carry_ref[...]
    cw = cw_ref[...]
    conv = (cw[0:1] * _shift_rows(a, carry, 2) + cw[1:2] * _shift_rows(a, carry, 1)
            + cw[2:3] * a + cb_ref[...])
    carry_ref[...] = a[a.shape[0] - 8:]
    o_ref[...] = (_silu(conv) * u).astype(BF16)


def _ffn_up(x, w_up_all, layer, conv_w, conv_b):
    m, k = x.shape
    tn = 512
    tm = min(1024, m)
    nj = FFN_DIM // tn
    return pl.pallas_call(
        _ffn_up_kernel,
        out_shape=jax.ShapeDtypeStruct((m, FFN_DIM), BF16),
        grid=(nj, m // tm),
        in_specs=[pl.BlockSpec((tm, k), lambda j, i: (i, 0)),
                  pl.BlockSpec((None, k, tn), lambda j, i: (layer, 0, j)),
                  pl.BlockSpec((None, k, tn), lambda j, i: (layer, 0, j + nj)),
                  pl.BlockSpec((FFN_CONV, tn), lambda j, i: (0, j)),
                  pl.BlockSpec((1, tn), lambda j, i: (0, j))],
        out_specs=pl.BlockSpec((tm, tn), lambda j, i: (i, j)),
        scratch_shapes=[pltpu.VMEM((8, tn), F32), pltpu.VMEM((k, tn), BF16), pltpu.VMEM((k, tn), BF16)],
        compiler_params=_params("parallel", "arbitrary"),
        name="ffn_up_conv",
    )(x, w_up_all, w_up_all, conv_w, conv_b.reshape(1, FFN_DIM))


def _ssd_kernel(z_ref, x_ref, bc_ref, dt_ref, dtt_ref, cwx_ref, cbx_ref, cwb_ref, cbb_ref,
                dtb_ref, alog_ref, dtbc_ref, alogc_ref, dskip_ref, nw_ref, expand_ref,
                o_ref, cx_ref, cbc_ref, state_ref):
    L = CHUNK
    hpg = SSD_HEADS // SSD_GROUPS
    gw = hpg * SSD_HEAD_DIM

    @pl.when(pl.program_id(0) == 0)
    def _():
        cx_ref[...] = jnp.zeros_like(cx_ref)
        cbc_ref[...] = jnp.zeros_like(cbc_ref)
        state_ref[...] = jnp.zeros_like(state_ref)

    def conv(raw, carry_ref, w_ref, b_ref):
        carry = carry_ref[...]
        w = w_ref[...]
        y = (w[0:1] * _shift_rows(raw, carry, 3) + w[1:2] * _shift_rows(raw, carry, 2)
             + w[2:3] * _shift_rows(raw, carry, 1) + w[3:4] * raw + b_ref[...])
        carry_ref[...] = raw[L - 8:]
        return _silu(y)

    xs = conv(x_ref[...], cx_ref, cwx_ref, cbx_ref)
    bcs = conv(bc_ref[...], cbc_ref, cwb_ref, cbb_ref)

    def softplus(v):
        return jnp.maximum(v, 0.0) + jnp.log(1.0 + jnp.exp(-jnp.abs(v)))

    dt = softplus(dt_ref[...] + dtb_ref[...])
    da = dt * (-jnp.exp(alog_ref[...]))
    row = lax.broadcasted_iota(jnp.int32, (L, L), 0)
    col = lax.broadcasted_iota(jnp.int32, (L, L), 1)
    causal = row >= col
    tril = causal.astype(F32)
    a_cum = jnp.dot(tril, da, precision=HIGHEST, preferred_element_type=F32)
    dt_r = softplus(dtt_ref[...] + dtbc_ref[...])
    da_r = dt_r * (-jnp.exp(alogc_ref[...]))
    a_cum_r = jnp.dot(da_r, (row <= col).astype(F32), precision=HIGHEST,
                      preferred_element_type=F32)

    expand = expand_ref[...]

    def widen(v):
        return jnp.dot(v, expand, precision=HIGHEST, preferred_element_type=F32)

    a_last = a_cum[L - 1:L]
    dt_w = widen(dt)
    decay_in_w = widen(jnp.exp(a_cum))
    decay_end_w = widen(jnp.exp(a_last - a_cum))
    chunk_decay_w = widen(jnp.broadcast_to(jnp.exp(a_last), (8, LANES)))[0:1]
    xdt = xs * dt_w
    xdt_b = xdt.astype(BF16)
    xdt_end_b = (xdt * decay_end_w).astype(BF16)
    lane_lo = lax.broadcasted_iota(jnp.int32, (L, LANES), 1) < SSD_HEAD_DIM

    ys = []
    for g in range(SSD_GROUPS):
        b_g = bcs[:, g * SSD_STATE:(g + 1) * SSD_STATE].astype(BF16)
        c_g = bcs[:, (SSD_GROUPS + g) * SSD_STATE:(SSD_GROUPS + g + 1) * SSD_STATE].astype(BF16)
        cb = _dot_nt(c_g, b_g)
        pieces = []
        for pair in range(hpg // 2):
            outs = []
            for k in range(2):
                h = g * hpg + pair * 2 + k
                diff = a_cum[:, h:h + 1] - a_cum_r[h:h + 1, :]
                w = (cb * jnp.exp(jnp.where(causal, diff, NEG))).astype(BF16)
                lo = (g * hpg + pair * 2) * SSD_HEAD_DIM
                outs.append(_dot(w, xdt_b[:, lo:lo + LANES]))
            pieces.append(jnp.where(lane_lo, outs[0], outs[1]))
        y_diag = jnp.concatenate(pieces, axis=1)
        state = state_ref[g]
        sl = slice(g * gw, (g + 1) * gw)
        y_off = _dot(c_g, state.astype(BF16)) * decay_in_w[:, sl]
        state_ref[g] = state * chunk_decay_w[:, sl] + _dot_tn(b_g, xdt_end_b[:, sl])
        ys.append(y_diag + y_off)
    y = jnp.concatenate(ys, axis=1) + xs * dskip_ref[...]
    yg = y * _silu(z_ref[...])
    half = SSD_INNER // SSD_GROUPS
    outs = []
    for g in range(SSD_GROUPS):
        part = yg[:, g * half:(g + 1) * half]
        outs.append(part * lax.rsqrt(jnp.mean(part * part, axis=-1, keepdims=True) + LN_EPS))
    o_ref[...] = (jnp.concatenate(outs, axis=1) * nw_ref[...]).astype(BF16)


def _ssd(p, dt_t, conv_w, conv_b, dt_bias, a_log, d_skip, norm_w, col):
    t = p.shape[0]
    L = CHUNK
    pad16 = lambda v: jnp.pad(v.astype(F32), (0, LANES - SSD_HEADS)).reshape(1, LANES)
    expand = (jnp.arange(LANES)[:, None] == (jnp.arange(SSD_INNER) // SSD_HEAD_DIM)[None, :]).astype(F32)
    full = lambda shape: pl.BlockSpec(shape, lambda c: (0,) * len(shape))
    return pl.pallas_call(
        _ssd_kernel,
        out_shape=jax.ShapeDtypeStruct((t, SSD_INNER), BF16),
        grid=(t // L,),
        in_specs=[pl.BlockSpec((L, 1024), lambda c: (c, col["z"])),
                  pl.BlockSpec((L, 1024), lambda c: (c, col["x"])),
                  pl.BlockSpec((L, 512), lambda c: (c, col["bc"])),
                  pl.BlockSpec((L, LANES), lambda c: (c, col["dt"])),
                  pl.BlockSpec((SSD_HEADS, L), lambda c: (0, c)),
                  full((SSD_CONV, 1024)), full((1, 1024)), full((SSD_CONV, 512)), full((1, 512)),
                  full((1, LANES)), full((1, LANES)), full((SSD_HEADS, 1)), full((SSD_HEADS, 1)),
                  full((1, 1024)), full((1, 1024)), full((LANES, 1024))],
        out_specs=pl.BlockSpec((L, SSD_INNER), lambda c: (c, 0)),
        scratch_shapes=[pltpu.VMEM((8, 1024), F32), pltpu.VMEM((8, 512), F32),
                        pltpu.VMEM((SSD_GROUPS, SSD_STATE, 512), F32)],
        compiler_params=_params("arbitrary"),
        name="ssd_mixer",
    )(p, p, p, p, dt_t,
      conv_w[:, :SSD_INNER], conv_b[:SSD_INNER].reshape(1, -1),
      conv_w[:, SSD_INNER:], conv_b[SSD_INNER:].reshape(1, -1),
      pad16(dt_bias), pad16(a_log), dt_bias.reshape(-1, 1), a_log.reshape(-1, 1),
      jnp.repeat(d_skip, SSD_HEAD_DIM).reshape(1, -1), norm_w.reshape(1, -1), expand)


_RET_GROUP = 4


def _ret_kernel(logg_ref, q_ref, k_ref, v_ref, g_ref, cos_ref, sin_ref, o_ref,
                state_ref, decay_ref, qdec_ref, kdec_ref, cdec_ref):
    C = CHUNK
    hg = pl.program_id(0)
    head_cols = lambda g: slice(g * HEAD_DIM, (g + 1) * HEAD_DIM)

    @pl.when(pl.program_id(1) == 0)
    def _():
        state_ref[...] = jnp.zeros_like(state_ref)
        rel = lax.broadcasted_iota(jnp.int32, (C, C), 0) - lax.broadcasted_iota(jnp.int32, (C, C), 1)
        pos = lax.broadcasted_iota(jnp.int32, (C, 1), 0).astype(F32)
        for g in range(_RET_GROUP):
            lg = logg_ref[hg * _RET_GROUP + g]
            decay_ref[g] = jnp.where(rel >= 0, jnp.exp(lg * jnp.maximum(rel, 0).astype(F32)), 0.0)
            qdec_ref[g] = jnp.exp(lg * (pos + 1.0))
            kdec_ref[g] = jnp.exp(lg * (C - 1.0 - pos))
            cdec_ref[g] = jnp.exp(lg * jnp.full((1, HEAD_DIM), float(C), F32))

    cos = cos_ref[...]
    sin = sin_ref[...]
    even_lane = lax.broadcasted_iota(jnp.int32, (C, HEAD_DIM), 1) % 2 == 0

    def rot(x):
        swapped = jnp.where(even_lane, pltpu.roll(x, HEAD_DIM - 1, 1), pltpu.roll(x, 1, 1))
        return x * cos + swapped * sin

    qs = [rot(q_ref[:, head_cols(g)]) for g in range(_RET_GROUP)]
    ks = [rot(k_ref[:, head_cols(g)]) * (HEAD_DIM ** -0.5) for g in range(_RET_GROUP)]
    vs = [v_ref[:, head_cols(g)].astype(BF16) for g in range(_RET_GROUP)]
    inners = [_dot_nt(qs[g].astype(BF16), ks[g].astype(BF16)) for g in range(_RET_GROUP)]
    for g in range(_RET_GROUP):
        state = state_ref[g]
        y = _dot((inners[g] * decay_ref[g]).astype(BF16), vs[g])
        y = y + _dot((qs[g] * qdec_ref[g]).astype(BF16), state.astype(BF16))
        state_ref[g] = state * cdec_ref[g] + _dot_tn((ks[g] * kdec_ref[g]).astype(BF16), vs[g])
        mu = jnp.mean(y, axis=-1, keepdims=True)
        yc = y - mu
        var = jnp.mean(yc * yc, axis=-1, keepdims=True)
        o_ref[:, head_cols(g)] = (_silu(g_ref[:, head_cols(g)]) * (yc * lax.rsqrt(var + LN_EPS))).astype(BF16)


def _retention(p, col):
    t = p.shape[0]
    C = CHUNK
    G = _RET_GROUP
    inv = 1.0 / (ROPE_BASE ** (jnp.arange(0, HEAD_DIM, 2, dtype=F32) / HEAD_DIM))
    ang = jnp.arange(t, dtype=F32)[:, None] * inv[None, :]
    cos = jnp.repeat(jnp.cos(ang), 2, axis=1)
    sin = jnp.stack([-jnp.sin(ang), jnp.sin(ang)], axis=-1).reshape(t, HEAD_DIM)
    log_g = jnp.log(1.0 - 2.0 ** (-5.0 - jnp.arange(RET_HEADS, dtype=F32)))
    blk = lambda name: pl.BlockSpec((C, G * HEAD_DIM), lambda h, c, lg: (c, col[name] // G + h))
    tab = pl.BlockSpec((C, HEAD_DIM), lambda h, c, lg: (c, 0))
    return pl.pallas_call(
        _ret_kernel,
        out_shape=jax.ShapeDtypeStruct((t, RET_HEADS * HEAD_DIM), BF16),
        grid_spec=pltpu.PrefetchScalarGridSpec(
            num_scalar_prefetch=1,
            grid=(RET_HEADS // G, t // C),
            in_specs=[blk("rq"), blk("rk"), blk("rv"), blk("rg"), tab, tab],
            out_specs=pl.BlockSpec((C, G * HEAD_DIM), lambda h, c, lg: (c, h)),
            scratch_shapes=[pltpu.VMEM((G, HEAD_DIM, HEAD_DIM), F32), pltpu.VMEM((G, C, C), F32),
                            pltpu.VMEM((G, C, 1), F32), pltpu.VMEM((G, C, 1), F32),
                            pltpu.VMEM((G, 1, HEAD_DIM), F32)]),
        compiler_params=_params("parallel", "arbitrary"),
        name="retention_mixer",
    )(log_g, p, p, p, p, cos, sin)


def _rel_bucket(dist):
    n = jnp.maximum(dist, 0)
    max_exact = REL_BUCKETS // 2
    nf = jnp.maximum(n, max_exact).astype(F32)
    large = max_exact + (jnp.log(nf / max_exact) / math.log(REL_MAX_DISTANCE / max_exact)
                         * (REL_BUCKETS - max_exact)).astype(jnp.int32)
    large = jnp.minimum(large, REL_BUCKETS - 1)
    return jnp.where(n < max_exact, n, large)


def _bias_lookup(tab_ref, bucket, h):
    acc = jnp.full(bucket.shape, tab_ref[h], F32)
    for b in range(1, REL_BUCKETS):
        acc = jnp.where(bucket == b, tab_ref[b * ATTN_HEADS + h], acc)
    return acc


def _bias_tile_kernel(tab_ref, o_ref, *, window):
    h = pl.program_id(0)
    o = pl.program_id(1)
    row = lax.broadcasted_iota(jnp.int32, (CHUNK, CHUNK), 0)
    col = lax.broadcasted_iota(jnp.int32, (CHUNK, CHUNK), 1)
    d = o * CHUNK + col - row
    valid = d >= 0
    if window is not None:
        valid = valid & (d < window)
    o_ref[0, 0] = jnp.where(valid, _bias_lookup(tab_ref, _rel_bucket(d), h) * LOG2E, NEG)


def _bias_tiles(rel_bias, n_off, window=None):
    return pl.pallas_call(
        functools.partial(_bias_tile_kernel, window=window),
        out_shape=jax.ShapeDtypeStruct((ATTN_HEADS, n_off, CHUNK, CHUNK), F32),
        grid_spec=pltpu.PrefetchScalarGridSpec(
            num_scalar_prefetch=1, grid=(ATTN_HEADS, n_off), in_specs=[],
            out_specs=pl.BlockSpec((1, 1, CHUNK, CHUNK), lambda h, o, tab: (h, o, 0, 0))),
        compiler_params=_params("parallel", "parallel"),
        name="rel_bias_tiles",
    )(rel_bias.reshape(-1))


_CMP_BAND_PERIOD = LANES * NSA_CMP_STRIDE // CHUNK


def _cmp_band_kernel(tab_ref, o_ref):
    h = pl.program_id(0)
    b = pl.program_id(1)
    row = lax.broadcasted_iota(jnp.int32, (CHUNK, 2 * LANES), 0)
    col = lax.broadcasted_iota(jnp.int32, (CHUNK, 2 * LANES), 1)
    d = b * CHUNK + row - NSA_CMP_STRIDE * (col - LANES) - (NSA_CMP_BLOCK - 1)
    o_ref[0, 0] = jnp.where(d >= 0, _bias_lookup(tab_ref, _rel_bucket(d), h), NEG)


def _cmp_band(rel_bias):
    return pl.pallas_call(
        _cmp_band_kernel,
        out_shape=jax.ShapeDtypeStruct((ATTN_HEADS, _CMP_BAND_PERIOD, CHUNK, 2 * LANES), F32),
        grid_spec=pltpu.PrefetchScalarGridSpec(
            num_scalar_prefetch=1, grid=(ATTN_HEADS, _CMP_BAND_PERIOD), in_specs=[],
            out_specs=pl.BlockSpec((1, 1, CHUNK, 2 * LANES), lambda h, b, tab: (h, b, 0, 0))),
        compiler_params=_params("parallel", "parallel"),
        name="cmp_bias_band",
    )(rel_bias.reshape(-1))


def _kprep_kernel(k_ref, v_ref, ka_ref, vt_ref, *rest, heads, block, with_mean):
    i = pl.program_id(0)
    k = k_ref[...]
    v = v_ref[...]
    if block is not None:
        row = lax.broadcasted_iota(jnp.int32, (CHUNK, LANES), 0)
        lane = lax.broadcasted_iota(jnp.int32, (CHUNK, LANES), 1)
        onehot = (lane == (i * CHUNK + row) // block).astype(BF16)
    for h in range(heads):
        sl = slice(h * HEAD_DIM, (h + 1) * HEAD_DIM)
        kh = k[:, sl].astype(BF16)
        ka_ref[h, 0] = kh if block is None else jnp.concatenate([kh, onehot], axis=1)
        vt_ref[h, 0] = v[:, sl].T.astype(BF16)
    if with_mean:
        km_ref = rest[0]

        @pl.when(i == 0)
        def _():
            km_ref[...] = jnp.zeros_like(km_ref)

        km_ref[pl.ds(i, 1), :] = jnp.mean(k, axis=0, keepdims=True)


def _kprep(p, kcol, vcol, heads, block, with_mean=False):
    t = p.shape[0]
    nt = t // CHUNK
    w = heads * HEAD_DIM
    kw = HEAD_DIM if block is None else 2 * HEAD_DIM
    out_shape = [jax.ShapeDtypeStruct((heads, nt, CHUNK, kw), BF16),
                 jax.ShapeDtypeStruct((heads, nt, HEAD_DIM, CHUNK), BF16)]
    out_specs = [pl.BlockSpec((heads, 1, CHUNK, kw), lambda i: (0, i, 0, 0)),
                 pl.BlockSpec((heads, 1, HEAD_DIM, CHUNK), lambda i: (0, i, 0, 0))]
    if with_mean:
        out_shape.append(jax.ShapeDtypeStruct((LANES, w), F32))
        out_specs.append(pl.BlockSpec((LANES, w), lambda i: (0, 0)))
    return pl.pallas_call(
        functools.partial(_kprep_kernel, heads=heads, block=block, with_mean=with_mean),
        out_shape=tuple(out_shape),
        grid=(t // CHUNK,),
        in_specs=[pl.BlockSpec((CHUNK, w), lambda i: (i, kcol)),
                  pl.BlockSpec((CHUNK, w), lambda i: (i, vcol))],
        out_specs=tuple(out_specs),
        compiler_params=_params("arbitrary"),
        name="key_prep",
    )(p, p)


def _topk_mask(score, k):
    lane = lax.broadcasted_iota(jnp.int32, score.shape, 1)

    def body(_, carry):
        g, sel = carry
        m = jnp.max(g, axis=1, keepdims=True)
        idx = jnp.min(jnp.where(g == m, lane, LANES), axis=1, keepdims=True)
        hit = lane == idx
        sel = jnp.where(hit & (m > -0.5 * BIG), 1.0, sel)
        return jnp.where(hit, -BIG, g), sel

    _, sel = lax.fori_loop(0, k, body, (score, jnp.zeros(score.shape, F32)))
    return sel > 0.5


def _moba_select_kernel(q_ref, km_ref, qa_ref):
    i = pl.program_id(0)
    q = q_ref[...]
    km = km_ref[...]
    lane = lax.broadcasted_iota(jnp.int32, (CHUNK, LANES), 1)
    head_cols = lambda h: slice(h * HEAD_DIM, (h + 1) * HEAD_DIM)
    gates = [jnp.where(lane < i, _dot_nt_split(q[:, head_cols(h)], km[:, head_cols(h)]), -BIG)
             for h in range(ATTN_HEADS)]
    sel = _topk_mask(jnp.concatenate(gates, axis=0), MOBA_TOPK)
    own = lane == i
    for h in range(ATTN_HEADS):
        pen = jnp.where(sel[h * CHUNK:(h + 1) * CHUNK] | own, 0.0, NEG)
        qa_ref[h, 0] = jnp.concatenate([q[:, head_cols(h)] * _QK_SCALE, pen], axis=1).T.astype(BF16)


def _moba_select(p, qcol, kmean):
    t = p.shape[0]
    w = ATTN_HEADS * HEAD_DIM
    return pl.pallas_call(
        _moba_select_kernel,
        out_shape=jax.ShapeDtypeStruct((ATTN_HEADS, t // CHUNK, 2 * HEAD_DIM, CHUNK), BF16),
        grid=(t // CHUNK,),
        in_specs=[pl.BlockSpec((CHUNK, w), lambda i: (i, qcol)),
                  pl.BlockSpec((LANES, w), lambda i: (0, 0))],
        out_specs=pl.BlockSpec((ATTN_HEADS, 1, 2 * HEAD_DIM, CHUNK), lambda i: (0, i, 0, 0)),
        compiler_params=_params("parallel"),
        name="moba_select",
    )(p, kmean)


def _flash_kernel(q_ref, k_ref, vt_ref, bias_ref, *rest, group, shared_kv, span, combine):
    if combine:
        ocmp_ref, osel_ref, gate_ref, o_ref = rest[:4]
    else:
        o_ref = rest[0]
    m_ref, l_ref, acc_ref, alpha_ref, pt_ref, s_even_ref, s_odd_ref = rest[-7:]
    hg = pl.program_id(0)
    i = pl.program_id(1)
    n_off = bias_ref.shape[1]
    nt = k_ref.shape[-3]
    head_cols = lambda g: slice(g * HEAD_DIM, (g + 1) * HEAD_DIM)
    if len(q_ref.shape) == 3:
        qts = [q_ref[g] for g in range(group)]
    else:
        qts = [(q_ref[:, head_cols(g)] * _QK_SCALE).T.astype(BF16) for g in range(group)]
    m_ref[...] = jnp.full_like(m_ref, -BIG)
    l_ref[...] = jnp.zeros_like(l_ref)
    acc_ref[...] = jnp.zeros_like(acc_ref)
    k_tile = lambda g, j: k_ref[j] if shared_kv else k_ref[g, j]
    vt_tile = lambda g, j: vt_ref[j] if shared_kv else vt_ref[g, j]
    alpha_ref[...] = jnp.ones_like(alpha_ref)
    pt_ref[...] = jnp.zeros_like(pt_ref)

    def scores(j, s_ref):
        for g in range(group):
            s_ref[g] = _dot(k_tile(g, j), qts[g])

    def value_update(j):
        for g in range(group):
            acc_ref[g] = alpha_ref[g] * acc_ref[g] + _dot(vt_tile(g, j), pt_ref[g])

    def step(j, s_ref, s_next_ref, const_bias):
        scores(jnp.minimum(j + 1, nt - 1), s_next_ref)
        value_update(jnp.maximum(j - 1, 0))
        for g in range(group):
            m_prev = m_ref[g]
            if const_bias:
                far = bias_ref[g, n_off - 1, 0:1, :]
                st = s_ref[g]
                m_new = jnp.maximum(m_prev, jnp.max(st, axis=0, keepdims=True) + far)
                pt = jnp.exp2(st - (m_new - far))
            else:
                st = s_ref[g] + bias_ref[g, i - j]
                m_new = jnp.maximum(m_prev, jnp.max(st, axis=0, keepdims=True))
                pt = jnp.exp2(st - m_new)
            alpha = jnp.exp2(m_prev - m_new)
            l_ref[g] = alpha * l_ref[g] + jnp.sum(pt, axis=0, keepdims=True)
            m_ref[g] = m_new
            alpha_ref[g] = alpha
            pt_ref[g] = pt.astype(BF16)

    def pairs(start, count, const_bias):
        def body(t, carry):
            j = start + 2 * t
            step(j, s_even_ref, s_odd_ref, const_bias)
            step(j + 1, s_odd_ref, s_even_ref, const_bias)
            return carry
        lax.fori_loop(0, count, body, 0)

    if span is None:
        n_far = jnp.maximum(i - (n_off - 2), 0) // 2 * 2
        j_lo = 0
    else:
        n_far = 0
        j_lo = jnp.maximum(i - (span - 1), 0)
    scores(j_lo, s_even_ref)
    if span is None:
        pairs(j_lo, n_far // 2, True)
    near_lo = j_lo + n_far
    n_near = i + 1 - near_lo
    pairs(near_lo, n_near // 2, False)

    @pl.when(n_near % 2 == 1)
    def _():
        step(i, s_even_ref, s_odd_ref, False)

    value_update(i)
    if combine:
        sig = 1.0 / (1.0 + jnp.exp(-gate_ref[...]))
        lane = lax.broadcasted_iota(jnp.int32, sig.shape, 1)
    for g in range(group):
        out = (acc_ref[g] * (1.0 / l_ref[g])).T
        if combine:
            h = hg * group + g
            pick = lambda c: jnp.sum(jnp.where(lane == 3 * h + c, sig, 0.0), axis=1, keepdims=True)
            out = pick(0) * ocmp_ref[:, head_cols(g)] + pick(1) * osel_ref[:, head_cols(g)] + pick(2) * out
        o_ref[:, head_cols(g)] = out.astype(o_ref.dtype)


def _flash(q, k, vt, bias, *, heads, group, shared_kv, out_dtype, span=None, combine=None):
    n_off = bias.shape[1]
    nt = k.shape[1]
    gw = group * HEAD_DIM
    if isinstance(q, tuple):
        q, base = q
        q_spec = pl.BlockSpec((CHUNK, gw), lambda hg, i: (i, base // group + hg))
    else:
        q_spec = pl.BlockSpec((group, None) + q.shape[2:], lambda hg, i: (hg, i, 0, 0))
    kv_block = (None,) if shared_kv else (group,)
    resident = lambda shape: pl.BlockSpec(shape, lambda hg, i: (hg, 0, 0, 0), pipeline_mode=pl.Buffered(1))
    in_specs = [q_spec, resident(kv_block + k.shape[1:]), resident(kv_block + vt.shape[1:]),
                resident((group, n_off, CHUNK, CHUNK))]
    args = [q, k, vt, bias]
    tile = pl.BlockSpec((CHUNK, gw), lambda hg, i: (i, hg))
    if combine is not None:
        ocmp, osel, p, gcol = combine
        in_specs += [tile, tile, pl.BlockSpec((CHUNK, LANES), lambda hg, i: (i, gcol))]
        args += [ocmp, osel, p]
    return pl.pallas_call(
        functools.partial(_flash_kernel, group=group, shared_kv=shared_kv, span=span,
                          combine=combine is not None),
        out_shape=jax.ShapeDtypeStruct((nt * CHUNK, heads * HEAD_DIM), out_dtype),
        grid=(heads // group, nt),
        in_specs=in_specs,
        out_specs=tile,
        scratch_shapes=[pltpu.VMEM((group, 1, CHUNK), F32),
                        pltpu.VMEM((group, 1, CHUNK), F32),
                        pltpu.VMEM((group, HEAD_DIM, CHUNK), F32), pltpu.VMEM((group, 1, CHUNK), F32),
                        pltpu.VMEM((group, CHUNK, CHUNK), BF16),
                        pltpu.VMEM((group, CHUNK, CHUNK), F32), pltpu.VMEM((group, CHUNK, CHUNK), F32)],
        compiler_params=_params("parallel", "arbitrary"),
        name="flash_attention",
    )(*args)


def _compress_kernel(x_ref, pe_ref, w1_ref, w2_ref, o_ref):
    x = x_ref[0, 0]
    pe = pe_ref[0]
    half = NSA_CMP_STRIDE * HEAD_DIM
    a = _dot((x + pe[0:1]).astype(BF16), w1_ref[0, :half, :])
    b = _dot((x + pe[1:2]).astype(BF16), w1_ref[0, half:, :])
    hid = a + pltpu.roll(b, x.shape[0] - 1, 0)
    gelu = 0.5 * hid * (1.0 + jnp.tanh(math.sqrt(2.0 / math.pi) * (hid + 0.044715 * hid * hid * hid)))
    o_ref[0, 0] = _dot(gelu.astype(BF16), w2_ref[0])


def _compress(x2, pe, w1, w2):
    _, hkv, w, kd = x2.shape
    return pl.pallas_call(
        _compress_kernel,
        out_shape=jax.ShapeDtypeStruct((2, hkv, w, HEAD_DIM), F32),
        grid=(2, hkv),
        in_specs=[pl.BlockSpec((1, 1, w, kd), lambda a, h: (a, h, 0, 0)),
                  pl.BlockSpec((1, 2, kd), lambda a, h: (a, 0, 0)),
                  pl.BlockSpec((1, 2 * kd, HEAD_DIM), lambda a, h: (a, 0, 0)),
                  pl.BlockSpec((1, HEAD_DIM, HEAD_DIM), lambda a, h: (a, 0, 0))],
        out_specs=pl.BlockSpec((1, 1, w, HEAD_DIM), lambda a, h: (a, h, 0, 0)),
        compiler_params=_params("parallel", "parallel"),
        name="nsa_compress",
    )(x2, pe, w1, w2)


def _nsa_cmp_kernel(tab_ref, q_ref, kv_ref, band_ref, ov_ref, ocmp_ref, qa_ref, imp_ref):
    i = pl.program_id(0)
    width = kv_ref.shape[2]
    scale = HEAD_DIM ** -0.5
    q = q_ref[...]
    lane = lax.broadcasted_iota(jnp.int32, (CHUNK, LANES), 1)
    cur = (i * CHUNK + lax.broadcasted_iota(jnp.int32, (CHUNK, LANES), 0)) // NSA_SEL_BLOCK
    forced = (lane == 0) | (lane == cur) | (lane == cur - 1)
    diag = i // _CMP_BAND_PERIOD
    head_cols = lambda h: slice(h * HEAD_DIM, (h + 1) * HEAD_DIM)

    def attend(n_tiles):
        w = n_tiles * LANES
        ov = ov_ref[:w, :]
        for kvh in range(NSA_KV_HEADS):
            kc = kv_ref[0, kvh, :w, :].astype(BF16)
            vc = kv_ref[1, kvh, :w, :].astype(BF16)
            psum = jnp.zeros((CHUNK, w), F32)
            for g in range(NSA_GROUP):
                h = kvh * NSA_GROUP + g
                far = tab_ref[(REL_BUCKETS - 1) * ATTN_HEADS + h]
                band = band_ref[h]
                tiles = [jnp.full((CHUNK, LANES), far, F32)] * (n_tiles - 2)
                tiles += [band[:, :LANES], band[:, LANES:]][-min(n_tiles, 2):]
                s = _dot_nt(q[:, head_cols(h)].astype(BF16), kc) * scale + jnp.concatenate(tiles, axis=1)
                m = jnp.max(s, axis=1, keepdims=True)
                e = jnp.exp(s - m)
                l = jnp.sum(e, axis=1, keepdims=True)
                pr = e * jnp.where(m > 0.1 * NEG, 1.0 / l, 0.0)
                ocmp_ref[:, head_cols(h)] = _dot(pr.astype(BF16), vc)
                psum = psum + pr
            hi, lo = _split_bf16(psum)
            imp_ref[kvh] = _dot(hi, ov) + _dot(lo, ov)

    for n_tiles in range(1, width // LANES + 1):
        pl.when(diag == n_tiles - 1)(functools.partial(attend, n_tiles))

    scores = [jnp.where(forced, BIG, jnp.where(lane <= cur, imp_ref[kvh], -BIG)) for kvh in range(NSA_KV_HEADS)]
    sel = _topk_mask(jnp.concatenate(scores, axis=0), NSA_SEL_TOPN)
    for kvh in range(NSA_KV_HEADS):
        pen_t = jnp.where(sel[kvh * CHUNK:(kvh + 1) * CHUNK], 0.0, NEG).T.astype(BF16)
        for g in range(NSA_GROUP):
            h = kvh * NSA_GROUP + g
            qa_ref[h, 0] = jnp.concatenate([(q[:, head_cols(h)] * _QK_SCALE).T.astype(BF16), pen_t], axis=0)


def _nsa_cmp(p, qcol, kv_cmp, band, rel_bias):
    t = p.shape[0]
    width = kv_cmp.shape[2]
    w = ATTN_HEADS * HEAD_DIM
    n = jnp.arange(width)[:, None] * NSA_CMP_STRIDE
    j = jnp.arange(LANES)[None, :] * NSA_SEL_BLOCK
    overlap = ((n < j + NSA_SEL_BLOCK) & (n + NSA_CMP_BLOCK > j)
               & (jnp.arange(width)[:, None] < width - 1)).astype(BF16)
    return pl.pallas_call(
        _nsa_cmp_kernel,
        out_shape=(jax.ShapeDtypeStruct((t, w), F32),
                   jax.ShapeDtypeStruct((ATTN_HEADS, t // CHUNK, 2 * HEAD_DIM, CHUNK), BF16)),
        grid_spec=pltpu.PrefetchScalarGridSpec(
            num_scalar_prefetch=1,
            grid=(t // CHUNK,),
            in_specs=[pl.BlockSpec((CHUNK, w), lambda i, tab: (i, qcol)),
                      pl.BlockSpec(kv_cmp.shape, lambda i, tab: (0, 0, 0, 0)),
                      pl.BlockSpec((ATTN_HEADS, None, CHUNK, 2 * LANES),
                                   lambda i, tab: (0, i % _CMP_BAND_PERIOD, 0, 0)),
                      pl.BlockSpec((width, LANES), lambda i, tab: (0, 0))],
            out_specs=(pl.BlockSpec((CHUNK, w), lambda i, tab: (i, 0)),
                       pl.BlockSpec((ATTN_HEADS, 1, 2 * HEAD_DIM, CHUNK), lambda i, tab: (0, i, 0, 0))),
            scratch_shapes=[pltpu.VMEM((NSA_KV_HEADS, CHUNK, LANES), F32)]),
        compiler_params=_params("parallel"),
        name="nsa_cmp_select",
    )(rel_bias.reshape(-1), p, kv_cmp, band, overlap)


_EVEN_ORDER = ((0, 2048), (2576, 5648), (2048, 2576))
_EVEN_WIDTH = 5760
_EVEN_TN = 1920
_OUT_TK = 1024
_DOWN_TK = 1408
_ODD_ORDER = ((0, 1024), (2584, 6680), (1024, 2584))
_ODD_WIDTH = 6912
_ODD_TN = 768


def _reorder_cols(w, order, width):
    parts = [w[:, :, a:b] for a, b in order]
    pad = width - sum(b - a for a, b in order)
    parts.append(jnp.zeros(w.shape[:2] + (pad,), w.dtype))
    return jnp.concatenate(parts, axis=2).astype(BF16)


def _even_mixer(hb, w_in_all, layer, conv_w, conv_b, dt_bias, a_log, d_skip, norm_w, bias_c):
    p = _matmul(hb, w_in_all, layer, _EVEN_TN)
    dt_t = p[:, 5632:5632 + SSD_HEADS].T
    y_ssd = _ssd(p, dt_t, conv_w, conv_b, dt_bias, a_log, d_skip, norm_w,
                 {"z": 0, "x": 1, "bc": 10, "dt": 44})
    k_aug, v_t, kmean = _kprep(p, 3, 4, ATTN_HEADS, MOBA_BLOCK, True)
    q_aug = _moba_select(p, 2, kmean)
    y_moba = _flash(q_aug, k_aug, v_t, bias_c, heads=ATTN_HEADS, group=4, shared_kv=False, out_dtype=BF16)
    return y_ssd, y_moba


def _odd_mixer(hb, w_in_all, layer, cmp_pe, cmp_w1, cmp_w2, rel_bias, bias_c, bias_w, cmp_band):
    p = _matmul(hb, w_in_all, layer, _ODD_TN)
    t = p.shape[0]
    x2 = p[:, 5120:5632].reshape(t, 2, NSA_KV_HEADS, HEAD_DIM).transpose(1, 2, 0, 3)
    x2 = x2.reshape(2, NSA_KV_HEADS, t // NSA_CMP_STRIDE, NSA_CMP_STRIDE * HEAD_DIM)
    kv_cmp = _compress(x2, cmp_pe.reshape(2, 2, NSA_CMP_STRIDE * HEAD_DIM),
                       cmp_w1.astype(BF16), cmp_w2.astype(BF16))
    o_cmp, q_aug = _nsa_cmp(p, 0, kv_cmp, cmp_band, rel_bias)
    k_aug, v_t = _kprep(p, 22, 23, NSA_KV_HEADS, NSA_SEL_BLOCK)
    o_sel = _flash(q_aug, k_aug, v_t, bias_c, heads=ATTN_HEADS, group=NSA_GROUP, shared_kv=True, out_dtype=F32)
    k_win, v_win_t = _kprep(p, 24, 25, NSA_KV_HEADS, None)
    y_nsa = _flash((p, 0), k_win, v_win_t, bias_w, heads=ATTN_HEADS, group=NSA_GROUP, shared_kv=True,
                   out_dtype=BF16, span=NSA_WINDOW // CHUNK + 1, combine=(o_cmp, o_sel, p, 52))
    y_ret = _retention(p, {"rq": 8, "rk": 16, "rv": 24, "rg": 32})
    return y_nsa, y_ret


def kernel(x, rel_bias, ev_w_in, ev_conv_w, ev_conv_b, ev_dt_bias, ev_a_log, ev_d_skip, ev_norm_w, ev_w_out,
           od_w_in, od_cmp_pe, od_cmp_w1, od_cmp_w2, od_w_out,
           ffn_w_up, ffn_conv_w, ffn_conv_b, ffn_w_down, ln_g, ln_b):
    _, t0, _ = x.shape
    t = -(-t0 // CHUNK) * CHUNK
    h = jnp.pad(x[0], ((0, t - t0), (0, 0)))
    hb = h.astype(BF16)
    bias_c = _bias_tiles(rel_bias, 8)
    bias_w = _bias_tiles(rel_bias, NSA_WINDOW // CHUNK + 1, window=NSA_WINDOW)
    cmp_band = _cmp_band(rel_bias)
    ev_w_in_b = _reorder_cols(ev_w_in, _EVEN_ORDER, _EVEN_WIDTH)
    od_w_in_b = _reorder_cols(od_w_in, _ODD_ORDER, _ODD_WIDTH)
    ev_w_out_b, od_w_out_b = ev_w_out.astype(BF16), od_w_out.astype(BF16)
    w_down_b = ffn_w_down.astype(BF16)
    for layer in range(DEPTH):
        i = layer // 2
        if layer % 2 == 0:
            mix = _even_mixer(hb, ev_w_in_b, i, ev_conv_w[i], ev_conv_b[i], ev_dt_bias[i], ev_a_log[i],
                              ev_d_skip[i], ev_norm_w[i], bias_c)
            w_out_b = ev_w_out_b
        else:
            mix = _odd_mixer(hb, od_w_in_b, i, od_cmp_pe[i], od_cmp_w1[i], od_cmp_w2[i], rel_bias,
                             bias_c, bias_w, cmp_band)
            w_out_b = od_w_out_b
        h, hb = _matmul_res_ln(mix, w_out_b, i, h, ln_g[layer, 0], ln_b[layer, 0], _OUT_TK)
        gated = _ffn_up(hb, ffn_w_up, layer, ffn_conv_w[layer], ffn_conv_b[layer])
        h, hb = _matmul_res_ln((gated,), w_down_b, layer, h, ln_g[layer, 1], ln_b[layer, 1], _DOWN_TK)
    return h[None, :t0]
```

```python
import functools
import math

import numpy as np
import jax
import jax.numpy as jnp
from jax import lax
from jax.experimental import pallas as pl
from jax.experimental.pallas import tpu as pltpu

F32 = jnp.float32
BF16 = jnp.bfloat16

D_MODEL = 2048
DEPTH = 4
HEAD_DIM = 128
ATTN_HEADS = 8
REL_BUCKETS = 32
REL_MAX_DISTANCE = 2048
SSD_HEADS = 16
SSD_HEAD_DIM = 64
SSD_INNER = 1024
SSD_GROUPS = 2
SSD_STATE = 128
SSD_CONV = 4
MOBA_BLOCK = 256
MOBA_TOPK = 3
NSA_KV_HEADS = 2
NSA_GROUP = 4
NSA_CMP_BLOCK = 32
NSA_CMP_STRIDE = 16
NSA_SEL_BLOCK = 64
NSA_SEL_TOPN = 16
NSA_WINDOW = 512
RET_HEADS = 8
ROPE_BASE = 10000.0
FFN_DIM = 5632
FFN_CONV = 3
LN_EPS = 1e-5
NEG = -1e30
BIG = 3e38
LOG2E = math.log2(math.e)
_QK_SCALE = HEAD_DIM ** -0.5 * LOG2E
DEEPNORM_ALPHA = (2 * DEPTH) ** 0.25

CHUNK = 256
LANES = 128
VMEM_LIMIT = 56 * 1024 * 1024


def _params(*sem):
    return pltpu.CompilerParams(dimension_semantics=sem, vmem_limit_bytes=VMEM_LIMIT)


def _silu(x):
    return x * (1.0 / (1.0 + jnp.exp(-x)))


def _dot(a, b):
    return jnp.dot(a, b, preferred_element_type=F32)


def _dot_nt(a, b):
    return lax.dot_general(a, b, (((1,), (1,)), ((), ())), preferred_element_type=F32)


def _dot_tn(a, b):
    return lax.dot_general(a, b, (((0,), (0,)), ((), ())), preferred_element_type=F32)


def _split_bf16(a):
    hi = a.astype(BF16)
    return hi, (a - hi.astype(F32)).astype(BF16)


def _dot_nt_split(a, b):
    ah, al = _split_bf16(a)
    bh, bl = _split_bf16(b)
    return _dot_nt(ah, bh) + (_dot_nt(ah, bl) + _dot_nt(al, bh))


def _split3_bf16(a):
    hi = a.astype(BF16)
    mid, lo = _split_bf16(a - hi.astype(F32))
    return hi, mid, lo


def _dot_f32_by_01(a, b01):
    return sum(_dot(t, b01) for t in _split3_bf16(a))


def _dot_01_by_f32(a01, b):
    return sum(_dot(a01, t) for t in _split3_bf16(b))


def _shift_rows(a, carry, k):
    rolled = pltpu.roll(a, k, 0)
    head = jnp.where(lax.broadcasted_iota(jnp.int32, carry.shape, 0) < k,
                     pltpu.roll(carry, k, 0), rolled[:8])
    return jnp.concatenate([head, rolled[8:]], axis=0)


def _mm_kernel(x_ref, w_ref, o_ref):
    o_ref[...] = _dot(x_ref[...], w_ref[...]).astype(o_ref.dtype)


def _matmul(x, w_all, layer, tn, out_dtype=F32):
    m, k = x.shape
    n = w_all.shape[2]
    tm = min(1024, m)
    return pl.pallas_call(
        _mm_kernel,
        out_shape=jax.ShapeDtypeStruct((m, n), out_dtype),
        grid=(m // tm, n // tn),
        in_specs=[pl.BlockSpec((tm, k), lambda i, j: (i, 0)),
                  pl.BlockSpec((None, k, tn), lambda i, j: (layer, 0, j))],
        out_specs=pl.BlockSpec((tm, tn), lambda i, j: (i, j)),
        compiler_params=_params("parallel", "arbitrary"),
        name="proj_matmul",
    )(x, w_all)


def _mm_ln_kernel(*refs, steps):
    a_refs = refs[:len(steps)]
    w_ref, h_ref, g_ref, b_ref, o_ref, ob_ref, acc_ref = refs[len(steps):]
    k = pl.program_id(1)

    @pl.when(k == 0)
    def _():
        acc_ref[...] = _dot(a_refs[0][...], w_ref[...])

    first = 0
    for a_ref, n in zip(a_refs, steps):
        @pl.when((k >= max(first, 1)) & (k < first + n))
        def _(a_ref=a_ref):
            acc_ref[...] += _dot(a_ref[...], w_ref[...])
        first += n

    @pl.when(k == pl.num_programs(1) - 1)
    def _():
        y = DEEPNORM_ALPHA * h_ref[...] + acc_ref[...]
        mu = jnp.mean(y, axis=-1, keepdims=True)
        yc = y - mu
        var = jnp.mean(yc * yc, axis=-1, keepdims=True)
        out = yc * lax.rsqrt(var + LN_EPS) * g_ref[...] + b_ref[...]
        o_ref[...] = out
        ob_ref[...] = out.astype(BF16)


def _matmul_res_ln(parts, w_all, layer, h, g, b, tk):
    m = h.shape[0]
    n = w_all.shape[2]
    tm = min(512, m)
    steps = tuple(a.shape[1] // tk for a in parts)
    starts = tuple(sum(steps[:p]) for p in range(len(parts)))

    def part_spec(p):
        return pl.BlockSpec((tm, tk), lambda i, j: (i, jnp.clip(j - starts[p], 0, steps[p] - 1)))

    return pl.pallas_call(
        functools.partial(_mm_ln_kernel, steps=steps),
        out_shape=(jax.ShapeDtypeStruct((m, n), F32), jax.ShapeDtypeStruct((m, n), BF16)),
        grid=(m // tm, sum(steps)),
        in_specs=[part_spec(p) for p in range(len(parts))] + [
            pl.BlockSpec((None, tk, n), lambda i, j: (layer, j, 0)),
            pl.BlockSpec((tm, n), lambda i, j: (i, 0)),
            pl.BlockSpec((1, n), lambda i, j: (0, 0)),
            pl.BlockSpec((1, n), lambda i, j: (0, 0))],
        out_specs=(pl.BlockSpec((tm, n), lambda i, j: (i, 0)),
                   pl.BlockSpec((tm, n), lambda i, j: (i, 0))),
        scratch_shapes=[pltpu.VMEM((tm, n), F32)],
        compiler_params=_params("parallel", "arbitrary"),
        name="matmul_res_ln",
    )(*parts, w_all, h, g.reshape(1, n), b.reshape(1, n))


def _ffn_up_kernel(x_ref, wa_ref, wu_ref, cw_ref, cb_ref, o_ref, carry_ref, wab_ref, wub_ref):
    @pl.when(pl.program_id(1) == 0)
    def _():
        carry_ref[...] = jnp.zeros_like(carry_ref)
        wab_ref[...] = wa_ref[...].astype(BF16)
        wub_ref[...] = wu_ref[...].astype(BF16)

    x = x_ref[...]
    a = _dot(x, wab_ref[...])
    u = _dot(x, wub_ref[...])
    carry = carry_ref[...]
    cw = cw_ref[...]
    conv = (cw[0:1] * _shift_rows(a, carry, 2) + cw[1:2] * _shift_rows(a, carry, 1)
            + cw[2:3] * a + cb_ref[...])
    carry_ref[...] = a[a.shape[0] - 8:]
    o_ref[...] = (_silu(conv) * u).astype(BF16)


def _ffn_up(x, w_up_all, layer, conv_w, conv_b):
    m, k = x.shape
    tn = 512
    tm = min(1024, m)
    nj = FFN_DIM // tn
    return pl.pallas_call(
        _ffn_up_kernel,
        out_shape=jax.ShapeDtypeStruct((m, FFN_DIM), BF16),
        grid=(nj, m // tm),
        in_specs=[pl.BlockSpec((tm, k), lambda j, i: (i, 0)),
                  pl.BlockSpec((None, k, tn), lambda j, i: (layer, 0, j)),
                  pl.BlockSpec((None, k, tn), lambda j, i: (layer, 0, j + nj)),
                  pl.BlockSpec((FFN_CONV, tn), lambda j, i: (0, j)),
                  pl.BlockSpec((1, tn), lambda j, i: (0, j))],
        out_specs=pl.BlockSpec((tm, tn), lambda j, i: (i, j)),
        scratch_shapes=[pltpu.VMEM((8, tn), F32), pltpu.VMEM((k, tn), BF16), pltpu.VMEM((k, tn), BF16)],
        compiler_params=_params("parallel", "arbitrary"),
        name="ffn_up_conv",
    )(x, w_up_all, w_up_all, conv_w, conv_b.reshape(1, FFN_DIM))


def _ssd_kernel(z_ref, x_ref, bc_ref, dt_ref, dtt_ref, cwx_ref, cbx_ref, cwb_ref, cbb_ref,
                dtb_ref, alog_ref, dtbc_ref, alogc_ref, dskip_ref, nw_ref, expand_ref,
                o_ref, cx_ref, cbc_ref, state_ref):
    L = CHUNK
    hpg = SSD_HEADS // SSD_GROUPS
    gw = hpg * SSD_HEAD_DIM

    @pl.when(pl.program_id(0) == 0)
    def _():
        cx_ref[...] = jnp.zeros_like(cx_ref)
        cbc_ref[...] = jnp.zeros_like(cbc_ref)
        state_ref[...] = jnp.zeros_like(state_ref)

    def conv(raw, carry_ref, w_ref, b_ref):
        carry = carry_ref[...]
        w = w_ref[...]
        y = (w[0:1] * _shift_rows(raw, carry, 3) + w[1:2] * _shift_rows(raw, carry, 2)
             + w[2:3] * _shift_rows(raw, carry, 1) + w[3:4] * raw + b_ref[...])
        carry_ref[...] = raw[L - 8:]
        return _silu(y)

    xs = conv(x_ref[...], cx_ref, cwx_ref, cbx_ref)
    bcs = conv(bc_ref[...], cbc_ref, cwb_ref, cbb_ref)

    def softplus(v):
        return jnp.maximum(v, 0.0) + jnp.log(1.0 + jnp.exp(-jnp.abs(v)))

    dt = softplus(dt_ref[...] + dtb_ref[...])
    da = dt * (-jnp.exp(alog_ref[...]))
    row = lax.broadcasted_iota(jnp.int32, (L, L), 0)
    col = lax.broadcasted_iota(jnp.int32, (L, L), 1)
    causal = row >= col
    a_cum = _dot_01_by_f32(causal.astype(BF16), da)
    dt_r = softplus(dtt_ref[...] + dtbc_ref[...])
    da_r = dt_r * (-jnp.exp(alogc_ref[...]))
    a_cum_r = _dot_f32_by_01(da_r, (row <= col).astype(BF16))

    expand = expand_ref[...]

    def widen(v):
        return _dot_f32_by_01(v, expand)

    a_last = a_cum[L - 1:L]
    dt_w = widen(dt)
    decay_in_w = widen(jnp.exp(a_cum))
    decay_end_w = widen(jnp.exp(a_last - a_cum))
    chunk_decay_w = widen(jnp.broadcast_to(jnp.exp(a_last), (8, LANES)))[0:1]
    xdt = xs * dt_w
    xdt_b = xdt.astype(BF16)
    xdt_end_b = (xdt * decay_end_w).astype(BF16)
    lane_lo = lax.broadcasted_iota(jnp.int32, (L, LANES), 1) < SSD_HEAD_DIM

    ys = []
    for g in range(SSD_GROUPS):
        b_g = bcs[:, g * SSD_STATE:(g + 1) * SSD_STATE].astype(BF16)
        c_g = bcs[:, (SSD_GROUPS + g) * SSD_STATE:(SSD_GROUPS + g + 1) * SSD_STATE].astype(BF16)
        cb = _dot_nt(c_g, b_g)
        pieces = []
        for pair in range(hpg // 2):
            outs = []
            for k in range(2):
                h = g * hpg + pair * 2 + k
                diff = a_cum[:, h:h + 1] - a_cum_r[h:h + 1, :]
                w = (cb * jnp.exp(jnp.where(causal, diff, NEG))).astype(BF16)
                lo = (g * hpg + pair * 2) * SSD_HEAD_DIM
                outs.append(_dot(w, xdt_b[:, lo:lo + LANES]))
            pieces.append(jnp.where(lane_lo, outs[0], outs[1]))
        y_diag = jnp.concatenate(pieces, axis=1)
        state = state_ref[g]
        sl = slice(g * gw, (g + 1) * gw)
        y_off = _dot(c_g, state.astype(BF16)) * decay_in_w[:, sl]
        state_ref[g] = state * chunk_decay_w[:, sl] + _dot_tn(b_g, xdt_end_b[:, sl])
        ys.append(y_diag + y_off)
    y = jnp.concatenate(ys, axis=1) + xs * dskip_ref[...]
    yg = y * _silu(z_ref[...])
    half = SSD_INNER // SSD_GROUPS
    outs = []
    for g in range(SSD_GROUPS):
        part = yg[:, g * half:(g + 1) * half]
        outs.append(part * lax.rsqrt(jnp.mean(part * part, axis=-1, keepdims=True) + LN_EPS))
    o_ref[...] = (jnp.concatenate(outs, axis=1) * nw_ref[...]).astype(BF16)


def _ssd(p, dt_t, conv_w, conv_b, dt_bias, a_log, d_skip, norm_w, col):
    t = p.shape[0]
    L = CHUNK
    pad16 = lambda v: jnp.pad(v.astype(F32), (0, LANES - SSD_HEADS)).reshape(1, LANES)
    expand = (jnp.arange(LANES)[:, None] == (jnp.arange(SSD_INNER) // SSD_HEAD_DIM)[None, :]).astype(BF16)
    full = lambda shape: pl.BlockSpec(shape, lambda c: (0,) * len(shape))
    return pl.pallas_call(
        _ssd_kernel,
        out_shape=jax.ShapeDtypeStruct((t, SSD_INNER), BF16),
        grid=(t // L,),
        in_specs=[pl.BlockSpec((L, 1024), lambda c: (c, col["z"])),
                  pl.BlockSpec((L, 1024), lambda c: (c, col["x"])),
                  pl.BlockSpec((L, 512), lambda c: (c, col["bc"])),
                  pl.BlockSpec((L, LANES), lambda c: (c, col["dt"])),
                  pl.BlockSpec((SSD_HEADS, L), lambda c: (0, c)),
                  full((SSD_CONV, 1024)), full((1, 1024)), full((SSD_CONV, 512)), full((1, 512)),
                  full((1, LANES)), full((1, LANES)), full((SSD_HEADS, 1)), full((SSD_HEADS, 1)),
                  full((1, 1024)), full((1, 1024)), full((LANES, 1024))],
        out_specs=pl.BlockSpec((L, SSD_INNER), lambda c: (c, 0)),
        scratch_shapes=[pltpu.VMEM((8, 1024), F32), pltpu.VMEM((8, 512), F32),
                        pltpu.VMEM((SSD_GROUPS, SSD_STATE, 512), F32)],
        compiler_params=_params("arbitrary"),
        name="ssd_mixer",
    )(p, p, p, p, dt_t,
      conv_w[:, :SSD_INNER], conv_b[:SSD_INNER].reshape(1, -1),
      conv_w[:, SSD_INNER:], conv_b[SSD_INNER:].reshape(1, -1),
      pad16(dt_bias), pad16(a_log), dt_bias.reshape(-1, 1), a_log.reshape(-1, 1),
      jnp.repeat(d_skip, SSD_HEAD_DIM).reshape(1, -1), norm_w.reshape(1, -1), expand)


_RET_GROUP = 4


def _ret_kernel(logg_ref, q_ref, k_ref, v_ref, g_ref, cos_ref, sin_ref, o_ref,
                state_ref, decay_ref, qdec_ref, kdec_ref, cdec_ref):
    C = CHUNK
    hg = pl.program_id(0)
    head_cols = lambda g: slice(g * HEAD_DIM, (g + 1) * HEAD_DIM)

    @pl.when(pl.program_id(1) == 0)
    def _():
        state_ref[...] = jnp.zeros_like(state_ref)
        rel = lax.broadcasted_iota(jnp.int32, (C, C), 0) - lax.broadcasted_iota(jnp.int32, (C, C), 1)
        pos = lax.broadcasted_iota(jnp.int32, (C, 1), 0).astype(F32)
        for g in range(_RET_GROUP):
            lg = logg_ref[hg * _RET_GROUP + g]
            decay_ref[g] = jnp.where(rel >= 0, jnp.exp(lg * jnp.maximum(rel, 0).astype(F32)), 0.0)
            qdec_ref[g] = jnp.exp(lg * (pos + 1.0))
            kdec_ref[g] = jnp.exp(lg * (C - 1.0 - pos))
            cdec_ref[g] = jnp.exp(lg * jnp.full((1, HEAD_DIM), float(C), F32))

    cos = cos_ref[...]
    sin = sin_ref[...]
    even_lane = lax.broadcasted_iota(jnp.int32, (C, HEAD_DIM), 1) % 2 == 0

    def rot(x):
        swapped = jnp.where(even_lane, pltpu.roll(x, HEAD_DIM - 1, 1), pltpu.roll(x, 1, 1))
        return x * cos + swapped * sin

    qs = [rot(q_ref[:, head_cols(g)]) for g in range(_RET_GROUP)]
    ks = [rot(k_ref[:, head_cols(g)]) * (HEAD_DIM ** -0.5) for g in range(_RET_GROUP)]
    vs = [v_ref[:, head_cols(g)].astype(BF16) for g in range(_RET_GROUP)]
    inners = [_dot_nt(qs[g].astype(BF16), ks[g].astype(BF16)) for g in range(_RET_GROUP)]
    for g in range(_RET_GROUP):
        state = state_ref[g]
        y = _dot((inners[g] * decay_ref[g]).astype(BF16), vs[g])
        y = y + _dot((qs[g] * qdec_ref[g]).astype(BF16), state.astype(BF16))
        state_ref[g] = state * cdec_ref[g] + _dot_tn((ks[g] * kdec_ref[g]).astype(BF16), vs[g])
        mu = jnp.mean(y, axis=-1, keepdims=True)
        yc = y - mu
        var = jnp.mean(yc * yc, axis=-1, keepdims=True)
        o_ref[:, head_cols(g)] = (_silu(g_ref[:, head_cols(g)]) * (yc * lax.rsqrt(var + LN_EPS))).astype(BF16)


def _retention(p, col):
    t = p.shape[0]
    C = CHUNK
    G = _RET_GROUP
    inv = 1.0 / (ROPE_BASE ** (jnp.arange(0, HEAD_DIM, 2, dtype=F32) / HEAD_DIM))
    ang = jnp.arange(t, dtype=F32)[:, None] * inv[None, :]
    cos = jnp.repeat(jnp.cos(ang), 2, axis=1)
    sin = jnp.stack([-jnp.sin(ang), jnp.sin(ang)], axis=-1).reshape(t, HEAD_DIM)
    log_g = jnp.log(1.0 - 2.0 ** (-5.0 - jnp.arange(RET_HEADS, dtype=F32)))
    blk = lambda name: pl.BlockSpec((C, G * HEAD_DIM), lambda h, c, lg: (c, col[name] // G + h))
    tab = pl.BlockSpec((C, HEAD_DIM), lambda h, c, lg: (c, 0))
    return pl.pallas_call(
        _ret_kernel,
        out_shape=jax.ShapeDtypeStruct((t, RET_HEADS * HEAD_DIM), BF16),
        grid_spec=pltpu.PrefetchScalarGridSpec(
            num_scalar_prefetch=1,
            grid=(RET_HEADS // G, t // C),
            in_specs=[blk("rq"), blk("rk"), blk("rv"), blk("rg"), tab, tab],
            out_specs=pl.BlockSpec((C, G * HEAD_DIM), lambda h, c, lg: (c, h)),
            scratch_shapes=[pltpu.VMEM((G, HEAD_DIM, HEAD_DIM), F32), pltpu.VMEM((G, C, C), F32),
                            pltpu.VMEM((G, C, 1), F32), pltpu.VMEM((G, C, 1), F32),
                            pltpu.VMEM((G, 1, HEAD_DIM), F32)]),
        compiler_params=_params("parallel", "arbitrary"),
        name="retention_mixer",
    )(log_g, p, p, p, p, cos, sin)


def _rel_bucket(dist):
    n = jnp.maximum(dist, 0)
    max_exact = REL_BUCKETS // 2
    nf = jnp.maximum(n, max_exact).astype(F32)
    large = max_exact + (jnp.log(nf / max_exact) / math.log(REL_MAX_DISTANCE / max_exact)
                         * (REL_BUCKETS - max_exact)).astype(jnp.int32)
    large = jnp.minimum(large, REL_BUCKETS - 1)
    return jnp.where(n < max_exact, n, large)


def _bias_lookup(tab_ref, bucket, h):
    acc = jnp.full(bucket.shape, tab_ref[h], F32)
    for b in range(1, REL_BUCKETS):
        acc = jnp.where(bucket == b, tab_ref[b * ATTN_HEADS + h], acc)
    return acc


def _bias_tile_kernel(tab_ref, o_ref, *, window):
    h = pl.program_id(0)
    o = pl.program_id(1)
    row = lax.broadcasted_iota(jnp.int32, (CHUNK, CHUNK), 0)
    col = lax.broadcasted_iota(jnp.int32, (CHUNK, CHUNK), 1)
    d = o * CHUNK + col - row
    valid = d >= 0
    if window is not None:
        valid = valid & (d < window)
    o_ref[0, 0] = jnp.where(valid, _bias_lookup(tab_ref, _rel_bucket(d), h) * LOG2E, NEG)


def _bias_tiles(rel_bias, n_off, window=None):
    return pl.pallas_call(
        functools.partial(_bias_tile_kernel, window=window),
        out_shape=jax.ShapeDtypeStruct((ATTN_HEADS, n_off, CHUNK, CHUNK), F32),
        grid_spec=pltpu.PrefetchScalarGridSpec(
            num_scalar_prefetch=1, grid=(ATTN_HEADS, n_off), in_specs=[],
            out_specs=pl.BlockSpec((1, 1, CHUNK, CHUNK), lambda h, o, tab: (h, o, 0, 0))),
        compiler_params=_params("parallel", "parallel"),
        name="rel_bias_tiles",
    )(rel_bias.reshape(-1))


_CMP_BAND_PERIOD = LANES * NSA_CMP_STRIDE // CHUNK


def _cmp_band_kernel(tab_ref, o_ref):
    h = pl.program_id(0)
    b = pl.program_id(1)
    row = lax.broadcasted_iota(jnp.int32, (CHUNK, 2 * LANES), 0)
    col = lax.broadcasted_iota(jnp.int32, (CHUNK, 2 * LANES), 1)
    d = b * CHUNK + row - NSA_CMP_STRIDE * (col - LANES) - (NSA_CMP_BLOCK - 1)
    o_ref[0, 0] = jnp.where(d >= 0, _bias_lookup(tab_ref, _rel_bucket(d), h), NEG)


def _cmp_band(rel_bias):
    return pl.pallas_call(
        _cmp_band_kernel,
        out_shape=jax.ShapeDtypeStruct((ATTN_HEADS, _CMP_BAND_PERIOD, CHUNK, 2 * LANES), F32),
        grid_spec=pltpu.PrefetchScalarGridSpec(
            num_scalar_prefetch=1, grid=(ATTN_HEADS, _CMP_BAND_PERIOD), in_specs=[],
            out_specs=pl.BlockSpec((1, 1, CHUNK, 2 * LANES), lambda h, b, tab: (h, b, 0, 0))),
        compiler_params=_params("parallel", "parallel"),
        name="cmp_bias_band",
    )(rel_bias.reshape(-1))


def _kprep_kernel(k_ref, v_ref, ka_ref, vt_ref, *rest, heads, block, with_mean):
    i = pl.program_id(0)
    k = k_ref[...]
    v = v_ref[...]
    if block is not None:
        row = lax.broadcasted_iota(jnp.int32, (CHUNK, LANES), 0)
        lane = lax.broadcasted_iota(jnp.int32, (CHUNK, LANES), 1)
        onehot = (lane == (i * CHUNK + row) // block).astype(BF16)
    for h in range(heads):
        sl = slice(h * HEAD_DIM, (h + 1) * HEAD_DIM)
        kh = k[:, sl].astype(BF16)
        ka_ref[h, 0] = kh if block is None else jnp.concatenate([kh, onehot], axis=1)
        vt_ref[h, 0] = v[:, sl].T.astype(BF16)
    if with_mean:
        km_ref = rest[0]

        @pl.when(i == 0)
        def _():
            km_ref[...] = jnp.zeros_like(km_ref)

        km_ref[pl.ds(i, 1), :] = jnp.mean(k, axis=0, keepdims=True)


def _kprep(p, kcol, vcol, heads, block, with_mean=False):
    t = p.shape[0]
    nt = t // CHUNK
    w = heads * HEAD_DIM
    kw = HEAD_DIM if block is None else 2 * HEAD_DIM
    out_shape = [jax.ShapeDtypeStruct((heads, nt, CHUNK, kw), BF16),
                 jax.ShapeDtypeStruct((heads, nt, HEAD_DIM, CHUNK), BF16)]
    out_specs = [pl.BlockSpec((heads, 1, CHUNK, kw), lambda i: (0, i, 0, 0)),
                 pl.BlockSpec((heads, 1, HEAD_DIM, CHUNK), lambda i: (0, i, 0, 0))]
    if with_mean:
        out_shape.append(jax.ShapeDtypeStruct((LANES, w), F32))
        out_specs.append(pl.BlockSpec((LANES, w), lambda i: (0, 0)))
    return pl.pallas_call(
        functools.partial(_kprep_kernel, heads=heads, block=block, with_mean=with_mean),
        out_shape=tuple(out_shape),
        grid=(t // CHUNK,),
        in_specs=[pl.BlockSpec((CHUNK, w), lambda i: (i, kcol)),
                  pl.BlockSpec((CHUNK, w), lambda i: (i, vcol))],
        out_specs=tuple(out_specs),
        compiler_params=_params("arbitrary"),
        name="key_prep",
    )(p, p)


def _topk_mask_t(score_t, k):
    row = lax.broadcasted_iota(jnp.int32, score_t.shape, 0)

    def body(_, carry):
        g, sel = carry
        m = jnp.max(g, axis=0, keepdims=True)
        idx = jnp.min(jnp.where(g == m, row, score_t.shape[0]), axis=0, keepdims=True)
        hit = row == idx
        sel = jnp.where(hit & (m > -0.5 * BIG), 1.0, sel)
        return jnp.where(hit, -BIG, g), sel

    _, sel = lax.fori_loop(0, k, body, (score_t, jnp.zeros(score_t.shape, F32)))
    return sel > 0.5


def _moba_select_kernel(q_ref, km_ref, qa_ref):
    i = pl.program_id(0)
    q = q_ref[...]
    km = km_ref[...]
    block = lax.broadcasted_iota(jnp.int32, (LANES, CHUNK), 0)
    head_cols = lambda h: slice(h * HEAD_DIM, (h + 1) * HEAD_DIM)
    gates = [jnp.where(block < i, _dot_nt_split(km[:, head_cols(h)], q[:, head_cols(h)]), -BIG)
             for h in range(ATTN_HEADS)]
    sel = _topk_mask_t(jnp.concatenate(gates, axis=1), MOBA_TOPK)
    own = block == i
    for h in range(ATTN_HEADS):
        pen_t = jnp.where(sel[:, h * CHUNK:(h + 1) * CHUNK] | own, 0.0, NEG)
        q_t = (q[:, head_cols(h)] * _QK_SCALE).T
        qa_ref[h, 0] = jnp.concatenate([q_t, pen_t], axis=0).astype(BF16)


def _moba_select(p, qcol, kmean):
    t = p.shape[0]
    w = ATTN_HEADS * HEAD_DIM
    return pl.pallas_call(
        _moba_select_kernel,
        out_shape=jax.ShapeDtypeStruct((ATTN_HEADS, t // CHUNK, 2 * HEAD_DIM, CHUNK), BF16),
        grid=(t // CHUNK,),
        in_specs=[pl.BlockSpec((CHUNK, w), lambda i: (i, qcol)),
                  pl.BlockSpec((LANES, w), lambda i: (0, 0))],
        out_specs=pl.BlockSpec((ATTN_HEADS, 1, 2 * HEAD_DIM, CHUNK), lambda i: (0, i, 0, 0)),
        compiler_params=_params("parallel"),
        name="moba_select",
    )(p, kmean)


def _flash_kernel(q_ref, k_ref, vt_ref, bias_ref, *rest, group, shared_kv, span, combine):
    if combine:
        ocmp_ref, osel_ref, gate_ref, o_ref = rest[:4]
    else:
        o_ref = rest[0]
    m_ref, l_ref, acc_ref, alpha_ref, pt_ref, s_even_ref, s_odd_ref = rest[-7:]
    hg = pl.program_id(0)
    i = pl.program_id(1)
    n_off = bias_ref.shape[1]
    nt = k_ref.shape[-3]
    head_cols = lambda g: slice(g * HEAD_DIM, (g + 1) * HEAD_DIM)
    if len(q_ref.shape) == 3:
        qts = [q_ref[g] for g in range(group)]
    else:
        qts = [(q_ref[:, head_cols(g)] * _QK_SCALE).T.astype(BF16) for g in range(group)]
    m_ref[...] = jnp.full_like(m_ref, -BIG)
    l_ref[...] = jnp.zeros_like(l_ref)
    acc_ref[...] = jnp.zeros_like(acc_ref)
    k_tile = lambda g, j: k_ref[j] if shared_kv else k_ref[g, j]
    vt_tile = lambda g, j: vt_ref[j] if shared_kv else vt_ref[g, j]
    alpha_ref[...] = jnp.ones_like(alpha_ref)
    pt_ref[...] = jnp.zeros_like(pt_ref)

    def scores(j, s_ref):
        for g in range(group):
            s_ref[g] = _dot(k_tile(g, j), qts[g])

    def value_update(j):
        for g in range(group):
            acc_ref[g] = alpha_ref[g] * acc_ref[g] + _dot(vt_tile(g, j), pt_ref[g])

    def step(j, s_ref, s_next_ref, const_bias):
        scores(jnp.minimum(j + 1, nt - 1), s_next_ref)
        value_update(jnp.maximum(j - 1, 0))
        for g in range(group):
            m_prev = m_ref[g]
            if const_bias:
                far = bias_ref[g, n_off - 1, 0:1, :]
                st = s_ref[g]
                m_new = jnp.maximum(m_prev, jnp.max(st, axis=0, keepdims=True) + far)
                pt = jnp.exp2(st - (m_new - far))
            else:
                st = s_ref[g] + bias_ref[g, i - j]
                m_new = jnp.maximum(m_prev, jnp.max(st, axis=0, keepdims=True))
                pt = jnp.exp2(st - m_new)
            alpha = jnp.exp2(m_prev - m_new)
            l_ref[g] = alpha * l_ref[g] + jnp.sum(pt, axis=0, keepdims=True)
            m_ref[g] = m_new
            alpha_ref[g] = alpha
            pt_ref[g] = pt.astype(BF16)

    def pairs(start, count, const_bias):
        def body(t, carry):
            j = start + 2 * t
            step(j, s_even_ref, s_odd_ref, const_bias)
            step(j + 1, s_odd_ref, s_even_ref, const_bias)
            return carry
        lax.fori_loop(0, count, body, 0)

    if span is None:
        n_far = jnp.maximum(i - (n_off - 2), 0) // 2 * 2
        j_lo = 0
    else:
        n_far = 0
        j_lo = jnp.maximum(i - (span - 1), 0)
    scores(j_lo, s_even_ref)
    if span is None:
        pairs(j_lo, n_far // 2, True)
    near_lo = j_lo + n_far
    n_near = i + 1 - near_lo
    pairs(near_lo, n_near // 2, False)

    @pl.when(n_near % 2 == 1)
    def _():
        step(i, s_even_ref, s_odd_ref, False)

    value_update(i)
    if combine:
        sig = 1.0 / (1.0 + jnp.exp(-gate_ref[...]))
        lane = lax.broadcasted_iota(jnp.int32, sig.shape, 1)
    for g in range(group):
        out = (acc_ref[g] * (1.0 / l_ref[g])).T
        if combine:
            h = hg * group + g
            pick = lambda c: jnp.sum(jnp.where(lane == 3 * h + c, sig, 0.0), axis=1, keepdims=True)
            out = pick(0) * ocmp_ref[:, head_cols(g)] + pick(1) * osel_ref[:, head_cols(g)] + pick(2) * out
        o_ref[:, head_cols(g)] = out.astype(o_ref.dtype)


def _flash(q, k, vt, bias, *, heads, group, shared_kv, out_dtype, span=None, combine=None):
    n_off = bias.shape[1]
    nt = k.shape[1]
    gw = group * HEAD_DIM
    if isinstance(q, tuple):
        q, base = q
        q_spec = pl.BlockSpec((CHUNK, gw), lambda hg, i: (i, base // group + hg))
    else:
        q_spec = pl.BlockSpec((group, None) + q.shape[2:], lambda hg, i: (hg, i, 0, 0))
    kv_block = (None,) if shared_kv else (group,)
    resident = lambda shape: pl.BlockSpec(shape, lambda hg, i: (hg, 0, 0, 0), pipeline_mode=pl.Buffered(1))
    in_specs = [q_spec, resident(kv_block + k.shape[1:]), resident(kv_block + vt.shape[1:]),
                resident((group, n_off, CHUNK, CHUNK))]
    args = [q, k, vt, bias]
    tile = pl.BlockSpec((CHUNK, gw), lambda hg, i: (i, hg))
    if combine is not None:
        ocmp, osel, p, gcol = combine
        in_specs += [tile, tile, pl.BlockSpec((CHUNK, LANES), lambda hg, i: (i, gcol))]
        args += [ocmp, osel, p]
    return pl.pallas_call(
        functools.partial(_flash_kernel, group=group, shared_kv=shared_kv, span=span,
                          combine=combine is not None),
        out_shape=jax.ShapeDtypeStruct((nt * CHUNK, heads * HEAD_DIM), out_dtype),
        grid=(heads // group, nt),
        in_specs=in_specs,
        out_specs=tile,
        scratch_shapes=[pltpu.VMEM((group, 1, CHUNK), F32),
                        pltpu.VMEM((group, 1, CHUNK), F32),
                        pltpu.VMEM((group, HEAD_DIM, CHUNK), F32), pltpu.VMEM((group, 1, CHUNK), F32),
                        pltpu.VMEM((group, CHUNK, CHUNK), BF16),
                        pltpu.VMEM((group, CHUNK, CHUNK), F32), pltpu.VMEM((group, CHUNK, CHUNK), F32)],
        compiler_params=_params("parallel", "arbitrary"),
        name="flash_attention",
    )(*args)


def _compress_kernel(x_ref, pe_ref, w1_ref, w2_ref, o_ref):
    x = x_ref[0, 0]
    pe = pe_ref[0]
    half = NSA_CMP_STRIDE * HEAD_DIM
    a = _dot((x + pe[0:1]).astype(BF16), w1_ref[0, :half, :])
    b = _dot((x + pe[1:2]).astype(BF16), w1_ref[0, half:, :])
    hid = a + pltpu.roll(b, x.shape[0] - 1, 0)
    gelu = 0.5 * hid * (1.0 + jnp.tanh(math.sqrt(2.0 / math.pi) * (hid + 0.044715 * hid * hid * hid)))
    o_ref[0, 0] = _dot(gelu.astype(BF16), w2_ref[0])


def _compress(x2, pe, w1, w2):
    _, hkv, w, kd = x2.shape
    return pl.pallas_call(
        _compress_kernel,
        out_shape=jax.ShapeDtypeStruct((2, hkv, w, HEAD_DIM), F32),
        grid=(2, hkv),
        in_specs=[pl.BlockSpec((1, 1, w, kd), lambda a, h: (a, h, 0, 0)),
                  pl.BlockSpec((1, 2, kd), lambda a, h: (a, 0, 0)),
                  pl.BlockSpec((1, 2 * kd, HEAD_DIM), lambda a, h: (a, 0, 0)),
                  pl.BlockSpec((1, HEAD_DIM, HEAD_DIM), lambda a, h: (a, 0, 0))],
        out_specs=pl.BlockSpec((1, 1, w, HEAD_DIM), lambda a, h: (a, h, 0, 0)),
        compiler_params=_params("parallel", "parallel"),
        name="nsa_compress",
    )(x2, pe, w1, w2)


def _nsa_cmp_kernel(tab_ref, q_ref, kv_ref, band_ref, ov_ref, ocmp_ref, qa_ref, imp_ref):
    i = pl.program_id(0)
    width = kv_ref.shape[2]
    scale = HEAD_DIM ** -0.5
    q = q_ref[...]
    block = lax.broadcasted_iota(jnp.int32, (LANES, CHUNK), 0)
    cur = (i * CHUNK + lax.broadcasted_iota(jnp.int32, (LANES, CHUNK), 1)) // NSA_SEL_BLOCK
    forced = (block == 0) | (block == cur) | (block == cur - 1)
    diag = i // _CMP_BAND_PERIOD
    head_cols = lambda h: slice(h * HEAD_DIM, (h + 1) * HEAD_DIM)

    def attend(n_tiles):
        w = n_tiles * LANES
        ov_t = ov_ref[:, :w]
        for kvh in range(NSA_KV_HEADS):
            kc = kv_ref[0, kvh, :w, :].astype(BF16)
            vc = kv_ref[1, kvh, :w, :].astype(BF16)
            psum = jnp.zeros((CHUNK, w), F32)
            for g in range(NSA_GROUP):
                h = kvh * NSA_GROUP + g
                far = tab_ref[(REL_BUCKETS - 1) * ATTN_HEADS + h]
                band = band_ref[h]
                tiles = [jnp.full((CHUNK, LANES), far, F32)] * (n_tiles - 2)
                tiles += [band[:, :LANES], band[:, LANES:]][-min(n_tiles, 2):]
                s = _dot_nt(q[:, head_cols(h)].astype(BF16), kc) * scale + jnp.concatenate(tiles, axis=1)
                m = jnp.max(s, axis=1, keepdims=True)
                e = jnp.exp(s - m)
                l = jnp.sum(e, axis=1, keepdims=True)
                pr = e * jnp.where(m > 0.1 * NEG, 1.0 / l, 0.0)
                ocmp_ref[:, head_cols(h)] = _dot(pr.astype(BF16), vc)
                psum = psum + pr
            hi, lo = _split_bf16(psum)
            imp_ref[kvh] = _dot_nt(ov_t, hi) + _dot_nt(ov_t, lo)

    for n_tiles in range(1, width // LANES + 1):
        pl.when(diag == n_tiles - 1)(functools.partial(attend, n_tiles))

    scores = [jnp.where(forced, BIG, jnp.where(block <= cur, imp_ref[kvh], -BIG)) for kvh in range(NSA_KV_HEADS)]
    sel = _topk_mask_t(jnp.concatenate(scores, axis=1), NSA_SEL_TOPN)
    for kvh in range(NSA_KV_HEADS):
        pen_t = jnp.where(sel[:, kvh * CHUNK:(kvh + 1) * CHUNK], 0.0, NEG).astype(BF16)
        for g in range(NSA_GROUP):
            h = kvh * NSA_GROUP + g
            qa_ref[h, 0] = jnp.concatenate([(q[:, head_cols(h)] * _QK_SCALE).T.astype(BF16), pen_t], axis=0)


def _nsa_cmp(p, qcol, kv_cmp, band, rel_bias):
    t = p.shape[0]
    width = kv_cmp.shape[2]
    w = ATTN_HEADS * HEAD_DIM
    n = jnp.arange(width)[None, :] * NSA_CMP_STRIDE
    j = jnp.arange(LANES)[:, None] * NSA_SEL_BLOCK
    overlap = ((n < j + NSA_SEL_BLOCK) & (n + NSA_CMP_BLOCK > j)
               & (jnp.arange(width)[None, :] < width - 1)).astype(BF16)
    return pl.pallas_call(
        _nsa_cmp_kernel,
        out_shape=(jax.ShapeDtypeStruct((t, w), F32),
                   jax.ShapeDtypeStruct((ATTN_HEADS, t // CHUNK, 2 * HEAD_DIM, CHUNK), BF16)),
        grid_spec=pltpu.PrefetchScalarGridSpec(
            num_scalar_prefetch=1,
            grid=(t // CHUNK,),
            in_specs=[pl.BlockSpec((CHUNK, w), lambda i, tab: (i, qcol)),
                      pl.BlockSpec(kv_cmp.shape, lambda i, tab: (0, 0, 0, 0)),
                      pl.BlockSpec((ATTN_HEADS, None, CHUNK, 2 * LANES),
                                   lambda i, tab: (0, i % _CMP_BAND_PERIOD, 0, 0)),
                      pl.BlockSpec((LANES, width), lambda i, tab: (0, 0))],
            out_specs=(pl.BlockSpec((CHUNK, w), lambda i, tab: (i, 0)),
                       pl.BlockSpec((ATTN_HEADS, 1, 2 * HEAD_DIM, CHUNK), lambda i, tab: (0, i, 0, 0))),
            scratch_shapes=[pltpu.VMEM((NSA_KV_HEADS, LANES, CHUNK), F32)]),
        compiler_params=_params("parallel"),
        name="nsa_cmp_select",
    )(rel_bias.reshape(-1), p, kv_cmp, band, overlap)


_EVEN_ORDER = ((0, 2048), (2576, 5648), (2048, 2576))
_EVEN_WIDTH = 5760
_EVEN_TN = 1920
_OUT_TK = 1024
_DOWN_TK = 1408
_ODD_ORDER = ((0, 1024), (2584, 6680), (1024, 2584))
_ODD_WIDTH = 6912
_ODD_TN = 768


def _reorder_cols(w, order, width):
    parts = [w[:, :, a:b] for a, b in order]
    pad = width - sum(b - a for a, b in order)
    parts.append(jnp.zeros(w.shape[:2] + (pad,), w.dtype))
    return jnp.concatenate(parts, axis=2).astype(BF16)


def _even_mixer(hb, w_in_all, layer, conv_w, conv_b, dt_bias, a_log, d_skip, norm_w, bias_c):
    p = _matmul(hb, w_in_all, layer, _EVEN_TN)
    dt_t = p[:, 5632:5632 + SSD_HEADS].T
    y_ssd = _ssd(p, dt_t, conv_w, conv_b, dt_bias, a_log, d_skip, norm_w,
                 {"z": 0, "x": 1, "bc": 10, "dt": 44})
    k_aug, v_t, kmean = _kprep(p, 3, 4, ATTN_HEADS, MOBA_BLOCK, True)
    q_aug = _moba_select(p, 2, kmean)
    y_moba = _flash(q_aug, k_aug, v_t, bias_c, heads=ATTN_HEADS, group=4, shared_kv=False, out_dtype=BF16)
    return y_ssd, y_moba


def _odd_mixer(hb, w_in_all, layer, cmp_pe, cmp_w1, cmp_w2, rel_bias, bias_c, bias_w, cmp_band):
    p = _matmul(hb, w_in_all, layer, _ODD_TN)
    t = p.shape[0]
    x2 = p[:, 5120:5632].reshape(t, 2, NSA_KV_HEADS, HEAD_DIM).transpose(1, 2, 0, 3)
    x2 = x2.reshape(2, NSA_KV_HEADS, t // NSA_CMP_STRIDE, NSA_CMP_STRIDE * HEAD_DIM)
    kv_cmp = _compress(x2, cmp_pe.reshape(2, 2, NSA_CMP_STRIDE * HEAD_DIM),
                       cmp_w1.astype(BF16), cmp_w2.astype(BF16))
    o_cmp, q_aug = _nsa_cmp(p, 0, kv_cmp, cmp_band, rel_bias)
    k_aug, v_t = _kprep(p, 22, 23, NSA_KV_HEADS, NSA_SEL_BLOCK)
    o_sel = _flash(q_aug, k_aug, v_t, bias_c, heads=ATTN_HEADS, group=NSA_GROUP, shared_kv=True, out_dtype=F32)
    k_win, v_win_t = _kprep(p, 24, 25, NSA_KV_HEADS, None)
    y_nsa = _flash((p, 0), k_win, v_win_t, bias_w, heads=ATTN_HEADS, group=NSA_GROUP, shared_kv=True,
                   out_dtype=BF16, span=NSA_WINDOW // CHUNK + 1, combine=(o_cmp, o_sel, p, 52))
    y_ret = _retention(p, {"rq": 8, "rk": 16, "rv": 24, "rg": 32})
    return y_nsa, y_ret


def kernel(x, rel_bias, ev_w_in, ev_conv_w, ev_conv_b, ev_dt_bias, ev_a_log, ev_d_skip, ev_norm_w, ev_w_out,
           od_w_in, od_cmp_pe, od_cmp_w1, od_cmp_w2, od_w_out,
           ffn_w_up, ffn_conv_w, ffn_conv_b, ffn_w_down, ln_g, ln_b):
    _, t0, _ = x.shape
    t = -(-t0 // CHUNK) * CHUNK
    h = jnp.pad(x[0], ((0, t - t0), (0, 0)))
    hb = h.astype(BF16)
    bias_c = _bias_tiles(rel_bias, 8)
    bias_w = _bias_tiles(rel_bias, NSA_WINDOW // CHUNK + 1, window=NSA_WINDOW)
    cmp_band = _cmp_band(rel_bias)
    ev_w_in_b = _reorder_cols(ev_w_in, _EVEN_ORDER, _EVEN_WIDTH)
    od_w_in_b = _reorder_cols(od_w_in, _ODD_ORDER, _ODD_WIDTH)
    ev_w_out_b, od_w_out_b = ev_w_out.astype(BF16), od_w_out.astype(BF16)
    w_down_b = ffn_w_down.astype(BF16)
    for layer in range(DEPTH):
        i = layer // 2
        if layer % 2 == 0:
            mix = _even_mixer(hb, ev_w_in_b, i, ev_conv_w[i], ev_conv_b[i], ev_dt_bias[i], ev_a_log[i],
                              ev_d_skip[i], ev_norm_w[i], bias_c)
            w_out_b = ev_w_out_b
        else:
            mix = _odd_mixer(hb, od_w_in_b, i, od_cmp_pe[i], od_cmp_w1[i], od_cmp_w2[i], rel_bias,
                             bias_c, bias_w, cmp_band)
            w_out_b = od_w_out_b
        h, hb = _matmul_res_ln(mix, w_out_b, i, h, ln_g[layer, 0], ln_b[layer, 0], _OUT_TK)
        gated = _ffn_up(hb, ffn_w_up, layer, ffn_conv_w[layer], ffn_conv_b[layer])
        h, hb = _matmul_res_ln((gated,), w_down_b, layer, h, ln_g[layer, 1], ln_b[layer, 1], _DOWN_TK)
    return h[None, :t0]
```

```python
import functools
import math

import jax
import jax.numpy as jnp
from jax import lax
from jax.experimental import pallas as pl
from jax.experimental.pallas import tpu as pltpu

F32 = jnp.float32
BF16 = jnp.bfloat16

D_MODEL = 2048
DEPTH = 4
HEAD_DIM = 128
ATTN_HEADS = 8
REL_BUCKETS = 32
REL_MAX_DISTANCE = 2048
SSD_HEADS = 16
SSD_HEAD_DIM = 64
SSD_INNER = 1024
SSD_GROUPS = 2
SSD_STATE = 128
SSD_CONV = 4
MOBA_BLOCK = 256
MOBA_TOPK = 3
NSA_KV_HEADS = 2
NSA_GROUP = 4
NSA_CMP_BLOCK = 32
NSA_CMP_STRIDE = 16
NSA_SEL_BLOCK = 64
NSA_SEL_TOPN = 16
NSA_WINDOW = 512
RET_HEADS = 8
ROPE_BASE = 10000.0
FFN_DIM = 5632
FFN_CONV = 3
LN_EPS = 1e-5
NEG = -1e30
BIG = 3e38
LOG2E = math.log2(math.e)
_QK_SCALE = HEAD_DIM ** -0.5 * LOG2E
DEEPNORM_ALPHA = (2 * DEPTH) ** 0.25

CHUNK = 256
LANES = 128
VMEM_LIMIT = 56 * 1024 * 1024


def _params(*sem):
    return pltpu.CompilerParams(dimension_semantics=sem, vmem_limit_bytes=VMEM_LIMIT)


def _silu(x):
    return x * (1.0 / (1.0 + jnp.exp(-x)))


def _dot(a, b):
    return jnp.dot(a, b, preferred_element_type=F32)


def _dot_nt(a, b):
    return lax.dot_general(a, b, (((1,), (1,)), ((), ())), preferred_element_type=F32)


def _dot_tn(a, b):
    return lax.dot_general(a, b, (((0,), (0,)), ((), ())), preferred_element_type=F32)


def _split_bf16(a):
    hi = a.astype(BF16)
    return hi, (a - hi.astype(F32)).astype(BF16)


def _dot_nt_split(a, b):
    ah, al = _split_bf16(a)
    bh, bl = _split_bf16(b)
    return _dot_nt(ah, bh) + (_dot_nt(ah, bl) + _dot_nt(al, bh))


def _split3_bf16(a):
    hi = a.astype(BF16)
    mid, lo = _split_bf16(a - hi.astype(F32))
    return hi, mid, lo


def _dot_f32_by_01(a, b01):
    return sum(_dot(t, b01) for t in _split3_bf16(a))


def _dot_01_by_f32(a01, b):
    return sum(_dot(a01, t) for t in _split3_bf16(b))


def _shift_rows(a, carry, k):
    rolled = pltpu.roll(a, k, 0)
    head = jnp.where(lax.broadcasted_iota(jnp.int32, carry.shape, 0) < k,
                     pltpu.roll(carry, k, 0), rolled[:8])
    return jnp.concatenate([head, rolled[8:]], axis=0)


def _mm_kernel(x_ref, w_ref, o_ref):
    o_ref[...] = _dot(x_ref[...], w_ref[...]).astype(o_ref.dtype)


def _matmul(x, w_all, layer, tn, out_dtype=F32):
    m, k = x.shape
    n = w_all.shape[2]
    tm = min(1024, m)
    return pl.pallas_call(
        _mm_kernel,
        out_shape=jax.ShapeDtypeStruct((m, n), out_dtype),
        grid=(m // tm, n // tn),
        in_specs=[pl.BlockSpec((tm, k), lambda i, j: (i, 0)),
                  pl.BlockSpec((None, k, tn), lambda i, j: (layer, 0, j))],
        out_specs=pl.BlockSpec((tm, tn), lambda i, j: (i, j)),
        compiler_params=_params("parallel", "arbitrary"),
        name="proj_matmul",
    )(x, w_all)


def _mm_ln_kernel(*refs, steps):
    a_refs = refs[:len(steps)]
    w_ref, h_ref, g_ref, b_ref, o_ref, ob_ref, acc_ref = refs[len(steps):]
    k = pl.program_id(1)

    @pl.when(k == 0)
    def _():
        acc_ref[...] = _dot(a_refs[0][...], w_ref[...])

    first = 0
    for a_ref, n in zip(a_refs, steps):
        @pl.when((k >= max(first, 1)) & (k < first + n))
        def _(a_ref=a_ref):
            acc_ref[...] += _dot(a_ref[...], w_ref[...])
        first += n

    @pl.when(k == pl.num_programs(1) - 1)
    def _():
        y = DEEPNORM_ALPHA * h_ref[...] + acc_ref[...]
        mu = jnp.mean(y, axis=-1, keepdims=True)
        yc = y - mu
        var = jnp.mean(yc * yc, axis=-1, keepdims=True)
        out = yc * lax.rsqrt(var + LN_EPS) * g_ref[...] + b_ref[...]
        o_ref[...] = out
        ob_ref[...] = out.astype(BF16)


def _matmul_res_ln(parts, w_all, layer, h, g, b, tk):
    m = h.shape[0]
    n = w_all.shape[2]
    tm = min(512, m)
    steps = tuple(a.shape[1] // tk for a in parts)
    starts = tuple(sum(steps[:p]) for p in range(len(parts)))

    def part_spec(p):
        return pl.BlockSpec((tm, tk), lambda i, j: (i, jnp.clip(j - starts[p], 0, steps[p] - 1)))

    return pl.pallas_call(
        functools.partial(_mm_ln_kernel, steps=steps),
        out_shape=(jax.ShapeDtypeStruct((m, n), F32), jax.ShapeDtypeStruct((m, n), BF16)),
        grid=(m // tm, sum(steps)),
        in_specs=[part_spec(p) for p in range(len(parts))] + [
            pl.BlockSpec((None, tk, n), lambda i, j: (layer, j, 0)),
            pl.BlockSpec((tm, n), lambda i, j: (i, 0)),
            pl.BlockSpec((1, n), lambda i, j: (0, 0)),
            pl.BlockSpec((1, n), lambda i, j: (0, 0))],
        out_specs=(pl.BlockSpec((tm, n), lambda i, j: (i, 0)),
                   pl.BlockSpec((tm, n), lambda i, j: (i, 0))),
        scratch_shapes=[pltpu.VMEM((tm, n), F32)],
        compiler_params=_params("parallel", "arbitrary"),
        name="matmul_res_ln",
    )(*parts, w_all, h, g.reshape(1, n), b.reshape(1, n))


def _ffn_up_kernel(x_ref, wa_ref, wu_ref, cw_ref, cb_ref, o_ref, carry_ref, wab_ref, wub_ref):
    @pl.when(pl.program_id(1) == 0)
    def _():
        carry_ref[...] = jnp.zeros_like(carry_ref)
        wab_ref[...] = wa_ref[...].astype(BF16)
        wub_ref[...] = wu_ref[...].astype(BF16)

    x = x_ref[...]
    a = _dot(x, wab_ref[...])
    u = _dot(x, wub_ref[...])
    carry = carry_ref[...]
    cw = cw_ref[...]
    conv = (cw[0:1] * _shift_rows(a, carry, 2) + cw[1:2] * _shift_rows(a, carry, 1)
            + cw[2:3] * a + cb_ref[...])
    carry_ref[...] = a[a.shape[0] - 8:]
    o_ref[...] = (_silu(conv) * u).astype(BF16)


def _ffn_up(x, w_up_all, layer, conv_w, conv_b):
    m, k = x.shape
    tn = 512
    tm = min(1024, m)
    nj = FFN_DIM // tn
    return pl.pallas_call(
        _ffn_up_kernel,
        out_shape=jax.ShapeDtypeStruct((m, FFN_DIM), BF16),
        grid=(nj, m // tm),
        in_specs=[pl.BlockSpec((tm, k), lambda j, i: (i, 0)),
                  pl.BlockSpec((None, k, tn), lambda j, i: (layer, 0, j)),
                  pl.BlockSpec((None, k, tn), lambda j, i: (layer, 0, j + nj)),
                  pl.BlockSpec((FFN_CONV, tn), lambda j, i: (0, j)),
                  pl.BlockSpec((1, tn), lambda j, i: (0, j))],
        out_specs=pl.BlockSpec((tm, tn), lambda j, i: (i, j)),
        scratch_shapes=[pltpu.VMEM((8, tn), F32), pltpu.VMEM((k, tn), BF16), pltpu.VMEM((k, tn), BF16)],
        compiler_params=_params("parallel", "arbitrary"),
        name="ffn_up_conv",
    )(x, w_up_all, w_up_all, conv_w, conv_b.reshape(1, FFN_DIM))


def _ssd_kernel(z_ref, x_ref, bc_ref, dt_ref, dtt_ref, cwx_ref, cbx_ref, cwb_ref, cbb_ref,
                dtb_ref, alog_ref, dtbc_ref, alogc_ref, dskip_ref, nw_ref, expand_ref,
                o_ref, cx_ref, cbc_ref, state_ref):
    L = CHUNK
    hpg = SSD_HEADS // SSD_GROUPS
    gw = hpg * SSD_HEAD_DIM

    @pl.when(pl.program_id(0) == 0)
    def _():
        cx_ref[...] = jnp.zeros_like(cx_ref)
        cbc_ref[...] = jnp.zeros_like(cbc_ref)
        state_ref[...] = jnp.zeros_like(state_ref)

    def conv(raw, carry_ref, w_ref, b_ref):
        carry = carry_ref[...]
        w = w_ref[...]
        y = (w[0:1] * _shift_rows(raw, carry, 3) + w[1:2] * _shift_rows(raw, carry, 2)
             + w[2:3] * _shift_rows(raw, carry, 1) + w[3:4] * raw + b_ref[...])
        carry_ref[...] = raw[L - 8:]
        return _silu(y)

    xs = conv(x_ref[...], cx_ref, cwx_ref, cbx_ref)
    bcs = conv(bc_ref[...], cbc_ref, cwb_ref, cbb_ref)

    def softplus(v):
        return jnp.maximum(v, 0.0) + jnp.log(1.0 + jnp.exp(-jnp.abs(v)))

    dt = softplus(dt_ref[...] + dtb_ref[...])
    da = dt * (-jnp.exp(alog_ref[...]))
    row = lax.broadcasted_iota(jnp.int32, (L, L), 0)
    col = lax.broadcasted_iota(jnp.int32, (L, L), 1)
    causal = row >= col
    a_cum = _dot_01_by_f32(causal.astype(BF16), da)
    dt_r = softplus(dtt_ref[...] + dtbc_ref[...])
    da_r = dt_r * (-jnp.exp(alogc_ref[...]))
    a_cum_r = _dot_f32_by_01(da_r, (row <= col).astype(BF16))

    expand = expand_ref[...]

    def widen(v):
        return _dot_f32_by_01(v, expand)

    a_last = a_cum[L - 1:L]
    dt_w = widen(dt)
    decay_in_w = widen(jnp.exp(a_cum))
    decay_end_w = widen(jnp.exp(a_last - a_cum))
    chunk_decay_w = widen(jnp.broadcast_to(jnp.exp(a_last), (8, LANES)))[0:1]
    xdt = xs * dt_w
    xdt_b = xdt.astype(BF16)
    xdt_end_b = (xdt * decay_end_w).astype(BF16)
    lane_lo = lax.broadcasted_iota(jnp.int32, (L, LANES), 1) < SSD_HEAD_DIM

    ys = []
    for g in range(SSD_GROUPS):
        b_g = bcs[:, g * SSD_STATE:(g + 1) * SSD_STATE].astype(BF16)
        c_g = bcs[:, (SSD_GROUPS + g) * SSD_STATE:(SSD_GROUPS + g + 1) * SSD_STATE].astype(BF16)
        cb = _dot_nt(c_g, b_g)
        pieces = []
        for pair in range(hpg // 2):
            outs = []
            for k in range(2):
                h = g * hpg + pair * 2 + k
                diff = a_cum[:, h:h + 1] - a_cum_r[h:h + 1, :]
                w = (cb * jnp.exp(jnp.where(causal, diff, NEG))).astype(BF16)
                lo = (g * hpg + pair * 2) * SSD_HEAD_DIM
                outs.append(_dot(w, xdt_b[:, lo:lo + LANES]))
            pieces.append(jnp.where(lane_lo, outs[0], outs[1]))
        y_diag = jnp.concatenate(pieces, axis=1)
        state = state_ref[g]
        sl = slice(g * gw, (g + 1) * gw)
        y_off = _dot(c_g, state.astype(BF16)) * decay_in_w[:, sl]
        state_ref[g] = state * chunk_decay_w[:, sl] + _dot_tn(b_g, xdt_end_b[:, sl])
        ys.append(y_diag + y_off)
    y = jnp.concatenate(ys, axis=1) + xs * dskip_ref[...]
    yg = y * _silu(z_ref[...])
    half = SSD_INNER // SSD_GROUPS
    outs = []
    for g in range(SSD_GROUPS):
        part = yg[:, g * half:(g + 1) * half]
        outs.append(part * lax.rsqrt(jnp.mean(part * part, axis=-1, keepdims=True) + LN_EPS))
    o_ref[...] = (jnp.concatenate(outs, axis=1) * nw_ref[...]).astype(BF16)


def _ssd(p, dt_t, conv_w, conv_b, dt_bias, a_log, d_skip, norm_w, col):
    t = p.shape[0]
    L = CHUNK
    pad16 = lambda v: jnp.pad(v.astype(F32), (0, LANES - SSD_HEADS)).reshape(1, LANES)
    expand = (jnp.arange(LANES)[:, None] == (jnp.arange(SSD_INNER) // SSD_HEAD_DIM)[None, :]).astype(BF16)
    full = lambda shape: pl.BlockSpec(shape, lambda c: (0,) * len(shape))
    return pl.pallas_call(
        _ssd_kernel,
        out_shape=jax.ShapeDtypeStruct((t, SSD_INNER), BF16),
        grid=(t // L,),
        in_specs=[pl.BlockSpec((L, 1024), lambda c: (c, col["z"])),
                  pl.BlockSpec((L, 1024), lambda c: (c, col["x"])),
                  pl.BlockSpec((L, 512), lambda c: (c, col["bc"])),
                  pl.BlockSpec((L, LANES), lambda c: (c, col["dt"])),
                  pl.BlockSpec((SSD_HEADS, L), lambda c: (0, c)),
                  full((SSD_CONV, 1024)), full((1, 1024)), full((SSD_CONV, 512)), full((1, 512)),
                  full((1, LANES)), full((1, LANES)), full((SSD_HEADS, 1)), full((SSD_HEADS, 1)),
                  full((1, 1024)), full((1, 1024)), full((LANES, 1024))],
        out_specs=pl.BlockSpec((L, SSD_INNER), lambda c: (c, 0)),
        scratch_shapes=[pltpu.VMEM((8, 1024), F32), pltpu.VMEM((8, 512), F32),
                        pltpu.VMEM((SSD_GROUPS, SSD_STATE, 512), F32)],
        compiler_params=_params("arbitrary"),
        name="ssd_mixer",
    )(p, p, p, p, dt_t,
      conv_w[:, :SSD_INNER], conv_b[:SSD_INNER].reshape(1, -1),
      conv_w[:, SSD_INNER:], conv_b[SSD_INNER:].reshape(1, -1),
      pad16(dt_bias), pad16(a_log), dt_bias.reshape(-1, 1), a_log.reshape(-1, 1),
      jnp.repeat(d_skip, SSD_HEAD_DIM).reshape(1, -1), norm_w.reshape(1, -1), expand)


_RET_GROUP = 4


def _ret_kernel(logg_ref, q_ref, k_ref, v_ref, g_ref, cos_ref, sin_ref, o_ref,
                state_ref, decay_ref, qdec_ref, kdec_ref, cdec_ref):
    C = CHUNK
    hg = pl.program_id(0)
    head_cols = lambda g: slice(g * HEAD_DIM, (g + 1) * HEAD_DIM)

    @pl.when(pl.program_id(1) == 0)
    def _():
        state_ref[...] = jnp.zeros_like(state_ref)
        rel = lax.broadcasted_iota(jnp.int32, (C, C), 0) - lax.broadcasted_iota(jnp.int32, (C, C), 1)
        pos = lax.broadcasted_iota(jnp.int32, (C, 1), 0).astype(F32)
        for g in range(_RET_GROUP):
            lg = logg_ref[hg * _RET_GROUP + g]
            decay_ref[g] = jnp.where(rel >= 0, jnp.exp(lg * jnp.maximum(rel, 0).astype(F32)), 0.0)
            qdec_ref[g] = jnp.exp(lg * (pos + 1.0))
            kdec_ref[g] = jnp.exp(lg * (C - 1.0 - pos))
            cdec_ref[g] = jnp.exp(lg * jnp.full((1, HEAD_DIM), float(C), F32))

    cos = cos_ref[...]
    sin = sin_ref[...]
    even_lane = lax.broadcasted_iota(jnp.int32, (C, HEAD_DIM), 1) % 2 == 0

    def rot(x):
        swapped = jnp.where(even_lane, pltpu.roll(x, HEAD_DIM - 1, 1), pltpu.roll(x, 1, 1))
        return x * cos + swapped * sin

    qs = [rot(q_ref[:, head_cols(g)]) for g in range(_RET_GROUP)]
    ks = [rot(k_ref[:, head_cols(g)]) * (HEAD_DIM ** -0.5) for g in range(_RET_GROUP)]
    vs = [v_ref[:, head_cols(g)].astype(BF16) for g in range(_RET_GROUP)]
    inners = [_dot_nt(qs[g].astype(BF16), ks[g].astype(BF16)) for g in range(_RET_GROUP)]
    for g in range(_RET_GROUP):
        state = state_ref[g]
        y = _dot((inners[g] * decay_ref[g]).astype(BF16), vs[g])
        y = y + _dot((qs[g] * qdec_ref[g]).astype(BF16), state.astype(BF16))
        state_ref[g] = state * cdec_ref[g] + _dot_tn((ks[g] * kdec_ref[g]).astype(BF16), vs[g])
        mu = jnp.mean(y, axis=-1, keepdims=True)
        yc = y - mu
        var = jnp.mean(yc * yc, axis=-1, keepdims=True)
        o_ref[:, head_cols(g)] = (_silu(g_ref[:, head_cols(g)]) * (yc * lax.rsqrt(var + LN_EPS))).astype(BF16)


def _retention(p, col):
    t = p.shape[0]
    C = CHUNK
    G = _RET_GROUP
    inv = 1.0 / (ROPE_BASE ** (jnp.arange(0, HEAD_DIM, 2, dtype=F32) / HEAD_DIM))
    ang = jnp.arange(t, dtype=F32)[:, None] * inv[None, :]
    cos = jnp.repeat(jnp.cos(ang), 2, axis=1)
    sin = jnp.stack([-jnp.sin(ang), jnp.sin(ang)], axis=-1).reshape(t, HEAD_DIM)
    log_g = jnp.log(1.0 - 2.0 ** (-5.0 - jnp.arange(RET_HEADS, dtype=F32)))
    blk = lambda name: pl.BlockSpec((C, G * HEAD_DIM), lambda h, c, lg: (c, col[name] // G + h))
    tab = pl.BlockSpec((C, HEAD_DIM), lambda h, c, lg: (c, 0))
    return pl.pallas_call(
        _ret_kernel,
        out_shape=jax.ShapeDtypeStruct((t, RET_HEADS * HEAD_DIM), BF16),
        grid_spec=pltpu.PrefetchScalarGridSpec(
            num_scalar_prefetch=1,
            grid=(RET_HEADS // G, t // C),
            in_specs=[blk("rq"), blk("rk"), blk("rv"), blk("rg"), tab, tab],
            out_specs=pl.BlockSpec((C, G * HEAD_DIM), lambda h, c, lg: (c, h)),
            scratch_shapes=[pltpu.VMEM((G, HEAD_DIM, HEAD_DIM), F32), pltpu.VMEM((G, C, C), F32),
                            pltpu.VMEM((G, C, 1), F32), pltpu.VMEM((G, C, 1), F32),
                            pltpu.VMEM((G, 1, HEAD_DIM), F32)]),
        compiler_params=_params("parallel", "arbitrary"),
        name="retention_mixer",
    )(log_g, p, p, p, p, cos, sin)


def _rel_bucket(dist):
    n = jnp.maximum(dist, 0)
    max_exact = REL_BUCKETS // 2
    nf = jnp.maximum(n, max_exact).astype(F32)
    large = max_exact + (jnp.log(nf / max_exact) / math.log(REL_MAX_DISTANCE / max_exact)
                         * (REL_BUCKETS - max_exact)).astype(jnp.int32)
    large = jnp.minimum(large, REL_BUCKETS - 1)
    return jnp.where(n < max_exact, n, large)


def _bias_lookup(tab_ref, bucket, h):
    acc = jnp.full(bucket.shape, tab_ref[h], F32)
    for b in range(1, REL_BUCKETS):
        acc = jnp.where(bucket == b, tab_ref[b * ATTN_HEADS + h], acc)
    return acc


def _bias_tile_kernel(tab_ref, o_ref, *, window):
    h = pl.program_id(0)
    o = pl.program_id(1)
    row = lax.broadcasted_iota(jnp.int32, (CHUNK, CHUNK), 0)
    col = lax.broadcasted_iota(jnp.int32, (CHUNK, CHUNK), 1)
    d = o * CHUNK + col - row
    valid = d >= 0
    if window is not None:
        valid = valid & (d < window)
    o_ref[0, 0] = jnp.where(valid, _bias_lookup(tab_ref, _rel_bucket(d), h) * LOG2E, NEG)


def _bias_tiles(rel_bias, n_off, window=None):
    return pl.pallas_call(
        functools.partial(_bias_tile_kernel, window=window),
        out_shape=jax.ShapeDtypeStruct((ATTN_HEADS, n_off, CHUNK, CHUNK), F32),
        grid_spec=pltpu.PrefetchScalarGridSpec(
            num_scalar_prefetch=1, grid=(ATTN_HEADS, n_off), in_specs=[],
            out_specs=pl.BlockSpec((1, 1, CHUNK, CHUNK), lambda h, o, tab: (h, o, 0, 0))),
        compiler_params=_params("parallel", "parallel"),
        name="rel_bias_tiles",
    )(rel_bias.reshape(-1))


_CMP_BAND_PERIOD = LANES * NSA_CMP_STRIDE // CHUNK


def _cmp_band_kernel(tab_ref, o_ref):
    h = pl.program_id(0)
    b = pl.program_id(1)
    row = lax.broadcasted_iota(jnp.int32, (CHUNK, 2 * LANES), 0)
    col = lax.broadcasted_iota(jnp.int32, (CHUNK, 2 * LANES), 1)
    d = b * CHUNK + row - NSA_CMP_STRIDE * (col - LANES) - (NSA_CMP_BLOCK - 1)
    o_ref[0, 0] = jnp.where(d >= 0, _bias_lookup(tab_ref, _rel_bucket(d), h), NEG)


def _cmp_band(rel_bias):
    return pl.pallas_call(
        _cmp_band_kernel,
        out_shape=jax.ShapeDtypeStruct((ATTN_HEADS, _CMP_BAND_PERIOD, CHUNK, 2 * LANES), F32),
        grid_spec=pltpu.PrefetchScalarGridSpec(
            num_scalar_prefetch=1, grid=(ATTN_HEADS, _CMP_BAND_PERIOD), in_specs=[],
            out_specs=pl.BlockSpec((1, 1, CHUNK, 2 * LANES), lambda h, b, tab: (h, b, 0, 0))),
        compiler_params=_params("parallel", "parallel"),
        name="cmp_bias_band",
    )(rel_bias.reshape(-1))


def _kprep_kernel(k_ref, v_ref, ka_ref, vt_ref, *rest, heads, block, with_mean):
    i = pl.program_id(0)
    k = k_ref[...]
    v = v_ref[...]
    if block is not None:
        row = lax.broadcasted_iota(jnp.int32, (CHUNK, LANES), 0)
        lane = lax.broadcasted_iota(jnp.int32, (CHUNK, LANES), 1)
        onehot = (lane == (i * CHUNK + row) // block).astype(BF16)
    for h in range(heads):
        sl = slice(h * HEAD_DIM, (h + 1) * HEAD_DIM)
        kh = k[:, sl].astype(BF16)
        ka_ref[h, 0] = kh if block is None else jnp.concatenate([kh, onehot], axis=1)
        vt_ref[h, 0] = v[:, sl].T.astype(BF16)
    if with_mean:
        km_ref = rest[0]

        @pl.when(i == 0)
        def _():
            km_ref[...] = jnp.zeros_like(km_ref)

        km_ref[pl.ds(i, 1), :] = jnp.mean(k, axis=0, keepdims=True)


def _kprep(p, kcol, vcol, heads, block, with_mean=False):
    t = p.shape[0]
    nt = t // CHUNK
    w = heads * HEAD_DIM
    kw = HEAD_DIM if block is None else 2 * HEAD_DIM
    out_shape = [jax.ShapeDtypeStruct((heads, nt, CHUNK, kw), BF16),
                 jax.ShapeDtypeStruct((heads, nt, HEAD_DIM, CHUNK), BF16)]
    out_specs = [pl.BlockSpec((heads, 1, CHUNK, kw), lambda i: (0, i, 0, 0)),
                 pl.BlockSpec((heads, 1, HEAD_DIM, CHUNK), lambda i: (0, i, 0, 0))]
    if with_mean:
        out_shape.append(jax.ShapeDtypeStruct((LANES, w), F32))
        out_specs.append(pl.BlockSpec((LANES, w), lambda i: (0, 0)))
    return pl.pallas_call(
        functools.partial(_kprep_kernel, heads=heads, block=block, with_mean=with_mean),
        out_shape=tuple(out_shape),
        grid=(t // CHUNK,),
        in_specs=[pl.BlockSpec((CHUNK, w), lambda i: (i, kcol)),
                  pl.BlockSpec((CHUNK, w), lambda i: (i, vcol))],
        out_specs=tuple(out_specs),
        compiler_params=_params("arbitrary"),
        name="key_prep",
    )(p, p)


def _topk_mask_t(score_t, k):
    row = lax.broadcasted_iota(jnp.int32, score_t.shape, 0)

    def body(_, carry):
        g, sel = carry
        m = jnp.max(g, axis=0, keepdims=True)
        idx = jnp.min(jnp.where(g == m, row, score_t.shape[0]), axis=0, keepdims=True)
        hit = row == idx
        sel = jnp.where(hit & (m > -0.5 * BIG), 1.0, sel)
        return jnp.where(hit, -BIG, g), sel

    _, sel = lax.fori_loop(0, k, body, (score_t, jnp.zeros(score_t.shape, F32)))
    return sel > 0.5


def _moba_select_kernel(q_ref, km_ref, qa_ref):
    i = pl.program_id(0)
    q = q_ref[...]
    km = km_ref[...]
    block = lax.broadcasted_iota(jnp.int32, (LANES, CHUNK), 0)
    head_cols = lambda h: slice(h * HEAD_DIM, (h + 1) * HEAD_DIM)
    gates = [jnp.where(block < i, _dot_nt_split(km[:, head_cols(h)], q[:, head_cols(h)]), -BIG)
             for h in range(ATTN_HEADS)]
    sel = _topk_mask_t(jnp.concatenate(gates, axis=1), MOBA_TOPK)
    own = block == i
    for h in range(ATTN_HEADS):
        pen_t = jnp.where(sel[:, h * CHUNK:(h + 1) * CHUNK] | own, 0.0, NEG)
        q_t = (q[:, head_cols(h)] * _QK_SCALE).T
        qa_ref[h, 0] = jnp.concatenate([q_t, pen_t], axis=0).astype(BF16)


def _moba_select(p, qcol, kmean):
    t = p.shape[0]
    w = ATTN_HEADS * HEAD_DIM
    return pl.pallas_call(
        _moba_select_kernel,
        out_shape=jax.ShapeDtypeStruct((ATTN_HEADS, t // CHUNK, 2 * HEAD_DIM, CHUNK), BF16),
        grid=(t // CHUNK,),
        in_specs=[pl.BlockSpec((CHUNK, w), lambda i: (i, qcol)),
                  pl.BlockSpec((LANES, w), lambda i: (0, 0))],
        out_specs=pl.BlockSpec((ATTN_HEADS, 1, 2 * HEAD_DIM, CHUNK), lambda i: (0, i, 0, 0)),
        compiler_params=_params("parallel"),
        name="moba_select",
    )(p, kmean)


def _flash_kernel(q_ref, k_ref, vt_ref, bias_ref, *rest, group, shared_kv, span, combine):
    if combine:
        ocmp_ref, osel_ref, gate_ref, o_ref = rest[:4]
    else:
        o_ref = rest[0]
    m_ref, l_ref, acc_ref, alpha_ref, pt_ref, s_even_ref, s_odd_ref = rest[-7:]
    hg = pl.program_id(0)
    i = pl.program_id(1)
    n_off = bias_ref.shape[1]
    nt = k_ref.shape[-3]
    head_cols = lambda g: slice(g * HEAD_DIM, (g + 1) * HEAD_DIM)
    if len(q_ref.shape) == 3:
        qts = [q_ref[g] for g in range(group)]
    else:
        qts = [(q_ref[:, head_cols(g)] * _QK_SCALE).T.astype(BF16) for g in range(group)]
    m_ref[...] = jnp.full_like(m_ref, -BIG)
    l_ref[...] = jnp.zeros_like(l_ref)
    acc_ref[...] = jnp.zeros_like(acc_ref)
    k_tile = lambda g, j: k_ref[j] if shared_kv else k_ref[g, j]
    vt_tile = lambda g, j: vt_ref[j] if shared_kv else vt_ref[g, j]
    alpha_ref[...] = jnp.ones_like(alpha_ref)
    pt_ref[...] = jnp.zeros_like(pt_ref)

    def scores(j, s_ref):
        for g in range(group):
            s_ref[g] = _dot(k_tile(g, j), qts[g])

    def value_update(j):
        for g in range(group):
            acc_ref[g] = alpha_ref[g] * acc_ref[g] + _dot(vt_tile(g, j), pt_ref[g])

    def step(j, s_ref, s_next_ref, const_bias):
        scores(jnp.minimum(j + 1, nt - 1), s_next_ref)
        value_update(jnp.maximum(j - 1, 0))
        for g in range(group):
            m_prev = m_ref[g]
            if const_bias:
                far = bias_ref[g, n_off - 1, 0:1, :]
                st = s_ref[g]
                m_new = jnp.maximum(m_prev, jnp.max(st, axis=0, keepdims=True) + far)
                pt = jnp.exp2(st - (m_new - far))
            else:
                st = s_ref[g] + bias_ref[g, i - j]
                m_new = jnp.maximum(m_prev, jnp.max(st, axis=0, keepdims=True))
                pt = jnp.exp2(st - m_new)
            alpha = jnp.exp2(m_prev - m_new)
            l_ref[g] = alpha * l_ref[g] + jnp.sum(pt, axis=0, keepdims=True)
            m_ref[g] = m_new
            alpha_ref[g] = alpha
            pt_ref[g] = pt.astype(BF16)

    def pairs(start, count, const_bias):
        def body(t, carry):
            j = start + 2 * t
            step(j, s_even_ref, s_odd_ref, const_bias)
            step(j + 1, s_odd_ref, s_even_ref, const_bias)
            return carry
        lax.fori_loop(0, count, body, 0)

    if span is None:
        n_far = jnp.maximum(i - (n_off - 2), 0) // 2 * 2
        j_lo = 0
    else:
        n_far = 0
        j_lo = jnp.maximum(i - (span - 1), 0)
    scores(j_lo, s_even_ref)
    if span is None:
        pairs(j_lo, n_far // 2, True)
    near_lo = j_lo + n_far
    n_near = i + 1 - near_lo
    pairs(near_lo, n_near // 2, False)

    @pl.when(n_near % 2 == 1)
    def _():
        step(i, s_even_ref, s_odd_ref, False)

    value_update(i)
    if combine:
        sig = 1.0 / (1.0 + jnp.exp(-gate_ref[...]))
        lane = lax.broadcasted_iota(jnp.int32, sig.shape, 1)
    for g in range(group):
        out = (acc_ref[g] * (1.0 / l_ref[g])).T
        if combine:
            h = hg * group + g
            pick = lambda c: jnp.sum(jnp.where(lane == 3 * h + c, sig, 0.0), axis=1, keepdims=True)
            out = pick(0) * ocmp_ref[:, head_cols(g)] + pick(1) * osel_ref[:, head_cols(g)] + pick(2) * out
        o_ref[:, head_cols(g)] = out.astype(o_ref.dtype)


def _flash(q, k, vt, bias, *, heads, group, shared_kv, out_dtype, span=None, combine=None):
    n_off = bias.shape[1]
    nt = k.shape[1]
    gw = group * HEAD_DIM
    if isinstance(q, tuple):
        q, base = q
        q_spec = pl.BlockSpec((CHUNK, gw), lambda hg, i: (i, base // group + hg))
    else:
        q_spec = pl.BlockSpec((group, None) + q.shape[2:], lambda hg, i: (hg, i, 0, 0))
    kv_block = (None,) if shared_kv else (group,)
    resident = lambda shape: pl.BlockSpec(shape, lambda hg, i: (hg, 0, 0, 0), pipeline_mode=pl.Buffered(1))
    in_specs = [q_spec, resident(kv_block + k.shape[1:]), resident(kv_block + vt.shape[1:]),
                resident((group, n_off, CHUNK, CHUNK))]
    args = [q, k, vt, bias]
    tile = pl.BlockSpec((CHUNK, gw), lambda hg, i: (i, hg))
    if combine is not None:
        ocmp, osel, p, gcol = combine
        in_specs += [tile, tile, pl.BlockSpec((CHUNK, LANES), lambda hg, i: (i, gcol))]
        args += [ocmp, osel, p]
    return pl.pallas_call(
        functools.partial(_flash_kernel, group=group, shared_kv=shared_kv, span=span,
                          combine=combine is not None),
        out_shape=jax.ShapeDtypeStruct((nt * CHUNK, heads * HEAD_DIM), out_dtype),
        grid=(heads // group, nt),
        in_specs=in_specs,
        out_specs=tile,
        scratch_shapes=[pltpu.VMEM((group, 1, CHUNK), F32),
                        pltpu.VMEM((group, 1, CHUNK), F32),
                        pltpu.VMEM((group, HEAD_DIM, CHUNK), F32), pltpu.VMEM((group, 1, CHUNK), F32),
                        pltpu.VMEM((group, CHUNK, CHUNK), BF16),
                        pltpu.VMEM((group, CHUNK, CHUNK), F32), pltpu.VMEM((group, CHUNK, CHUNK), F32)],
        compiler_params=_params("parallel", "arbitrary"),
        name="flash_attention",
    )(*args)


def _compress_kernel(x_ref, pe_ref, w1_ref, w2_ref, o_ref):
    x = x_ref[0, 0]
    pe = pe_ref[0]
    half = NSA_CMP_STRIDE * HEAD_DIM
    a = _dot((x + pe[0:1]).astype(BF16), w1_ref[0, :half, :])
    b = _dot((x + pe[1:2]).astype(BF16), w1_ref[0, half:, :])
    hid = a + pltpu.roll(b, x.shape[0] - 1, 0)
    gelu = 0.5 * hid * (1.0 + jnp.tanh(math.sqrt(2.0 / math.pi) * (hid + 0.044715 * hid * hid * hid)))
    o_ref[0, 0] = _dot(gelu.astype(BF16), w2_ref[0])


def _compress(x2, pe, w1, w2):
    _, hkv, w, kd = x2.shape
    return pl.pallas_call(
        _compress_kernel,
        out_shape=jax.ShapeDtypeStruct((2, hkv, w, HEAD_DIM), F32),
        grid=(2, hkv),
        in_specs=[pl.BlockSpec((1, 1, w, kd), lambda a, h: (a, h, 0, 0)),
                  pl.BlockSpec((1, 2, kd), lambda a, h: (a, 0, 0)),
                  pl.BlockSpec((1, 2 * kd, HEAD_DIM), lambda a, h: (a, 0, 0)),
                  pl.BlockSpec((1, HEAD_DIM, HEAD_DIM), lambda a, h: (a, 0, 0))],
        out_specs=pl.BlockSpec((1, 1, w, HEAD_DIM), lambda a, h: (a, h, 0, 0)),
        compiler_params=_params("parallel", "parallel"),
        name="nsa_compress",
    )(x2, pe, w1, w2)


def _nsa_cmp_kernel(tab_ref, q_ref, kv_ref, band_ref, ov_ref, ocmp_ref, qa_ref, imp_ref):
    i = pl.program_id(0)
    width = kv_ref.shape[2]
    scale = HEAD_DIM ** -0.5
    q = q_ref[...]
    block = lax.broadcasted_iota(jnp.int32, (LANES, CHUNK), 0)
    cur = (i * CHUNK + lax.broadcasted_iota(jnp.int32, (LANES, CHUNK), 1)) // NSA_SEL_BLOCK
    forced = (block == 0) | (block == cur) | (block == cur - 1)
    diag = i // _CMP_BAND_PERIOD
    head_cols = lambda h: slice(h * HEAD_DIM, (h + 1) * HEAD_DIM)

    def attend(n_tiles):
        w = n_tiles * LANES
        ov_t = ov_ref[:, :w]
        for kvh in range(NSA_KV_HEADS):
            kc = kv_ref[0, kvh, :w, :].astype(BF16)
            vc = kv_ref[1, kvh, :w, :].astype(BF16)
            psum = jnp.zeros((CHUNK, w), F32)
            for g in range(NSA_GROUP):
                h = kvh * NSA_GROUP + g
                far = tab_ref[(REL_BUCKETS - 1) * ATTN_HEADS + h]
                band = band_ref[h]
                tiles = [jnp.full((CHUNK, LANES), far, F32)] * (n_tiles - 2)
                tiles += [band[:, :LANES], band[:, LANES:]][-min(n_tiles, 2):]
                s = _dot_nt(q[:, head_cols(h)].astype(BF16), kc) * scale + jnp.concatenate(tiles, axis=1)
                m = jnp.max(s, axis=1, keepdims=True)
                e = jnp.exp(s - m)
                l = jnp.sum(e, axis=1, keepdims=True)
                pr = e * jnp.where(m > 0.1 * NEG, 1.0 / l, 0.0)
                ocmp_ref[:, head_cols(h)] = _dot(pr.astype(BF16), vc)
                psum = psum + pr
            hi, lo = _split_bf16(psum)
            imp_ref[kvh] = _dot_nt(ov_t, hi) + _dot_nt(ov_t, lo)

    for n_tiles in range(1, width // LANES + 1):
        pl.when(diag == n_tiles - 1)(functools.partial(attend, n_tiles))

    scores = [jnp.where(forced, BIG, jnp.where(block <= cur, imp_ref[kvh], -BIG)) for kvh in range(NSA_KV_HEADS)]
    sel = _topk_mask_t(jnp.concatenate(scores, axis=1), NSA_SEL_TOPN)
    for kvh in range(NSA_KV_HEADS):
        pen_t = jnp.where(sel[:, kvh * CHUNK:(kvh + 1) * CHUNK], 0.0, NEG).astype(BF16)
        for g in range(NSA_GROUP):
            h = kvh * NSA_GROUP + g
            qa_ref[h, 0] = jnp.concatenate([(q[:, head_cols(h)] * _QK_SCALE).T.astype(BF16), pen_t], axis=0)


def _nsa_cmp(p, qcol, kv_cmp, band, rel_bias):
    t = p.shape[0]
    width = kv_cmp.shape[2]
    w = ATTN_HEADS * HEAD_DIM
    n = jnp.arange(width)[None, :] * NSA_CMP_STRIDE
    j = jnp.arange(LANES)[:, None] * NSA_SEL_BLOCK
    overlap = ((n < j + NSA_SEL_BLOCK) & (n + NSA_CMP_BLOCK > j)
               & (jnp.arange(width)[None, :] < width - 1)).astype(BF16)
    return pl.pallas_call(
        _nsa_cmp_kernel,
        out_shape=(jax.ShapeDtypeStruct((t, w), F32),
                   jax.ShapeDtypeStruct((ATTN_HEADS, t // CHUNK, 2 * HEAD_DIM, CHUNK), BF16)),
        grid_spec=pltpu.PrefetchScalarGridSpec(
            num_scalar_prefetch=1,
            grid=(t // CHUNK,),
            in_specs=[pl.BlockSpec((CHUNK, w), lambda i, tab: (i, qcol)),
                      pl.BlockSpec(kv_cmp.shape, lambda i, tab: (0, 0, 0, 0)),
                      pl.BlockSpec((ATTN_HEADS, None, CHUNK, 2 * LANES),
                                   lambda i, tab: (0, i % _CMP_BAND_PERIOD, 0, 0)),
                      pl.BlockSpec((LANES, width), lambda i, tab: (0, 0))],
            out_specs=(pl.BlockSpec((CHUNK, w), lambda i, tab: (i, 0)),
                       pl.BlockSpec((ATTN_HEADS, 1, 2 * HEAD_DIM, CHUNK), lambda i, tab: (0, i, 0, 0))),
            scratch_shapes=[pltpu.VMEM((NSA_KV_HEADS, LANES, CHUNK), F32)]),
        compiler_params=_params("parallel"),
        name="nsa_cmp_select",
    )(rel_bias.reshape(-1), p, kv_cmp, band, overlap)


_EVEN_ORDER = ((0, 2048), (2576, 5648), (2048, 2576))
_EVEN_WIDTH = 5760
_EVEN_TN = 1920
_OUT_TK = 1024
_DOWN_TK = 1408
_ODD_ORDER = ((0, 1024), (2584, 6680), (1024, 2584))
_ODD_WIDTH = 6912
_ODD_TN = 768


def _reorder_kernel(w_ref, o_ref, *, order):
    w = w_ref[...]
    parts = [w[:, a:b] for a, b in order]
    pad = o_ref.shape[1] - sum(b - a for a, b in order)
    parts.append(jnp.zeros((w.shape[0], pad), w.dtype))
    o_ref[...] = jnp.concatenate(parts, axis=1).astype(BF16)


def _reorder_cols(w, order, width):
    layers, d, n = w.shape
    rows = 256
    return pl.pallas_call(
        functools.partial(_reorder_kernel, order=order),
        out_shape=jax.ShapeDtypeStruct((layers, d, width), BF16),
        grid=(layers, d // rows),
        in_specs=[pl.BlockSpec((None, rows, n), lambda l, r: (l, r, 0))],
        out_specs=pl.BlockSpec((None, rows, width), lambda l, r: (l, r, 0)),
        compiler_params=_params("parallel", "parallel"),
        name="reorder_weight_cols",
    )(w)


def _even_mixer(hb, w_in_all, layer, conv_w, conv_b, dt_bias, a_log, d_skip, norm_w, bias_c):
    p = _matmul(hb, w_in_all, layer, _EVEN_TN)
    dt_t = p[:, 5632:5632 + SSD_HEADS].T
    y_ssd = _ssd(p, dt_t, conv_w, conv_b, dt_bias, a_log, d_skip, norm_w,
                 {"z": 0, "x": 1, "bc": 10, "dt": 44})
    k_aug, v_t, kmean = _kprep(p, 3, 4, ATTN_HEADS, MOBA_BLOCK, True)
    q_aug = _moba_select(p, 2, kmean)
    y_moba = _flash(q_aug, k_aug, v_t, bias_c, heads=ATTN_HEADS, group=4, shared_kv=False, out_dtype=BF16)
    return y_ssd, y_moba


def _odd_mixer(hb, w_in_all, layer, cmp_pe, cmp_w1, cmp_w2, rel_bias, bias_c, bias_w, cmp_band):
    p = _matmul(hb, w_in_all, layer, _ODD_TN)
    t = p.shape[0]
    x2 = p[:, 5120:5632].reshape(t, 2, NSA_KV_HEADS, HEAD_DIM).transpose(1, 2, 0, 3)
    x2 = x2.reshape(2, NSA_KV_HEADS, t // NSA_CMP_STRIDE, NSA_CMP_STRIDE * HEAD_DIM)
    kv_cmp = _compress(x2, cmp_pe.reshape(2, 2, NSA_CMP_STRIDE * HEAD_DIM),
                       cmp_w1.astype(BF16), cmp_w2.astype(BF16))
    o_cmp, q_aug = _nsa_cmp(p, 0, kv_cmp, cmp_band, rel_bias)
    k_aug, v_t = _kprep(p, 22, 23, NSA_KV_HEADS, NSA_SEL_BLOCK)
    o_sel = _flash(q_aug, k_aug, v_t, bias_c, heads=ATTN_HEADS, group=NSA_GROUP, shared_kv=True, out_dtype=F32)
    k_win, v_win_t = _kprep(p, 24, 25, NSA_KV_HEADS, None)
    y_nsa = _flash((p, 0), k_win, v_win_t, bias_w, heads=ATTN_HEADS, group=NSA_GROUP, shared_kv=True,
                   out_dtype=BF16, span=NSA_WINDOW // CHUNK + 1, combine=(o_cmp, o_sel, p, 52))
    y_ret = _retention(p, {"rq": 8, "rk": 16, "rv": 24, "rg": 32})
    return y_nsa, y_ret


def kernel(x, rel_bias, ev_w_in, ev_conv_w, ev_conv_b, ev_dt_bias, ev_a_log, ev_d_skip, ev_norm_w, ev_w_out,
           od_w_in, od_cmp_pe, od_cmp_w1, od_cmp_w2, od_w_out,
           ffn_w_up, ffn_conv_w, ffn_conv_b, ffn_w_down, ln_g, ln_b):
    _, t0, _ = x.shape
    t = -(-t0 // CHUNK) * CHUNK
    h = jnp.pad(x[0], ((0, t - t0), (0, 0)))
    hb = h.astype(BF16)
    bias_c = _bias_tiles(rel_bias, 8)
    bias_w = _bias_tiles(rel_bias, NSA_WINDOW // CHUNK + 1, window=NSA_WINDOW)
    cmp_band = _cmp_band(rel_bias)
    ev_w_in_b = _reorder_cols(ev_w_in, _EVEN_ORDER, _EVEN_WIDTH)
    od_w_in_b = _reorder_cols(od_w_in, _ODD_ORDER, _ODD_WIDTH)
    ev_w_out_b, od_w_out_b = ev_w_out.astype(BF16), od_w_out.astype(BF16)
    w_down_b = ffn_w_down.astype(BF16)
    for layer in range(DEPTH):
        i = layer // 2
        if layer % 2 == 0:
            mix = _even_mixer(hb, ev_w_in_b, i, ev_conv_w[i], ev_conv_b[i], ev_dt_bias[i], ev_a_log[i],
                              ev_d_skip[i], ev_norm_w[i], bias_c)
            w_out_b = ev_w_out_b
        else:
            mix = _odd_mixer(hb, od_w_in_b, i, od_cmp_pe[i], od_cmp_w1[i], od_cmp_w2[i], rel_bias,
                             bias_c, bias_w, cmp_band)
            w_out_b = od_w_out_b
        h, hb = _matmul_res_ln(mix, w_out_b, i, h, ln_g[layer, 0], ln_b[layer, 0], _OUT_TK)
        gated = _ffn_up(hb, ffn_w_up, layer, ffn_conv_w[layer], ffn_conv_b[layer])
        h, hb = _matmul_res_ln((gated,), w_down_b, layer, h, ln_g[layer, 1], ln_b[layer, 1], _DOWN_TK)
    return h[None, :t0]
```

```python
import functools
import math

import jax
import jax.numpy as jnp
from jax import lax
from jax.experimental import pallas as pl
from jax.experimental.pallas import tpu as pltpu

F32 = jnp.float32
BF16 = jnp.bfloat16

D_MODEL = 2048
DEPTH = 4
HEAD_DIM = 128
ATTN_HEADS = 8
REL_BUCKETS = 32
REL_MAX_DISTANCE = 2048
SSD_HEADS = 16
SSD_HEAD_DIM = 64
SSD_INNER = 1024
SSD_GROUPS = 2
SSD_STATE = 128
SSD_CONV = 4
MOBA_BLOCK = 256
MOBA_TOPK = 3
NSA_KV_HEADS = 2
NSA_GROUP = 4
NSA_CMP_BLOCK = 32
NSA_CMP_STRIDE = 16
NSA_SEL_BLOCK = 64
NSA_SEL_TOPN = 16
NSA_WINDOW = 512
RET_HEADS = 8
ROPE_BASE = 10000.0
FFN_DIM = 5632
FFN_CONV = 3
LN_EPS = 1e-5
NEG = -1e30
BIG = 3e38
LOG2E = math.log2(math.e)
_QK_SCALE = HEAD_DIM ** -0.5 * LOG2E
DEEPNORM_ALPHA = (2 * DEPTH) ** 0.25

CHUNK = 256
LANES = 128
VMEM_LIMIT = 56 * 1024 * 1024


def _params(*sem):
    return pltpu.CompilerParams(dimension_semantics=sem, vmem_limit_bytes=VMEM_LIMIT)


def _silu(x):
    return x * (1.0 / (1.0 + jnp.exp(-x)))


def _dot(a, b):
    return jnp.dot(a, b, preferred_element_type=F32)


def _dot_nt(a, b):
    return lax.dot_general(a, b, (((1,), (1,)), ((), ())), preferred_element_type=F32)


def _dot_tn(a, b):
    return lax.dot_general(a, b, (((0,), (0,)), ((), ())), preferred_element_type=F32)


def _split_bf16(a):
    hi = a.astype(BF16)
    return hi, (a - hi.astype(F32)).astype(BF16)


def _dot_nt_split(a, b):
    ah, al = _split_bf16(a)
    bh, bl = _split_bf16(b)
    return _dot_nt(ah, bh) + (_dot_nt(ah, bl) + _dot_nt(al, bh))


def _split3_bf16(a):
    hi = a.astype(BF16)
    mid, lo = _split_bf16(a - hi.astype(F32))
    return hi, mid, lo


def _dot_f32_by_01(a, b01):
    return sum(_dot(t, b01) for t in _split3_bf16(a))


def _dot_01_by_f32(a01, b):
    return sum(_dot(a01, t) for t in _split3_bf16(b))


def _shift_rows(a, carry, k):
    rolled = pltpu.roll(a, k, 0)
    head = jnp.where(lax.broadcasted_iota(jnp.int32, carry.shape, 0) < k,
                     pltpu.roll(carry, k, 0), rolled[:8])
    return jnp.concatenate([head, rolled[8:]], axis=0)


def _mm_kernel(x_ref, w_ref, o_ref):
    o_ref[...] = _dot(x_ref[...], w_ref[...]).astype(o_ref.dtype)


def _matmul(x, w_all, layer, tn, out_dtype=F32):
    m, k = x.shape
    n = w_all.shape[2]
    tm = min(1024, m)
    return pl.pallas_call(
        _mm_kernel,
        out_shape=jax.ShapeDtypeStruct((m, n), out_dtype),
        grid=(m // tm, n // tn),
        in_specs=[pl.BlockSpec((tm, k), lambda i, j: (i, 0)),
                  pl.BlockSpec((None, k, tn), lambda i, j: (layer, 0, j))],
        out_specs=pl.BlockSpec((tm, tn), lambda i, j: (i, j)),
        compiler_params=_params("parallel", "arbitrary"),
        name="proj_matmul",
    )(x, w_all)


def _mm_ln_kernel(*refs, steps):
    a_refs = refs[:len(steps)]
    w_ref, h_ref, g_ref, b_ref, o_ref, ob_ref, acc_ref = refs[len(steps):]
    k = pl.program_id(1)

    @pl.when(k == 0)
    def _():
        acc_ref[...] = _dot(a_refs[0][...], w_ref[...])

    first = 0
    for a_ref, n in zip(a_refs, steps):
        @pl.when((k >= max(first, 1)) & (k < first + n))
        def _(a_ref=a_ref):
            acc_ref[...] += _dot(a_ref[...], w_ref[...])
        first += n

    @pl.when(k == pl.num_programs(1) - 1)
    def _():
        y = DEEPNORM_ALPHA * h_ref[...] + acc_ref[...]
        mu = jnp.mean(y, axis=-1, keepdims=True)
        yc = y - mu
        var = jnp.mean(yc * yc, axis=-1, keepdims=True)
        out = yc * lax.rsqrt(var + LN_EPS) * g_ref[...] + b_ref[...]
        o_ref[...] = out
        ob_ref[...] = out.astype(BF16)


def _matmul_res_ln(parts, w_all, layer, h, g, b, tk):
    m = h.shape[0]
    n = w_all.shape[2]
    tm = min(512, m)
    steps = tuple(a.shape[1] // tk for a in parts)
    starts = tuple(sum(steps[:p]) for p in range(len(parts)))

    def part_spec(p):
        return pl.BlockSpec((tm, tk), lambda i, j: (i, jnp.clip(j - starts[p], 0, steps[p] - 1)))

    return pl.pallas_call(
        functools.partial(_mm_ln_kernel, steps=steps),
        out_shape=(jax.ShapeDtypeStruct((m, n), F32), jax.ShapeDtypeStruct((m, n), BF16)),
        grid=(m // tm, sum(steps)),
        in_specs=[part_spec(p) for p in range(len(parts))] + [
            pl.BlockSpec((None, tk, n), lambda i, j: (layer, j, 0)),
            pl.BlockSpec((tm, n), lambda i, j: (i, 0)),
            pl.BlockSpec((1, n), lambda i, j: (0, 0)),
            pl.BlockSpec((1, n), lambda i, j: (0, 0))],
        out_specs=(pl.BlockSpec((tm, n), lambda i, j: (i, 0)),
                   pl.BlockSpec((tm, n), lambda i, j: (i, 0))),
        scratch_shapes=[pltpu.VMEM((tm, n), F32)],
        compiler_params=_params("parallel", "arbitrary"),
        name="matmul_res_ln",
    )(*parts, w_all, h, g.reshape(1, n), b.reshape(1, n))


def _ffn_up_kernel(x_ref, wa_ref, wu_ref, cw_ref, cb_ref, o_ref, carry_ref, wab_ref, wub_ref):
    @pl.when(pl.program_id(1) == 0)
    def _():
        carry_ref[...] = jnp.zeros_like(carry_ref)
        wab_ref[...] = wa_ref[...].astype(BF16)
        wub_ref[...] = wu_ref[...].astype(BF16)

    x = x_ref[...]
    a = _dot(x, wab_ref[...])
    u = _dot(x, wub_ref[...])
    carry = carry_ref[...]
    cw = cw_ref[...]
    conv = (cw[0:1] * _shift_rows(a, carry, 2) + cw[1:2] * _shift_rows(a, carry, 1)
            + cw[2:3] * a + cb_ref[...])
    carry_ref[...] = a[a.shape[0] - 8:]
    o_ref[...] = (_silu(conv) * u).astype(BF16)


def _ffn_up(x, w_up_all, layer, conv_w, conv_b):
    m, k = x.shape
    tn = 512
    tm = min(1024, m)
    nj = FFN_DIM // tn
    return pl.pallas_call(
        _ffn_up_kernel,
        out_shape=jax.ShapeDtypeStruct((m, FFN_DIM), BF16),
        grid=(nj, m // tm),
        in_specs=[pl.BlockSpec((tm, k), lambda j, i: (i, 0)),
                  pl.BlockSpec((None, k, tn), lambda j, i: (layer, 0, j)),
                  pl.BlockSpec((None, k, tn), lambda j, i: (layer, 0, j + nj)),
                  pl.BlockSpec((FFN_CONV, tn), lambda j, i: (0, j)),
                  pl.BlockSpec((1, tn), lambda j, i: (0, j))],
        out_specs=pl.BlockSpec((tm, tn), lambda j, i: (i, j)),
        scratch_shapes=[pltpu.VMEM((8, tn), F32), pltpu.VMEM((k, tn), BF16), pltpu.VMEM((k, tn), BF16)],
        compiler_params=_params("parallel", "arbitrary"),
        name="ffn_up_conv",
    )(x, w_up_all, w_up_all, conv_w, conv_b.reshape(1, FFN_DIM))


def _ssd_kernel(z_ref, x_ref, bc_ref, dt_ref, dtt_ref, cwx_ref, cbx_ref, cwb_ref, cbb_ref,
                dtb_ref, alog_ref, dtbc_ref, alogc_ref, dskip_ref, nw_ref, expand_ref,
                o_ref, cx_ref, cbc_ref, state_ref):
    L = CHUNK
    hpg = SSD_HEADS // SSD_GROUPS
    gw = hpg * SSD_HEAD_DIM

    @pl.when(pl.program_id(0) == 0)
    def _():
        cx_ref[...] = jnp.zeros_like(cx_ref)
        cbc_ref[...] = jnp.zeros_like(cbc_ref)
        state_ref[...] = jnp.zeros_like(state_ref)

    def conv(raw, carry_ref, w_ref, b_ref):
        carry = carry_ref[...]
        w = w_ref[...]
        y = (w[0:1] * _shift_rows(raw, carry, 3) + w[1:2] * _shift_rows(raw, carry, 2)
             + w[2:3] * _shift_rows(raw, carry, 1) + w[3:4] * raw + b_ref[...])
        carry_ref[...] = raw[L - 8:]
        return _silu(y)

    xs = conv(x_ref[...], cx_ref, cwx_ref, cbx_ref)
    bcs = conv(bc_ref[...], cbc_ref, cwb_ref, cbb_ref)

    def softplus(v):
        return jnp.maximum(v, 0.0) + jnp.log(1.0 + jnp.exp(-jnp.abs(v)))

    dt = softplus(dt_ref[...] + dtb_ref[...])
    da = dt * (-jnp.exp(alog_ref[...]))
    row = lax.broadcasted_iota(jnp.int32, (L, L), 0)
    col = lax.broadcasted_iota(jnp.int32, (L, L), 1)
    causal = row >= col
    a_cum = _dot_01_by_f32(causal.astype(BF16), da)
    dt_r = softplus(dtt_ref[...] + dtbc_ref[...])
    da_r = dt_r * (-jnp.exp(alogc_ref[...]))
    a_cum_r = _dot_f32_by_01(da_r, (row <= col).astype(BF16))

    expand = expand_ref[...]

    def widen(v):
        return _dot_f32_by_01(v, expand)

    a_last = a_cum[L - 1:L]
    dt_w = widen(dt)
    decay_in_w = widen(jnp.exp(a_cum))
    decay_end_w = widen(jnp.exp(a_last - a_cum))
    chunk_decay_w = widen(jnp.broadcast_to(jnp.exp(a_last), (8, LANES)))[0:1]
    xdt = xs * dt_w
    xdt_b = xdt.astype(BF16)
    xdt_end_b = (xdt * decay_end_w).astype(BF16)
    lane_lo = lax.broadcasted_iota(jnp.int32, (L, LANES), 1) < SSD_HEAD_DIM

    ys = []
    for g in range(SSD_GROUPS):
        b_g = bcs[:, g * SSD_STATE:(g + 1) * SSD_STATE].astype(BF16)
        c_g = bcs[:, (SSD_GROUPS + g) * SSD_STATE:(SSD_GROUPS + g + 1) * SSD_STATE].astype(BF16)
        cb = _dot_nt(c_g, b_g)
        pieces = []
        for pair in range(hpg // 2):
            outs = []
            for k in range(2):
                h = g * hpg + pair * 2 + k
                diff = a_cum[:, h:h + 1] - a_cum_r[h:h + 1, :]
                w = (cb * jnp.exp(jnp.where(causal, diff, NEG))).astype(BF16)
                lo = (g * hpg + pair * 2) * SSD_HEAD_DIM
                outs.append(_dot(w, xdt_b[:, lo:lo + LANES]))
            pieces.append(jnp.where(lane_lo, outs[0], outs[1]))
        y_diag = jnp.concatenate(pieces, axis=1)
        state = state_ref[g]
        sl = slice(g * gw, (g + 1) * gw)
        y_off = _dot(c_g, state.astype(BF16)) * decay_in_w[:, sl]
        state_ref[g] = state * chunk_decay_w[:, sl] + _dot_tn(b_g, xdt_end_b[:, sl])
        ys.append(y_diag + y_off)
    y = jnp.concatenate(ys, axis=1) + xs * dskip_ref[...]
    yg = y * _silu(z_ref[...])
    half = SSD_INNER // SSD_GROUPS
    outs = []
    for g in range(SSD_GROUPS):
        part = yg[:, g * half:(g + 1) * half]
        outs.append(part * lax.rsqrt(jnp.mean(part * part, axis=-1, keepdims=True) + LN_EPS))
    o_ref[...] = (jnp.concatenate(outs, axis=1) * nw_ref[...]).astype(BF16)


def _ssd(p, dt_t, conv_w, conv_b, dt_bias, a_log, d_skip, norm_w, col):
    t = p.shape[0]
    L = CHUNK
    pad16 = lambda v: jnp.pad(v.astype(F32), (0, LANES - SSD_HEADS)).reshape(1, LANES)
    expand = (jnp.arange(LANES)[:, None] == (jnp.arange(SSD_INNER) // SSD_HEAD_DIM)[None, :]).astype(BF16)
    full = lambda shape: pl.BlockSpec(shape, lambda c: (0,) * len(shape))
    return pl.pallas_call(
        _ssd_kernel,
        out_shape=jax.ShapeDtypeStruct((t, SSD_INNER), BF16),
        grid=(t // L,),
        in_specs=[pl.BlockSpec((L, 1024), lambda c: (c, col["z"])),
                  pl.BlockSpec((L, 1024), lambda c: (c, col["x"])),
                  pl.BlockSpec((L, 512), lambda c: (c, col["bc"])),
                  pl.BlockSpec((L, LANES), lambda c: (c, col["dt"])),
                  pl.BlockSpec((SSD_HEADS, L), lambda c: (0, c)),
                  full((SSD_CONV, 1024)), full((1, 1024)), full((SSD_CONV, 512)), full((1, 512)),
                  full((1, LANES)), full((1, LANES)), full((SSD_HEADS, 1)), full((SSD_HEADS, 1)),
                  full((1, 1024)), full((1, 1024)), full((LANES, 1024))],
        out_specs=pl.BlockSpec((L, SSD_INNER), lambda c: (c, 0)),
        scratch_shapes=[pltpu.VMEM((8, 1024), F32), pltpu.VMEM((8, 512), F32),
                        pltpu.VMEM((SSD_GROUPS, SSD_STATE, 512), F32)],
        compiler_params=_params("arbitrary"),
        name="ssd_mixer",
    )(p, p, p, p, dt_t,
      conv_w[:, :SSD_INNER], conv_b[:SSD_INNER].reshape(1, -1),
      conv_w[:, SSD_INNER:], conv_b[SSD_INNER:].reshape(1, -1),
      pad16(dt_bias), pad16(a_log), dt_bias.reshape(-1, 1), a_log.reshape(-1, 1),
      jnp.repeat(d_skip, SSD_HEAD_DIM).reshape(1, -1), norm_w.reshape(1, -1), expand)


_RET_GROUP = 4


def _ret_kernel(logg_ref, q_ref, k_ref, v_ref, g_ref, cos_ref, sin_ref, o_ref,
                state_ref, decay_ref, qdec_ref, kdec_ref, cdec_ref):
    C = CHUNK
    hg = pl.program_id(0)
    head_cols = lambda g: slice(g * HEAD_DIM, (g + 1) * HEAD_DIM)

    @pl.when(pl.program_id(1) == 0)
    def _():
        state_ref[...] = jnp.zeros_like(state_ref)
        rel = lax.broadcasted_iota(jnp.int32, (C, C), 0) - lax.broadcasted_iota(jnp.int32, (C, C), 1)
        pos = lax.broadcasted_iota(jnp.int32, (C, 1), 0).astype(F32)
        for g in range(_RET_GROUP):
            lg = logg_ref[hg * _RET_GROUP + g]
            decay_ref[g] = jnp.where(rel >= 0, jnp.exp(lg * jnp.maximum(rel, 0).astype(F32)), 0.0)
            qdec_ref[g] = jnp.exp(lg * (pos + 1.0))
            kdec_ref[g] = jnp.exp(lg * (C - 1.0 - pos))
            cdec_ref[g] = jnp.exp(lg * jnp.full((1, HEAD_DIM), float(C), F32))

    cos = cos_ref[...]
    sin = sin_ref[...]
    even_lane = lax.broadcasted_iota(jnp.int32, (C, HEAD_DIM), 1) % 2 == 0

    def rot(x):
        swapped = jnp.where(even_lane, pltpu.roll(x, HEAD_DIM - 1, 1), pltpu.roll(x, 1, 1))
        return x * cos + swapped * sin

    qs = [rot(q_ref[:, head_cols(g)]) for g in range(_RET_GROUP)]
    ks = [rot(k_ref[:, head_cols(g)]) * (HEAD_DIM ** -0.5) for g in range(_RET_GROUP)]
    vs = [v_ref[:, head_cols(g)].astype(BF16) for g in range(_RET_GROUP)]
    inners = [_dot_nt(qs[g].astype(BF16), ks[g].astype(BF16)) for g in range(_RET_GROUP)]
    for g in range(_RET_GROUP):
        state = state_ref[g]
        y = _dot((inners[g] * decay_ref[g]).astype(BF16), vs[g])
        y = y + _dot((qs[g] * qdec_ref[g]).astype(BF16), state.astype(BF16))
        state_ref[g] = state * cdec_ref[g] + _dot_tn((ks[g] * kdec_ref[g]).astype(BF16), vs[g])
        mu = jnp.mean(y, axis=-1, keepdims=True)
        yc = y - mu
        var = jnp.mean(yc * yc, axis=-1, keepdims=True)
        o_ref[:, head_cols(g)] = (_silu(g_ref[:, head_cols(g)]) * (yc * lax.rsqrt(var + LN_EPS))).astype(BF16)


def _retention(p, col):
    t = p.shape[0]
    C = CHUNK
    G = _RET_GROUP
    inv = 1.0 / (ROPE_BASE ** (jnp.arange(0, HEAD_DIM, 2, dtype=F32) / HEAD_DIM))
    ang = jnp.arange(t, dtype=F32)[:, None] * inv[None, :]
    cos = jnp.repeat(jnp.cos(ang), 2, axis=1)
    sin = jnp.stack([-jnp.sin(ang), jnp.sin(ang)], axis=-1).reshape(t, HEAD_DIM)
    log_g = jnp.log(1.0 - 2.0 ** (-5.0 - jnp.arange(RET_HEADS, dtype=F32)))
    blk = lambda name: pl.BlockSpec((C, G * HEAD_DIM), lambda h, c, lg: (c, col[name] // G + h))
    tab = pl.BlockSpec((C, HEAD_DIM), lambda h, c, lg: (c, 0))
    return pl.pallas_call(
        _ret_kernel,
        out_shape=jax.ShapeDtypeStruct((t, RET_HEADS * HEAD_DIM), BF16),
        grid_spec=pltpu.PrefetchScalarGridSpec(
            num_scalar_prefetch=1,
            grid=(RET_HEADS // G, t // C),
            in_specs=[blk("rq"), blk("rk"), blk("rv"), blk("rg"), tab, tab],
            out_specs=pl.BlockSpec((C, G * HEAD_DIM), lambda h, c, lg: (c, h)),
            scratch_shapes=[pltpu.VMEM((G, HEAD_DIM, HEAD_DIM), F32), pltpu.VMEM((G, C, C), F32),
                            pltpu.VMEM((G, C, 1), F32), pltpu.VMEM((G, C, 1), F32),
                            pltpu.VMEM((G, 1, HEAD_DIM), F32)]),
        compiler_params=_params("parallel", "arbitrary"),
        name="retention_mixer",
    )(log_g, p, p, p, p, cos, sin)


def _rel_bucket(dist):
    n = jnp.maximum(dist, 0)
    max_exact = REL_BUCKETS // 2
    nf = jnp.maximum(n, max_exact).astype(F32)
    large = max_exact + (jnp.log(nf / max_exact) / math.log(REL_MAX_DISTANCE / max_exact)
                         * (REL_BUCKETS - max_exact)).astype(jnp.int32)
    large = jnp.minimum(large, REL_BUCKETS - 1)
    return jnp.where(n < max_exact, n, large)


def _bias_lookup(tab_ref, bucket, h):
    rows, cols = bucket.shape
    lane = lax.broadcasted_iota(jnp.int32, (1, LANES), 1)
    tab_row = jnp.zeros((1, LANES), F32)
    for b in range(REL_BUCKETS):
        tab_row = jnp.where(lane == b, tab_ref[b * ATTN_HEADS + h], tab_row)
    table = jnp.broadcast_to(tab_row, (rows, LANES))
    return jnp.concatenate([jnp.take_along_axis(table, bucket[:, c:c + LANES], axis=1)
                            for c in range(0, cols, LANES)], axis=1)


def _bias_tile_kernel(tab_ref, o_ref, *, window):
    h = pl.program_id(0)
    o = pl.program_id(1)
    row = lax.broadcasted_iota(jnp.int32, (CHUNK, CHUNK), 0)
    col = lax.broadcasted_iota(jnp.int32, (CHUNK, CHUNK), 1)
    d = o * CHUNK + col - row
    valid = d >= 0
    if window is not None:
        valid = valid & (d < window)
    o_ref[0, 0] = jnp.where(valid, _bias_lookup(tab_ref, _rel_bucket(d), h) * LOG2E, NEG)


def _bias_tiles(rel_bias, n_off, window=None):
    return pl.pallas_call(
        functools.partial(_bias_tile_kernel, window=window),
        out_shape=jax.ShapeDtypeStruct((ATTN_HEADS, n_off, CHUNK, CHUNK), F32),
        grid_spec=pltpu.PrefetchScalarGridSpec(
            num_scalar_prefetch=1, grid=(ATTN_HEADS, n_off), in_specs=[],
            out_specs=pl.BlockSpec((1, 1, CHUNK, CHUNK), lambda h, o, tab: (h, o, 0, 0))),
        compiler_params=_params("parallel", "parallel"),
        name="rel_bias_tiles",
    )(rel_bias.reshape(-1))


_CMP_BAND_PERIOD = LANES * NSA_CMP_STRIDE // CHUNK


def _cmp_band_kernel(tab_ref, o_ref):
    h = pl.program_id(0)
    b = pl.program_id(1)
    row = lax.broadcasted_iota(jnp.int32, (CHUNK, 2 * LANES), 0)
    col = lax.broadcasted_iota(jnp.int32, (CHUNK, 2 * LANES), 1)
    d = b * CHUNK + row - NSA_CMP_STRIDE * (col - LANES) - (NSA_CMP_BLOCK - 1)
    o_ref[0, 0] = jnp.where(d >= 0, _bias_lookup(tab_ref, _rel_bucket(d), h), NEG)


def _cmp_band(rel_bias):
    return pl.pallas_call(
        _cmp_band_kernel,
        out_shape=jax.ShapeDtypeStruct((ATTN_HEADS, _CMP_BAND_PERIOD, CHUNK, 2 * LANES), F32),
        grid_spec=pltpu.PrefetchScalarGridSpec(
            num_scalar_prefetch=1, grid=(ATTN_HEADS, _CMP_BAND_PERIOD), in_specs=[],
            out_specs=pl.BlockSpec((1, 1, CHUNK, 2 * LANES), lambda h, b, tab: (h, b, 0, 0))),
        compiler_params=_params("parallel", "parallel"),
        name="cmp_bias_band",
    )(rel_bias.reshape(-1))


def _kprep_kernel(k_ref, v_ref, ka_ref, vt_ref, *rest, heads, block, with_mean):
    i = pl.program_id(0)
    k = k_ref[...]
    v = v_ref[...]
    if block is not None:
        row = lax.broadcasted_iota(jnp.int32, (CHUNK, LANES), 0)
        lane = lax.broadcasted_iota(jnp.int32, (CHUNK, LANES), 1)
        onehot = (lane == (i * CHUNK + row) // block).astype(BF16)
    for h in range(heads):
        sl = slice(h * HEAD_DIM, (h + 1) * HEAD_DIM)
        kh = k[:, sl].astype(BF16)
        ka_ref[h, 0] = kh if block is None else jnp.concatenate([kh, onehot], axis=1)
        vt_ref[h, 0] = v[:, sl].T.astype(BF16)
    if with_mean:
        km_ref = rest[0]

        @pl.when(i == 0)
        def _():
            km_ref[...] = jnp.zeros_like(km_ref)

        km_ref[pl.ds(i, 1), :] = jnp.mean(k, axis=0, keepdims=True)


def _kprep(p, kcol, vcol, heads, block, with_mean=False):
    t = p.shape[0]
    nt = t // CHUNK
    w = heads * HEAD_DIM
    kw = HEAD_DIM if block is None else 2 * HEAD_DIM
    out_shape = [jax.ShapeDtypeStruct((heads, nt, CHUNK, kw), BF16),
                 jax.ShapeDtypeStruct((heads, nt, HEAD_DIM, CHUNK), BF16)]
    out_specs = [pl.BlockSpec((heads, 1, CHUNK, kw), lambda i: (0, i, 0, 0)),
                 pl.BlockSpec((heads, 1, HEAD_DIM, CHUNK), lambda i: (0, i, 0, 0))]
    if with_mean:
        out_shape.append(jax.ShapeDtypeStruct((LANES, w), F32))
        out_specs.append(pl.BlockSpec((LANES, w), lambda i: (0, 0)))
    return pl.pallas_call(
        functools.partial(_kprep_kernel, heads=heads, block=block, with_mean=with_mean),
        out_shape=tuple(out_shape),
        grid=(t // CHUNK,),
        in_specs=[pl.BlockSpec((CHUNK, w), lambda i: (i, kcol)),
                  pl.BlockSpec((CHUNK, w), lambda i: (i, vcol))],
        out_specs=tuple(out_specs),
        compiler_params=_params("arbitrary"),
        name="key_prep",
    )(p, p)


def _topk_mask_t(score_t, k):
    row = lax.broadcasted_iota(jnp.int32, score_t.shape, 0)

    def body(_, carry):
        g, sel = carry
        m = jnp.max(g, axis=0, keepdims=True)
        idx = jnp.min(jnp.where(g == m, row, score_t.shape[0]), axis=0, keepdims=True)
        hit = row == idx
        sel = jnp.where(hit & (m > -0.5 * BIG), 1.0, sel)
        return jnp.where(hit, -BIG, g), sel

    _, sel = lax.fori_loop(0, k, body, (score_t, jnp.zeros(score_t.shape, F32)))
    return sel > 0.5


def _moba_select_kernel(q_ref, km_ref, qa_ref):
    i = pl.program_id(0)
    q = q_ref[...]
    km = km_ref[...]
    block = lax.broadcasted_iota(jnp.int32, (LANES, CHUNK), 0)
    head_cols = lambda h: slice(h * HEAD_DIM, (h + 1) * HEAD_DIM)
    gates = [jnp.where(block < i, _dot_nt_split(km[:, head_cols(h)], q[:, head_cols(h)]), -BIG)
             for h in range(ATTN_HEADS)]
    sel = _topk_mask_t(jnp.concatenate(gates, axis=1), MOBA_TOPK)
    own = block == i
    for h in range(ATTN_HEADS):
        pen_t = jnp.where(sel[:, h * CHUNK:(h + 1) * CHUNK] | own, 0.0, NEG)
        q_t = (q[:, head_cols(h)] * _QK_SCALE).T
        qa_ref[h, 0] = jnp.concatenate([q_t, pen_t], axis=0).astype(BF16)


def _moba_select(p, qcol, kmean):
    t = p.shape[0]
    w = ATTN_HEADS * HEAD_DIM
    return pl.pallas_call(
        _moba_select_kernel,
        out_shape=jax.ShapeDtypeStruct((ATTN_HEADS, t // CHUNK, 2 * HEAD_DIM, CHUNK), BF16),
        grid=(t // CHUNK,),
        in_specs=[pl.BlockSpec((CHUNK, w), lambda i: (i, qcol)),
                  pl.BlockSpec((LANES, w), lambda i: (0, 0))],
        out_specs=pl.BlockSpec((ATTN_HEADS, 1, 2 * HEAD_DIM, CHUNK), lambda i: (0, i, 0, 0)),
        compiler_params=_params("parallel"),
        name="moba_select",
    )(p, kmean)


def _flash_kernel(q_ref, k_ref, vt_ref, bias_ref, *rest, group, shared_kv, span, combine):
    if combine:
        ocmp_ref, osel_ref, gate_ref, o_ref = rest[:4]
    else:
        o_ref = rest[0]
    m_ref, l_ref, acc_ref, alpha_ref, pt_ref, s_even_ref, s_odd_ref = rest[-7:]
    hg = pl.program_id(0)
    i = pl.program_id(1)
    n_off = bias_ref.shape[1]
    nt = k_ref.shape[-3]
    head_cols = lambda g: slice(g * HEAD_DIM, (g + 1) * HEAD_DIM)
    if len(q_ref.shape) == 3:
        qts = [q_ref[g] for g in range(group)]
    else:
        qts = [(q_ref[:, head_cols(g)] * _QK_SCALE).T.astype(BF16) for g in range(group)]
    m_ref[...] = jnp.full_like(m_ref, -BIG)
    l_ref[...] = jnp.zeros_like(l_ref)
    acc_ref[...] = jnp.zeros_like(acc_ref)
    k_tile = lambda g, j: k_ref[j] if shared_kv else k_ref[g, j]
    vt_tile = lambda g, j: vt_ref[j] if shared_kv else vt_ref[g, j]
    alpha_ref[...] = jnp.ones_like(alpha_ref)
    pt_ref[...] = jnp.zeros_like(pt_ref)

    def scores(j, s_ref):
        for g in range(group):
            s_ref[g] = _dot(k_tile(g, j), qts[g])

    def value_update(j):
        for g in range(group):
            acc_ref[g] = alpha_ref[g] * acc_ref[g] + _dot(vt_tile(g, j), pt_ref[g])

    def step(j, s_ref, s_next_ref, const_bias):
        scores(jnp.minimum(j + 1, nt - 1), s_next_ref)
        value_update(jnp.maximum(j - 1, 0))
        for g in range(group):
            m_prev = m_ref[g]
            if const_bias:
                far = bias_ref[g, n_off - 1, 0:1, :]
                st = s_ref[g]
                m_new = jnp.maximum(m_prev, jnp.max(st, axis=0, keepdims=True) + far)
                pt = jnp.exp2(st - (m_new - far))
            else:
                st = s_ref[g] + bias_ref[g, i - j]
                m_new = jnp.maximum(m_prev, jnp.max(st, axis=0, keepdims=True))
                pt = jnp.exp2(st - m_new)
            alpha = jnp.exp2(m_prev - m_new)
            l_ref[g] = alpha * l_ref[g] + jnp.sum(pt, axis=0, keepdims=True)
            m_ref[g] = m_new
            alpha_ref[g] = alpha
            pt_ref[g] = pt.astype(BF16)

    def pairs(start, count, const_bias):
        def body(t, carry):
            j = start + 2 * t
            step(j, s_even_ref, s_odd_ref, const_bias)
            step(j + 1, s_odd_ref, s_even_ref, const_bias)
            return carry
        lax.fori_loop(0, count, body, 0)

    if span is None:
        n_far = jnp.maximum(i - (n_off - 2), 0) // 2 * 2
        j_lo = 0
    else:
        n_far = 0
        j_lo = jnp.maximum(i - (span - 1), 0)
    scores(j_lo, s_even_ref)
    if span is None:
        pairs(j_lo, n_far // 2, True)
    near_lo = j_lo + n_far
    n_near = i + 1 - near_lo
    pairs(near_lo, n_near // 2, False)

    @pl.when(n_near % 2 == 1)
    def _():
        step(i, s_even_ref, s_odd_ref, False)

    value_update(i)
    if combine:
        sig = 1.0 / (1.0 + jnp.exp(-gate_ref[...]))
        lane = lax.broadcasted_iota(jnp.int32, sig.shape, 1)
    for g in range(group):
        out = (acc_ref[g] * (1.0 / l_ref[g])).T
        if combine:
            h = hg * group + g
            pick = lambda c: jnp.sum(jnp.where(lane == 3 * h + c, sig, 0.0), axis=1, keepdims=True)
            out = pick(0) * ocmp_ref[:, head_cols(g)] + pick(1) * osel_ref[:, head_cols(g)] + pick(2) * out
        o_ref[:, head_cols(g)] = out.astype(o_ref.dtype)


def _flash(q, k, vt, bias, *, heads, group, shared_kv, out_dtype, span=None, combine=None):
    n_off = bias.shape[1]
    nt = k.shape[1]
    gw = group * HEAD_DIM
    if isinstance(q, tuple):
        q, base = q
        q_spec = pl.BlockSpec((CHUNK, gw), lambda hg, i: (i, base // group + hg))
    else:
        q_spec = pl.BlockSpec((group, None) + q.shape[2:], lambda hg, i: (hg, i, 0, 0))
    kv_block = (None,) if shared_kv else (group,)
    resident = lambda shape: pl.BlockSpec(shape, lambda hg, i: (hg, 0, 0, 0), pipeline_mode=pl.Buffered(1))
    in_specs = [q_spec, resident(kv_block + k.shape[1:]), resident(kv_block + vt.shape[1:]),
                resident((group, n_off, CHUNK, CHUNK))]
    args = [q, k, vt, bias]
    tile = pl.BlockSpec((CHUNK, gw), lambda hg, i: (i, hg))
    if combine is not None:
        ocmp, osel, p, gcol = combine
        in_specs += [tile, tile, pl.BlockSpec((CHUNK, LANES), lambda hg, i: (i, gcol))]
        args += [ocmp, osel, p]
    return pl.pallas_call(
        functools.partial(_flash_kernel, group=group, shared_kv=shared_kv, span=span,
                          combine=combine is not None),
        out_shape=jax.ShapeDtypeStruct((nt * CHUNK, heads * HEAD_DIM), out_dtype),
        grid=(heads // group, nt),
        in_specs=in_specs,
        out_specs=tile,
        scratch_shapes=[pltpu.VMEM((group, 1, CHUNK), F32),
                        pltpu.VMEM((group, 1, CHUNK), F32),
                        pltpu.VMEM((group, HEAD_DIM, CHUNK), F32), pltpu.VMEM((group, 1, CHUNK), F32),
                        pltpu.VMEM((group, CHUNK, CHUNK), BF16),
                        pltpu.VMEM((group, CHUNK, CHUNK), F32), pltpu.VMEM((group, CHUNK, CHUNK), F32)],
        compiler_params=_params("parallel", "arbitrary"),
        name="flash_attention",
    )(*args)


def _compress_kernel(x_ref, pe_ref, w1_ref, w2_ref, o_ref):
    x = x_ref[0, 0]
    pe = pe_ref[0]
    half = NSA_CMP_STRIDE * HEAD_DIM
    a = _dot((x + pe[0:1]).astype(BF16), w1_ref[0, :half, :])
    b = _dot((x + pe[1:2]).astype(BF16), w1_ref[0, half:, :])
    hid = a + pltpu.roll(b, x.shape[0] - 1, 0)
    gelu = 0.5 * hid * (1.0 + jnp.tanh(math.sqrt(2.0 / math.pi) * (hid + 0.044715 * hid * hid * hid)))
    o_ref[0, 0] = _dot(gelu.astype(BF16), w2_ref[0])


def _compress(x2, pe, w1, w2):
    _, hkv, w, kd = x2.shape
    return pl.pallas_call(
        _compress_kernel,
        out_shape=jax.ShapeDtypeStruct((2, hkv, w, HEAD_DIM), F32),
        grid=(2, hkv),
        in_specs=[pl.BlockSpec((1, 1, w, kd), lambda a, h: (a, h, 0, 0)),
                  pl.BlockSpec((1, 2, kd), lambda a, h: (a, 0, 0)),
                  pl.BlockSpec((1, 2 * kd, HEAD_DIM), lambda a, h: (a, 0, 0)),
                  pl.BlockSpec((1, HEAD_DIM, HEAD_DIM), lambda a, h: (a, 0, 0))],
        out_specs=pl.BlockSpec((1, 1, w, HEAD_DIM), lambda a, h: (a, h, 0, 0)),
        compiler_params=_params("parallel", "parallel"),
        name="nsa_compress",
    )(x2, pe, w1, w2)


def _nsa_cmp_kernel(tab_ref, q_ref, kv_ref, band_ref, ov_ref, ocmp_ref, qa_ref, imp_ref):
    i = pl.program_id(0)
    width = kv_ref.shape[2]
    scale = HEAD_DIM ** -0.5
    q = q_ref[...]
    block = lax.broadcasted_iota(jnp.int32, (LANES, CHUNK), 0)
    cur = (i * CHUNK + lax.broadcasted_iota(jnp.int32, (LANES, CHUNK), 1)) // NSA_SEL_BLOCK
    forced = (block == 0) | (block == cur) | (block == cur - 1)
    diag = i // _CMP_BAND_PERIOD
    head_cols = lambda h: slice(h * HEAD_DIM, (h + 1) * HEAD_DIM)

    def attend(n_tiles):
        w = n_tiles * LANES
        ov_t = ov_ref[:, :w]
        for kvh in range(NSA_KV_HEADS):
            kc = kv_ref[0, kvh, :w, :].astype(BF16)
            vc = kv_ref[1, kvh, :w, :].astype(BF16)
            psum = jnp.zeros((CHUNK, w), F32)
            for g in range(NSA_GROUP):
                h = kvh * NSA_GROUP + g
                far = tab_ref[(REL_BUCKETS - 1) * ATTN_HEADS + h]
                band = band_ref[h]
                tiles = [jnp.full((CHUNK, LANES), far, F32)] * (n_tiles - 2)
                tiles += [band[:, :LANES], band[:, LANES:]][-min(n_tiles, 2):]
                s = _dot_nt(q[:, head_cols(h)].astype(BF16), kc) * scale + jnp.concatenate(tiles, axis=1)
                m = jnp.max(s, axis=1, keepdims=True)
                e = jnp.exp(s - m)
                l = jnp.sum(e, axis=1, keepdims=True)
                pr = e * jnp.where(m > 0.1 * NEG, 1.0 / l, 0.0)
                ocmp_ref[:, head_cols(h)] = _dot(pr.astype(BF16), vc)
                psum = psum + pr
            hi, lo = _split_bf16(psum)
            imp_ref[kvh] = _dot_nt(ov_t, hi) + _dot_nt(ov_t, lo)

    for n_tiles in range(1, width // LANES + 1):
        pl.when(diag == n_tiles - 1)(functools.partial(attend, n_tiles))

    scores = [jnp.where(forced, BIG, jnp.where(block <= cur, imp_ref[kvh], -BIG)) for kvh in range(NSA_KV_HEADS)]
    sel = _topk_mask_t(jnp.concatenate(scores, axis=1), NSA_SEL_TOPN)
    for kvh in range(NSA_KV_HEADS):
        pen_t = jnp.where(sel[:, kvh * CHUNK:(kvh + 1) * CHUNK], 0.0, NEG).astype(BF16)
        for g in range(NSA_GROUP):
            h = kvh * NSA_GROUP + g
            qa_ref[h, 0] = jnp.concatenate([(q[:, head_cols(h)] * _QK_SCALE).T.astype(BF16), pen_t], axis=0)


def _nsa_cmp(p, qcol, kv_cmp, band, rel_bias):
    t = p.shape[0]
    width = kv_cmp.shape[2]
    w = ATTN_HEADS * HEAD_DIM
    n = jnp.arange(width)[None, :] * NSA_CMP_STRIDE
    j = jnp.arange(LANES)[:, None] * NSA_SEL_BLOCK
    overlap = ((n < j + NSA_SEL_BLOCK) & (n + NSA_CMP_BLOCK > j)
               & (jnp.arange(width)[None, :] < width - 1)).astype(BF16)
    return pl.pallas_call(
        _nsa_cmp_kernel,
        out_shape=(jax.ShapeDtypeStruct((t, w), F32),
                   jax.ShapeDtypeStruct((ATTN_HEADS, t // CHUNK, 2 * HEAD_DIM, CHUNK), BF16)),
        grid_spec=pltpu.PrefetchScalarGridSpec(
            num_scalar_prefetch=1,
            grid=(t // CHUNK,),
            in_specs=[pl.BlockSpec((CHUNK, w), lambda i, tab: (i, qcol)),
                      pl.BlockSpec(kv_cmp.shape, lambda i, tab: (0, 0, 0, 0)),
                      pl.BlockSpec((ATTN_HEADS, None, CHUNK, 2 * LANES),
                                   lambda i, tab: (0, i % _CMP_BAND_PERIOD, 0, 0)),
                      pl.BlockSpec((LANES, width), lambda i, tab: (0, 0))],
            out_specs=(pl.BlockSpec((CHUNK, w), lambda i, tab: (i, 0)),
                       pl.BlockSpec((ATTN_HEADS, 1, 2 * HEAD_DIM, CHUNK), lambda i, tab: (0, i, 0, 0))),
            scratch_shapes=[pltpu.VMEM((NSA_KV_HEADS, LANES, CHUNK), F32)]),
        compiler_params=_params("parallel"),
        name="nsa_cmp_select",
    )(rel_bias.reshape(-1), p, kv_cmp, band, overlap)


_EVEN_ORDER = ((0, 2048), (2576, 5648), (2048, 2576))
_EVEN_WIDTH = 5760
_EVEN_TN = 1920
_OUT_TK = 1024
_DOWN_TK = 1408
_ODD_ORDER = ((0, 1024), (2584, 6680), (1024, 2584))
_ODD_WIDTH = 6912
_ODD_TN = 768


def _reorder_kernel(w_ref, o_ref, *, order):
    w = w_ref[...]
    parts = [w[:, a:b] for a, b in order]
    pad = o_ref.shape[1] - sum(b - a for a, b in order)
    parts.append(jnp.zeros((w.shape[0], pad), w.dtype))
    o_ref[...] = jnp.concatenate(parts, axis=1).astype(BF16)


def _reorder_cols(w, order, width):
    layers, d, n = w.shape
    rows = 256
    return pl.pallas_call(
        functools.partial(_reorder_kernel, order=order),
        out_shape=jax.ShapeDtypeStruct((layers, d, width), BF16),
        grid=(layers, d // rows),
        in_specs=[pl.BlockSpec((None, rows, n), lambda l, r: (l, r, 0))],
        out_specs=pl.BlockSpec((None, rows, width), lambda l, r: (l, r, 0)),
        compiler_params=_params("parallel", "parallel"),
        name="reorder_weight_cols",
    )(w)


def _even_mixer(hb, w_in_all, layer, conv_w, conv_b, dt_bias, a_log, d_skip, norm_w, bias_c):
    p = _matmul(hb, w_in_all, layer, _EVEN_TN)
    dt_t = p[:, 5632:5632 + SSD_HEADS].T
    y_ssd = _ssd(p, dt_t, conv_w, conv_b, dt_bias, a_log, d_skip, norm_w,
                 {"z": 0, "x": 1, "bc": 10, "dt": 44})
    k_aug, v_t, kmean = _kprep(p, 3, 4, ATTN_HEADS, MOBA_BLOCK, True)
    q_aug = _moba_select(p, 2, kmean)
    y_moba = _flash(q_aug, k_aug, v_t, bias_c, heads=ATTN_HEADS, group=4, shared_kv=False, out_dtype=BF16)
    return y_ssd, y_moba


def _odd_mixer(hb, w_in_all, layer, cmp_pe, cmp_w1, cmp_w2, rel_bias, bias_c, bias_w, cmp_band):
    p = _matmul(hb, w_in_all, layer, _ODD_TN)
    t = p.shape[0]
    x2 = p[:, 5120:5632].reshape(t, 2, NSA_KV_HEADS, HEAD_DIM).transpose(1, 2, 0, 3)
    x2 = x2.reshape(2, NSA_KV_HEADS, t // NSA_CMP_STRIDE, NSA_CMP_STRIDE * HEAD_DIM)
    kv_cmp = _compress(x2, cmp_pe.reshape(2, 2, NSA_CMP_STRIDE * HEAD_DIM),
                       cmp_w1.astype(BF16), cmp_w2.astype(BF16))
    o_cmp, q_aug = _nsa_cmp(p, 0, kv_cmp, cmp_band, rel_bias)
    k_aug, v_t = _kprep(p, 22, 23, NSA_KV_HEADS, NSA_SEL_BLOCK)
    o_sel = _flash(q_aug, k_aug, v_t, bias_c, heads=ATTN_HEADS, group=NSA_GROUP, shared_kv=True, out_dtype=F32)
    k_win, v_win_t = _kprep(p, 24, 25, NSA_KV_HEADS, None)
    y_nsa = _flash((p, 0), k_win, v_win_t, bias_w, heads=ATTN_HEADS, group=NSA_GROUP, shared_kv=True,
                   out_dtype=BF16, span=NSA_WINDOW // CHUNK + 1, combine=(o_cmp, o_sel, p, 52))
    y_ret = _retention(p, {"rq": 8, "rk": 16, "rv": 24, "rg": 32})
    return y_nsa, y_ret


def kernel(x, rel_bias, ev_w_in, ev_conv_w, ev_conv_b, ev_dt_bias, ev_a_log, ev_d_skip, ev_norm_w, ev_w_out,
           od_w_in, od_cmp_pe, od_cmp_w1, od_cmp_w2, od_w_out,
           ffn_w_up, ffn_conv_w, ffn_conv_b, ffn_w_down, ln_g, ln_b):
    _, t0, _ = x.shape
    t = -(-t0 // CHUNK) * CHUNK
    h = jnp.pad(x[0], ((0, t - t0), (0, 0)))
    hb = h.astype(BF16)
    bias_c = _bias_tiles(rel_bias, 8)
    bias_w = _bias_tiles(rel_bias, NSA_WINDOW // CHUNK + 1, window=NSA_WINDOW)
    cmp_band = _cmp_band(rel_bias)
    ev_w_in_b = _reorder_cols(ev_w_in, _EVEN_ORDER, _EVEN_WIDTH)
    od_w_in_b = _reorder_cols(od_w_in, _ODD_ORDER, _ODD_WIDTH)
    ev_w_out_b, od_w_out_b = ev_w_out.astype(BF16), od_w_out.astype(BF16)
    w_down_b = ffn_w_down.astype(BF16)
    for layer in range(DEPTH):
        i = layer // 2
        if layer % 2 == 0:
            mix = _even_mixer(hb, ev_w_in_b, i, ev_conv_w[i], ev_conv_b[i], ev_dt_bias[i], ev_a_log[i],
                              ev_d_skip[i], ev_norm_w[i], bias_c)
            w_out_b = ev_w_out_b
        else:
            mix = _odd_mixer(hb, od_w_in_b, i, od_cmp_pe[i], od_cmp_w1[i], od_cmp_w2[i], rel_bias,
                             bias_c, bias_w, cmp_band)
            w_out_b = od_w_out_b
        h, hb = _matmul_res_ln(mix, w_out_b, i, h, ln_g[layer, 0], ln_b[layer, 0], _OUT_TK)
        gated = _ffn_up(hb, ffn_w_up, layer, ffn_conv_w[layer], ffn_conv_b[layer])
        h, hb = _matmul_res_ln((gated,), w_down_b, layer, h, ln_g[layer, 1], ln_b[layer, 1], _DOWN_TK)
    return h[None, :t0]
```

```python
import functools
import math

import jax
import jax.numpy as jnp
from jax import lax
from jax.experimental import pallas as pl
from jax.experimental.pallas import tpu as pltpu

F32 = jnp.float32
BF16 = jnp.bfloat16

D_MODEL = 2048
DEPTH = 4
HEAD_DIM = 128
ATTN_HEADS = 8
REL_BUCKETS = 32
REL_MAX_DISTANCE = 2048
SSD_HEADS = 16
SSD_HEAD_DIM = 64
SSD_INNER = 1024
SSD_GROUPS = 2
SSD_STATE = 128
SSD_CONV = 4
MOBA_BLOCK = 256
MOBA_TOPK = 3
NSA_KV_HEADS = 2
NSA_GROUP = 4
NSA_CMP_BLOCK = 32
NSA_CMP_STRIDE = 16
NSA_SEL_BLOCK = 64
NSA_SEL_TOPN = 16
NSA_WINDOW = 512
RET_HEADS = 8
ROPE_BASE = 10000.0
FFN_DIM = 5632
FFN_CONV = 3
LN_EPS = 1e-5
NEG = -1e30
BIG = 3e38
LOG2E = math.log2(math.e)
_QK_SCALE = HEAD_DIM ** -0.5 * LOG2E
DEEPNORM_ALPHA = (2 * DEPTH) ** 0.25

CHUNK = 256
LANES = 128
VMEM_LIMIT = 56 * 1024 * 1024


def _params(*sem):
    return pltpu.CompilerParams(dimension_semantics=sem, vmem_limit_bytes=VMEM_LIMIT)


def _silu(x):
    return x * (1.0 / (1.0 + jnp.exp(-x)))


def _dot(a, b):
    return jnp.dot(a, b, preferred_element_type=F32)


def _dot_nt(a, b):
    return lax.dot_general(a, b, (((1,), (1,)), ((), ())), preferred_element_type=F32)


def _dot_tn(a, b):
    return lax.dot_general(a, b, (((0,), (0,)), ((), ())), preferred_element_type=F32)


def _split_bf16(a):
    hi = a.astype(BF16)
    return hi, (a - hi.astype(F32)).astype(BF16)


def _dot_nt_split(a, b):
    ah, al = _split_bf16(a)
    bh, bl = _split_bf16(b)
    return _dot_nt(ah, bh) + (_dot_nt(ah, bl) + _dot_nt(al, bh))


def _split3_bf16(a):
    hi = a.astype(BF16)
    mid, lo = _split_bf16(a - hi.astype(F32))
    return hi, mid, lo


def _dot_f32_by_01(a, b01):
    return sum(_dot(t, b01) for t in _split3_bf16(a))


def _dot_01_by_f32(a01, b):
    return sum(_dot(a01, t) for t in _split3_bf16(b))


def _shift_rows(a, carry, k):
    rolled = pltpu.roll(a, k, 0)
    head = jnp.where(lax.broadcasted_iota(jnp.int32, carry.shape, 0) < k,
                     pltpu.roll(carry, k, 0), rolled[:8])
    return jnp.concatenate([head, rolled[8:]], axis=0)


def _mm_kernel(x_ref, w_ref, o_ref):
    o_ref[...] = _dot(x_ref[...], w_ref[...]).astype(o_ref.dtype)


def _matmul(x, w_all, layer, tn, out_dtype=F32):
    m, k = x.shape
    n = w_all.shape[2]
    tm = min(1024, m)
    return pl.pallas_call(
        _mm_kernel,
        out_shape=jax.ShapeDtypeStruct((m, n), out_dtype),
        grid=(m // tm, n // tn),
        in_specs=[pl.BlockSpec((tm, k), lambda i, j: (i, 0)),
                  pl.BlockSpec((None, k, tn), lambda i, j: (layer, 0, j))],
        out_specs=pl.BlockSpec((tm, tn), lambda i, j: (i, j)),
        compiler_params=_params("parallel", "arbitrary"),
        name="proj_matmul",
    )(x, w_all)


def _mm_ln_kernel(*refs, steps):
    a_refs = refs[:len(steps)]
    w_ref, h_ref, g_ref, b_ref, o_ref, ob_ref, acc_ref = refs[len(steps):]
    k = pl.program_id(1)

    @pl.when(k == 0)
    def _():
        acc_ref[...] = _dot(a_refs[0][...], w_ref[...])

    first = 0
    for a_ref, n in zip(a_refs, steps):
        @pl.when((k >= max(first, 1)) & (k < first + n))
        def _(a_ref=a_ref):
            acc_ref[...] += _dot(a_ref[...], w_ref[...])
        first += n

    @pl.when(k == pl.num_programs(1) - 1)
    def _():
        y = DEEPNORM_ALPHA * h_ref[...] + acc_ref[...]
        mu = jnp.mean(y, axis=-1, keepdims=True)
        yc = y - mu
        var = jnp.mean(yc * yc, axis=-1, keepdims=True)
        out = yc * lax.rsqrt(var + LN_EPS) * g_ref[...] + b_ref[...]
        o_ref[...] = out
        ob_ref[...] = out.astype(BF16)


def _matmul_res_ln(parts, w_all, layer, h, g, b, tk):
    m = h.shape[0]
    n = w_all.shape[2]
    tm = min(512, m)
    steps = tuple(a.shape[1] // tk for a in parts)
    starts = tuple(sum(steps[:p]) for p in range(len(parts)))

    def part_spec(p):
        return pl.BlockSpec((tm, tk), lambda i, j: (i, jnp.clip(j - starts[p], 0, steps[p] - 1)))

    return pl.pallas_call(
        functools.partial(_mm_ln_kernel, steps=steps),
        out_shape=(jax.ShapeDtypeStruct((m, n), F32), jax.ShapeDtypeStruct((m, n), BF16)),
        grid=(m // tm, sum(steps)),
        in_specs=[part_spec(p) for p in range(len(parts))] + [
            pl.BlockSpec((None, tk, n), lambda i, j: (layer, j, 0)),
            pl.BlockSpec((tm, n), lambda i, j: (i, 0)),
            pl.BlockSpec((1, n), lambda i, j: (0, 0)),
            pl.BlockSpec((1, n), lambda i, j: (0, 0))],
        out_specs=(pl.BlockSpec((tm, n), lambda i, j: (i, 0)),
                   pl.BlockSpec((tm, n), lambda i, j: (i, 0))),
        scratch_shapes=[pltpu.VMEM((tm, n), F32)],
        compiler_params=_params("parallel", "arbitrary"),
        name="matmul_res_ln",
    )(*parts, w_all, h, g.reshape(1, n), b.reshape(1, n))


def _ffn_up_kernel(x_ref, wa_ref, wu_ref, cw_ref, cb_ref, o_ref, carry_ref, wab_ref, wub_ref):
    @pl.when(pl.program_id(1) == 0)
    def _():
        carry_ref[...] = jnp.zeros_like(carry_ref)
        wab_ref[...] = wa_ref[...].astype(BF16)
        wub_ref[...] = wu_ref[...].astype(BF16)

    x = x_ref[...]
    a = _dot(x, wab_ref[...])
    u = _dot(x, wub_ref[...])
    carry = carry_ref[...]
    cw = cw_ref[...]
    conv = (cw[0:1] * _shift_rows(a, carry, 2) + cw[1:2] * _shift_rows(a, carry, 1)
            + cw[2:3] * a + cb_ref[...])
    carry_ref[...] = a[a.shape[0] - 8:]
    o_ref[...] = (_silu(conv) * u).astype(BF16)


def _ffn_up(x, w_up_all, layer, conv_w, conv_b):
    m, k = x.shape
    tn = 512
    tm = min(1024, m)
    nj = FFN_DIM // tn
    return pl.pallas_call(
        _ffn_up_kernel,
        out_shape=jax.ShapeDtypeStruct((m, FFN_DIM), BF16),
        grid=(nj, m // tm),
        in_specs=[pl.BlockSpec((tm, k), lambda j, i: (i, 0)),
                  pl.BlockSpec((None, k, tn), lambda j, i: (layer, 0, j)),
                  pl.BlockSpec((None, k, tn), lambda j, i: (layer, 0, j + nj)),
                  pl.BlockSpec((FFN_CONV, tn), lambda j, i: (0, j)),
                  pl.BlockSpec((1, tn), lambda j, i: (0, j))],
        out_specs=pl.BlockSpec((tm, tn), lambda j, i: (i, j)),
        scratch_shapes=[pltpu.VMEM((8, tn), F32), pltpu.VMEM((k, tn), BF16), pltpu.VMEM((k, tn), BF16)],
        compiler_params=_params("parallel", "arbitrary"),
        name="ffn_up_conv",
    )(x, w_up_all, w_up_all, conv_w, conv_b.reshape(1, FFN_DIM))


def _ssd_kernel(z_ref, x_ref, bc_ref, dt_ref, dtt_ref, cwx_ref, cbx_ref, cwb_ref, cbb_ref,
                dtb_ref, alog_ref, dtbc_ref, alogc_ref, dskip_ref, nw_ref, expand_ref,
                o_ref, cx_ref, cbc_ref, state_ref):
    L = CHUNK
    hpg = SSD_HEADS // SSD_GROUPS
    gw = hpg * SSD_HEAD_DIM

    @pl.when(pl.program_id(0) == 0)
    def _():
        cx_ref[...] = jnp.zeros_like(cx_ref)
        cbc_ref[...] = jnp.zeros_like(cbc_ref)
        state_ref[...] = jnp.zeros_like(state_ref)

    def conv(raw, carry_ref, w_ref, b_ref):
        carry = carry_ref[...]
        w = w_ref[...]
        y = (w[0:1] * _shift_rows(raw, carry, 3) + w[1:2] * _shift_rows(raw, carry, 2)
             + w[2:3] * _shift_rows(raw, carry, 1) + w[3:4] * raw + b_ref[...])
        carry_ref[...] = raw[L - 8:]
        return _silu(y)

    xs = conv(x_ref[...], cx_ref, cwx_ref, cbx_ref)
    bcs = conv(bc_ref[...], cbc_ref, cwb_ref, cbb_ref)

    def softplus(v):
        return jnp.maximum(v, 0.0) + jnp.log(1.0 + jnp.exp(-jnp.abs(v)))

    dt = softplus(dt_ref[...] + dtb_ref[...])
    da = dt * (-jnp.exp(alog_ref[...]))
    row = lax.broadcasted_iota(jnp.int32, (L, L), 0)
    col = lax.broadcasted_iota(jnp.int32, (L, L), 1)
    causal = row >= col
    a_cum = _dot_01_by_f32(causal.astype(BF16), da)
    dt_r = softplus(dtt_ref[...] + dtbc_ref[...])
    da_r = dt_r * (-jnp.exp(alogc_ref[...]))
    a_cum_r = _dot_f32_by_01(da_r, (row <= col).astype(BF16))

    expand = expand_ref[...]

    def widen(v):
        return _dot_f32_by_01(v, expand)

    a_last = a_cum[L - 1:L]
    dt_w = widen(dt)
    decay_in_w = widen(jnp.exp(a_cum))
    decay_end_w = widen(jnp.exp(a_last - a_cum))
    chunk_decay_w = widen(jnp.broadcast_to(jnp.exp(a_last), (8, LANES)))[0:1]
    xdt = xs * dt_w
    xdt_b = xdt.astype(BF16)
    xdt_end_b = (xdt * decay_end_w).astype(BF16)
    lane_lo = lax.broadcasted_iota(jnp.int32, (L, LANES), 1) < SSD_HEAD_DIM

    ys = []
    for g in range(SSD_GROUPS):
        b_g = bcs[:, g * SSD_STATE:(g + 1) * SSD_STATE].astype(BF16)
        c_g = bcs[:, (SSD_GROUPS + g) * SSD_STATE:(SSD_GROUPS + g + 1) * SSD_STATE].astype(BF16)
        cb = _dot_nt(c_g, b_g)
        pieces = []
        for pair in range(hpg // 2):
            outs = []
            for k in range(2):
                h = g * hpg + pair * 2 + k
                diff = a_cum[:, h:h + 1] - a_cum_r[h:h + 1, :]
                w = (cb * jnp.exp(jnp.where(causal, diff, NEG))).astype(BF16)
                lo = (g * hpg + pair * 2) * SSD_HEAD_DIM
                outs.append(_dot(w, xdt_b[:, lo:lo + LANES]))
            pieces.append(jnp.where(lane_lo, outs[0], outs[1]))
        y_diag = jnp.concatenate(pieces, axis=1)
        state = state_ref[g]
        sl = slice(g * gw, (g + 1) * gw)
        y_off = _dot(c_g, state.astype(BF16)) * decay_in_w[:, sl]
        state_ref[g] = state * chunk_decay_w[:, sl] + _dot_tn(b_g, xdt_end_b[:, sl])
        ys.append(y_diag + y_off)
    y = jnp.concatenate(ys, axis=1) + xs * dskip_ref[...]
    yg = y * _silu(z_ref[...])
    half = SSD_INNER // SSD_GROUPS
    outs = []
    for g in range(SSD_GROUPS):
        part = yg[:, g * half:(g + 1) * half]
        outs.append(part * lax.rsqrt(jnp.mean(part * part, axis=-1, keepdims=True) + LN_EPS))
    o_ref[...] = (jnp.concatenate(outs, axis=1) * nw_ref[...]).astype(BF16)


def _ssd(p, dt_t, conv_w, conv_b, dt_bias, a_log, d_skip, norm_w, col):
    t = p.shape[0]
    L = CHUNK
    pad16 = lambda v: jnp.pad(v.astype(F32), (0, LANES - SSD_HEADS)).reshape(1, LANES)
    expand = (jnp.arange(LANES)[:, None] == (jnp.arange(SSD_INNER) // SSD_HEAD_DIM)[None, :]).astype(BF16)
    full = lambda shape: pl.BlockSpec(shape, lambda c: (0,) * len(shape))
    return pl.pallas_call(
        _ssd_kernel,
        out_shape=jax.ShapeDtypeStruct((t, SSD_INNER), BF16),
        grid=(t // L,),
        in_specs=[pl.BlockSpec((L, 1024), lambda c: (c, col["z"])),
                  pl.BlockSpec((L, 1024), lambda c: (c, col["x"])),
                  pl.BlockSpec((L, 512), lambda c: (c, col["bc"])),
                  pl.BlockSpec((L, LANES), lambda c: (c, col["dt"])),
                  pl.BlockSpec((SSD_HEADS, L), lambda c: (0, c)),
                  full((SSD_CONV, 1024)), full((1, 1024)), full((SSD_CONV, 512)), full((1, 512)),
                  full((1, LANES)), full((1, LANES)), full((SSD_HEADS, 1)), full((SSD_HEADS, 1)),
                  full((1, 1024)), full((1, 1024)), full((LANES, 1024))],
        out_specs=pl.BlockSpec((L, SSD_INNER), lambda c: (c, 0)),
        scratch_shapes=[pltpu.VMEM((8, 1024), F32), pltpu.VMEM((8, 512), F32),
                        pltpu.VMEM((SSD_GROUPS, SSD_STATE, 512), F32)],
        compiler_params=_params("arbitrary"),
        name="ssd_mixer",
    )(p, p, p, p, dt_t,
      conv_w[:, :SSD_INNER], conv_b[:SSD_INNER].reshape(1, -1),
      conv_w[:, SSD_INNER:], conv_b[SSD_INNER:].reshape(1, -1),
      pad16(dt_bias), pad16(a_log), dt_bias.reshape(-1, 1), a_log.reshape(-1, 1),
      jnp.repeat(d_skip, SSD_HEAD_DIM).reshape(1, -1), norm_w.reshape(1, -1), expand)


_RET_GROUP = 4


def _ret_kernel(logg_ref, q_ref, k_ref, v_ref, g_ref, cos_ref, sin_ref, o_ref,
                state_ref, decay_ref, qdec_ref, kdec_ref, cdec_ref):
    C = CHUNK
    hg = pl.program_id(0)
    head_cols = lambda g: slice(g * HEAD_DIM, (g + 1) * HEAD_DIM)

    @pl.when(pl.program_id(1) == 0)
    def _():
        state_ref[...] = jnp.zeros_like(state_ref)
        rel = lax.broadcasted_iota(jnp.int32, (C, C), 0) - lax.broadcasted_iota(jnp.int32, (C, C), 1)
        pos = lax.broadcasted_iota(jnp.int32, (C, 1), 0).astype(F32)
        for g in range(_RET_GROUP):
            lg = logg_ref[hg * _RET_GROUP + g]
            decay_ref[g] = jnp.where(rel >= 0, jnp.exp(lg * jnp.maximum(rel, 0).astype(F32)), 0.0)
            qdec_ref[g] = jnp.exp(lg * (pos + 1.0))
            kdec_ref[g] = jnp.exp(lg * (C - 1.0 - pos))
            cdec_ref[g] = jnp.exp(lg * jnp.full((1, HEAD_DIM), float(C), F32))

    cos = cos_ref[...]
    sin = sin_ref[...]
    even_lane = lax.broadcasted_iota(jnp.int32, (C, HEAD_DIM), 1) % 2 == 0

    def rot(x):
        swapped = jnp.where(even_lane, pltpu.roll(x, HEAD_DIM - 1, 1), pltpu.roll(x, 1, 1))
        return x * cos + swapped * sin

    qs = [rot(q_ref[:, head_cols(g)]) for g in range(_RET_GROUP)]
    ks = [rot(k_ref[:, head_cols(g)]) * (HEAD_DIM ** -0.5) for g in range(_RET_GROUP)]
    vs = [v_ref[:, head_cols(g)].astype(BF16) for g in range(_RET_GROUP)]
    inners = [_dot_nt(qs[g].astype(BF16), ks[g].astype(BF16)) for g in range(_RET_GROUP)]
    for g in range(_RET_GROUP):
        state = state_ref[g]
        y = _dot((inners[g] * decay_ref[g]).astype(BF16), vs[g])
        y = y + _dot((qs[g] * qdec_ref[g]).astype(BF16), state.astype(BF16))
        state_ref[g] = state * cdec_ref[g] + _dot_tn((ks[g] * kdec_ref[g]).astype(BF16), vs[g])
        mu = jnp.mean(y, axis=-1, keepdims=True)
        yc = y - mu
        var = jnp.mean(yc * yc, axis=-1, keepdims=True)
        o_ref[:, head_cols(g)] = (_silu(g_ref[:, head_cols(g)]) * (yc * lax.rsqrt(var + LN_EPS))).astype(BF16)


def _retention(p, col):
    t = p.shape[0]
    C = CHUNK
    G = _RET_GROUP
    inv = 1.0 / (ROPE_BASE ** (jnp.arange(0, HEAD_DIM, 2, dtype=F32) / HEAD_DIM))
    ang = jnp.arange(t, dtype=F32)[:, None] * inv[None, :]
    cos = jnp.repeat(jnp.cos(ang), 2, axis=1)
    sin = jnp.stack([-jnp.sin(ang), jnp.sin(ang)], axis=-1).reshape(t, HEAD_DIM)
    log_g = jnp.log(1.0 - 2.0 ** (-5.0 - jnp.arange(RET_HEADS, dtype=F32)))
    blk = lambda name: pl.BlockSpec((C, G * HEAD_DIM), lambda h, c, lg: (c, col[name] // G + h))
    tab = pl.BlockSpec((C, HEAD_DIM), lambda h, c, lg: (c, 0))
    return pl.pallas_call(
        _ret_kernel,
        out_shape=jax.ShapeDtypeStruct((t, RET_HEADS * HEAD_DIM), BF16),
        grid_spec=pltpu.PrefetchScalarGridSpec(
            num_scalar_prefetch=1,
            grid=(RET_HEADS // G, t // C),
            in_specs=[blk("rq"), blk("rk"), blk("rv"), blk("rg"), tab, tab],
            out_specs=pl.BlockSpec((C, G * HEAD_DIM), lambda h, c, lg: (c, h)),
            scratch_shapes=[pltpu.VMEM((G, HEAD_DIM, HEAD_DIM), F32), pltpu.VMEM((G, C, C), F32),
                            pltpu.VMEM((G, C, 1), F32), pltpu.VMEM((G, C, 1), F32),
                            pltpu.VMEM((G, 1, HEAD_DIM), F32)]),
        compiler_params=_params("parallel", "arbitrary"),
        name="retention_mixer",
    )(log_g, p, p, p, p, cos, sin)


def _rel_bucket(dist):
    n = jnp.maximum(dist, 0)
    max_exact = REL_BUCKETS // 2
    nf = jnp.maximum(n, max_exact).astype(F32)
    large = max_exact + (jnp.log(nf / max_exact) / math.log(REL_MAX_DISTANCE / max_exact)
                         * (REL_BUCKETS - max_exact)).astype(jnp.int32)
    large = jnp.minimum(large, REL_BUCKETS - 1)
    return jnp.where(n < max_exact, n, large)


def _bias_lookup(tab_ref, bucket, h):
    rows, cols = bucket.shape
    lane = lax.broadcasted_iota(jnp.int32, (1, LANES), 1)
    tab_row = jnp.zeros((1, LANES), F32)
    for b in range(REL_BUCKETS):
        tab_row = jnp.where(lane == b, tab_ref[b * ATTN_HEADS + h], tab_row)
    table = jnp.broadcast_to(tab_row, (rows, LANES))
    return jnp.concatenate([jnp.take_along_axis(table, bucket[:, c:c + LANES], axis=1)
                            for c in range(0, cols, LANES)], axis=1)


def _bias_tile_kernel(tab_ref, o_ref, *, window):
    h = pl.program_id(0)
    o = pl.program_id(1)
    row = lax.broadcasted_iota(jnp.int32, (CHUNK, CHUNK), 0)
    col = lax.broadcasted_iota(jnp.int32, (CHUNK, CHUNK), 1)
    d = o * CHUNK + col - row
    valid = d >= 0
    if window is not None:
        valid = valid & (d < window)
    o_ref[0, 0] = jnp.where(valid, _bias_lookup(tab_ref, _rel_bucket(d), h) * LOG2E, NEG)


def _bias_tiles(rel_bias, n_off, window=None):
    return pl.pallas_call(
        functools.partial(_bias_tile_kernel, window=window),
        out_shape=jax.ShapeDtypeStruct((ATTN_HEADS, n_off, CHUNK, CHUNK), F32),
        grid_spec=pltpu.PrefetchScalarGridSpec(
            num_scalar_prefetch=1, grid=(ATTN_HEADS, n_off), in_specs=[],
            out_specs=pl.BlockSpec((1, 1, CHUNK, CHUNK), lambda h, o, tab: (h, o, 0, 0))),
        compiler_params=_params("parallel", "parallel"),
        name="rel_bias_tiles",
    )(rel_bias.reshape(-1))


_CMP_BAND_PERIOD = LANES * NSA_CMP_STRIDE // CHUNK


def _cmp_band_kernel(tab_ref, o_ref):
    h = pl.program_id(0)
    b = pl.program_id(1)
    row = lax.broadcasted_iota(jnp.int32, (CHUNK, 2 * LANES), 0)
    col = lax.broadcasted_iota(jnp.int32, (CHUNK, 2 * LANES), 1)
    d = b * CHUNK + row - NSA_CMP_STRIDE * (col - LANES) - (NSA_CMP_BLOCK - 1)
    o_ref[0, 0] = jnp.where(d >= 0, _bias_lookup(tab_ref, _rel_bucket(d), h), NEG)


def _cmp_band(rel_bias):
    return pl.pallas_call(
        _cmp_band_kernel,
        out_shape=jax.ShapeDtypeStruct((ATTN_HEADS, _CMP_BAND_PERIOD, CHUNK, 2 * LANES), F32),
        grid_spec=pltpu.PrefetchScalarGridSpec(
            num_scalar_prefetch=1, grid=(ATTN_HEADS, _CMP_BAND_PERIOD), in_specs=[],
            out_specs=pl.BlockSpec((1, 1, CHUNK, 2 * LANES), lambda h, b, tab: (h, b, 0, 0))),
        compiler_params=_params("parallel", "parallel"),
        name="cmp_bias_band",
    )(rel_bias.reshape(-1))


def _kprep_kernel(k_ref, v_ref, ka_ref, vt_ref, *rest, heads, block, with_mean):
    i = pl.program_id(0)
    k = k_ref[...]
    v = v_ref[...]
    if block is not None:
        row = lax.broadcasted_iota(jnp.int32, (CHUNK, LANES), 0)
        lane = lax.broadcasted_iota(jnp.int32, (CHUNK, LANES), 1)
        onehot = (lane == (i * CHUNK + row) // block).astype(BF16)
    for h in range(heads):
        sl = slice(h * HEAD_DIM, (h + 1) * HEAD_DIM)
        kh = k[:, sl].astype(BF16)
        ka_ref[h, 0] = kh if block is None else jnp.concatenate([kh, onehot], axis=1)
        vt_ref[h, 0] = v[:, sl].T.astype(BF16)
    if with_mean:
        km_ref = rest[0]

        @pl.when(i == 0)
        def _():
            km_ref[...] = jnp.zeros_like(km_ref)

        km_ref[pl.ds(i, 1), :] = jnp.mean(k, axis=0, keepdims=True)


def _kprep(p, kcol, vcol, heads, block, with_mean=False):
    t = p.shape[0]
    nt = t // CHUNK
    w = heads * HEAD_DIM
    kw = HEAD_DIM if block is None else 2 * HEAD_DIM
    out_shape = [jax.ShapeDtypeStruct((heads, nt, CHUNK, kw), BF16),
                 jax.ShapeDtypeStruct((heads, nt, HEAD_DIM, CHUNK), BF16)]
    out_specs = [pl.BlockSpec((heads, 1, CHUNK, kw), lambda i: (0, i, 0, 0)),
                 pl.BlockSpec((heads, 1, HEAD_DIM, CHUNK), lambda i: (0, i, 0, 0))]
    if with_mean:
        out_shape.append(jax.ShapeDtypeStruct((LANES, w), F32))
        out_specs.append(pl.BlockSpec((LANES, w), lambda i: (0, 0)))
    return pl.pallas_call(
        functools.partial(_kprep_kernel, heads=heads, block=block, with_mean=with_mean),
        out_shape=tuple(out_shape),
        grid=(t // CHUNK,),
        in_specs=[pl.BlockSpec((CHUNK, w), lambda i: (i, kcol)),
                  pl.BlockSpec((CHUNK, w), lambda i: (i, vcol))],
        out_specs=tuple(out_specs),
        compiler_params=_params("arbitrary"),
        name="key_prep",
    )(p, p)


def _topk_mask_t(score_t, k):
    row = lax.broadcasted_iota(jnp.int32, score_t.shape, 0)

    def body(_, g):
        m = jnp.max(g, axis=0, keepdims=True)
        idx = jnp.min(jnp.where(g == m, row, score_t.shape[0]), axis=0, keepdims=True)
        return jnp.where(row == idx, -BIG, g)

    left = lax.fori_loop(0, k, body, score_t)
    return (left < score_t) & (score_t > -0.5 * BIG)


def _moba_select_kernel(q_ref, km_ref, qa_ref):
    i = pl.program_id(0)
    q = q_ref[...]
    km = km_ref[...]
    block = lax.broadcasted_iota(jnp.int32, (LANES, CHUNK), 0)
    head_cols = lambda h: slice(h * HEAD_DIM, (h + 1) * HEAD_DIM)
    gates = [jnp.where(block < i, _dot_nt_split(km[:, head_cols(h)], q[:, head_cols(h)]), -BIG)
             for h in range(ATTN_HEADS)]
    sel = _topk_mask_t(jnp.concatenate(gates, axis=1), MOBA_TOPK)
    own = block == i
    for h in range(ATTN_HEADS):
        pen_t = jnp.where(sel[:, h * CHUNK:(h + 1) * CHUNK] | own, 0.0, NEG)
        q_t = (q[:, head_cols(h)] * _QK_SCALE).T
        qa_ref[h, 0] = jnp.concatenate([q_t, pen_t], axis=0).astype(BF16)


def _moba_select(p, qcol, kmean):
    t = p.shape[0]
    w = ATTN_HEADS * HEAD_DIM
    return pl.pallas_call(
        _moba_select_kernel,
        out_shape=jax.ShapeDtypeStruct((ATTN_HEADS, t // CHUNK, 2 * HEAD_DIM, CHUNK), BF16),
        grid=(t // CHUNK,),
        in_specs=[pl.BlockSpec((CHUNK, w), lambda i: (i, qcol)),
                  pl.BlockSpec((LANES, w), lambda i: (0, 0))],
        out_specs=pl.BlockSpec((ATTN_HEADS, 1, 2 * HEAD_DIM, CHUNK), lambda i: (0, i, 0, 0)),
        compiler_params=_params("parallel"),
        name="moba_select",
    )(p, kmean)


def _flash_kernel(q_ref, k_ref, vt_ref, bias_ref, *rest, group, shared_kv, span, combine):
    if combine:
        ocmp_ref, osel_ref, gate_ref, o_ref = rest[:4]
    else:
        o_ref = rest[0]
    m_ref, l_ref, acc_ref, alpha_ref, pt_ref, s_even_ref, s_odd_ref = rest[-7:]
    hg = pl.program_id(0)
    i = pl.program_id(1)
    n_off = bias_ref.shape[1]
    nt = k_ref.shape[-3]
    head_cols = lambda g: slice(g * HEAD_DIM, (g + 1) * HEAD_DIM)
    if len(q_ref.shape) == 3:
        qts = [q_ref[g] for g in range(group)]
    else:
        qts = [(q_ref[:, head_cols(g)] * _QK_SCALE).T.astype(BF16) for g in range(group)]
    m_ref[...] = jnp.full_like(m_ref, -BIG)
    l_ref[...] = jnp.zeros_like(l_ref)
    acc_ref[...] = jnp.zeros_like(acc_ref)
    k_tile = lambda g, j: k_ref[j] if shared_kv else k_ref[g, j]
    vt_tile = lambda g, j: vt_ref[j] if shared_kv else vt_ref[g, j]
    alpha_ref[...] = jnp.ones_like(alpha_ref)
    pt_ref[...] = jnp.zeros_like(pt_ref)

    def scores(j, s_ref):
        for g in range(group):
            s_ref[g] = _dot(k_tile(g, j), qts[g])

    def value_update(j):
        for g in range(group):
            acc_ref[g] = alpha_ref[g] * acc_ref[g] + _dot(vt_tile(g, j), pt_ref[g])

    def step(j, s_ref, s_next_ref, const_bias):
        scores(jnp.minimum(j + 1, nt - 1), s_next_ref)
        value_update(jnp.maximum(j - 1, 0))
        for g in range(group):
            m_prev = m_ref[g]
            if const_bias:
                far = bias_ref[g, n_off - 1, 0:1, :]
                st = s_ref[g]
                m_new = jnp.maximum(m_prev, jnp.max(st, axis=0, keepdims=True) + far)
                pt = jnp.exp2(st - (m_new - far))
            else:
                st = s_ref[g] + bias_ref[g, i - j]
                m_new = jnp.maximum(m_prev, jnp.max(st, axis=0, keepdims=True))
                pt = jnp.exp2(st - m_new)
            alpha = jnp.exp2(m_prev - m_new)
            l_ref[g] = alpha * l_ref[g] + jnp.sum(pt, axis=0, keepdims=True)
            m_ref[g] = m_new
            alpha_ref[g] = alpha
            pt_ref[g] = pt.astype(BF16)

    def pairs(start, count, const_bias):
        def body(t, carry):
            j = start + 2 * t
            step(j, s_even_ref, s_odd_ref, const_bias)
            step(j + 1, s_odd_ref, s_even_ref, const_bias)
            return carry
        lax.fori_loop(0, count, body, 0)

    if span is None:
        n_far = jnp.maximum(i - (n_off - 2), 0) // 2 * 2
        j_lo = 0
    else:
        n_far = 0
        j_lo = jnp.maximum(i - (span - 1), 0)
    scores(j_lo, s_even_ref)
    if span is None:
        pairs(j_lo, n_far // 2, True)
    near_lo = j_lo + n_far
    n_near = i + 1 - near_lo
    pairs(near_lo, n_near // 2, False)

    @pl.when(n_near % 2 == 1)
    def _():
        step(i, s_even_ref, s_odd_ref, False)

    value_update(i)
    if combine:
        sig = 1.0 / (1.0 + jnp.exp(-gate_ref[...]))
        lane = lax.broadcasted_iota(jnp.int32, sig.shape, 1)
    for g in range(group):
        out = (acc_ref[g] * (1.0 / l_ref[g])).T
        if combine:
            h = hg * group + g
            pick = lambda c: jnp.sum(jnp.where(lane == 3 * h + c, sig, 0.0), axis=1, keepdims=True)
            out = pick(0) * ocmp_ref[:, head_cols(g)] + pick(1) * osel_ref[:, head_cols(g)] + pick(2) * out
        o_ref[:, head_cols(g)] = out.astype(o_ref.dtype)


def _flash(q, k, vt, bias, *, heads, group, shared_kv, out_dtype, span=None, combine=None):
    n_off = bias.shape[1]
    nt = k.shape[1]
    gw = group * HEAD_DIM
    if isinstance(q, tuple):
        q, base = q
        q_spec = pl.BlockSpec((CHUNK, gw), lambda hg, i: (i, base // group + hg))
    else:
        q_spec = pl.BlockSpec((group, None) + q.shape[2:], lambda hg, i: (hg, i, 0, 0))
    kv_block = (None,) if shared_kv else (group,)
    resident = lambda shape: pl.BlockSpec(shape, lambda hg, i: (hg, 0, 0, 0), pipeline_mode=pl.Buffered(1))
    in_specs = [q_spec, resident(kv_block + k.shape[1:]), resident(kv_block + vt.shape[1:]),
                resident((group, n_off, CHUNK, CHUNK))]
    args = [q, k, vt, bias]
    tile = pl.BlockSpec((CHUNK, gw), lambda hg, i: (i, hg))
    if combine is not None:
        ocmp, osel, p, gcol = combine
        in_specs += [tile, tile, pl.BlockSpec((CHUNK, LANES), lambda hg, i: (i, gcol))]
        args += [ocmp, osel, p]
    return pl.pallas_call(
        functools.partial(_flash_kernel, group=group, shared_kv=shared_kv, span=span,
                          combine=combine is not None),
        out_shape=jax.ShapeDtypeStruct((nt * CHUNK, heads * HEAD_DIM), out_dtype),
        grid=(heads // group, nt),
        in_specs=in_specs,
        out_specs=tile,
        scratch_shapes=[pltpu.VMEM((group, 1, CHUNK), F32),
                        pltpu.VMEM((group, 1, CHUNK), F32),
                        pltpu.VMEM((group, HEAD_DIM, CHUNK), F32), pltpu.VMEM((group, 1, CHUNK), F32),
                        pltpu.VMEM((group, CHUNK, CHUNK), BF16),
                        pltpu.VMEM((group, CHUNK, CHUNK), F32), pltpu.VMEM((group, CHUNK, CHUNK), F32)],
        compiler_params=_params("parallel", "arbitrary"),
        name="flash_attention",
    )(*args)


def _compress_kernel(x_ref, pe_ref, w1_ref, w2_ref, o_ref):
    x = x_ref[0, 0]
    pe = pe_ref[0]
    half = NSA_CMP_STRIDE * HEAD_DIM
    a = _dot((x + pe[0:1]).astype(BF16), w1_ref[0, :half, :])
    b = _dot((x + pe[1:2]).astype(BF16), w1_ref[0, half:, :])
    hid = a + pltpu.roll(b, x.shape[0] - 1, 0)
    gelu = 0.5 * hid * (1.0 + jnp.tanh(math.sqrt(2.0 / math.pi) * (hid + 0.044715 * hid * hid * hid)))
    o_ref[0, 0] = _dot(gelu.astype(BF16), w2_ref[0])


def _compress(x2, pe, w1, w2):
    _, hkv, w, kd = x2.shape
    return pl.pallas_call(
        _compress_kernel,
        out_shape=jax.ShapeDtypeStruct((2, hkv, w, HEAD_DIM), F32),
        grid=(2, hkv),
        in_specs=[pl.BlockSpec((1, 1, w, kd), lambda a, h: (a, h, 0, 0)),
                  pl.BlockSpec((1, 2, kd), lambda a, h: (a, 0, 0)),
                  pl.BlockSpec((1, 2 * kd, HEAD_DIM), lambda a, h: (a, 0, 0)),
                  pl.BlockSpec((1, HEAD_DIM, HEAD_DIM), lambda a, h: (a, 0, 0))],
        out_specs=pl.BlockSpec((1, 1, w, HEAD_DIM), lambda a, h: (a, h, 0, 0)),
        compiler_params=_params("parallel", "parallel"),
        name="nsa_compress",
    )(x2, pe, w1, w2)


def _nsa_cmp_kernel(tab_ref, q_ref, kv_ref, band_ref, ov_ref, ocmp_ref, qa_ref, imp_ref):
    i = pl.program_id(0)
    width = kv_ref.shape[2]
    scale = HEAD_DIM ** -0.5
    q = q_ref[...]
    block = lax.broadcasted_iota(jnp.int32, (LANES, CHUNK), 0)
    cur = (i * CHUNK + lax.broadcasted_iota(jnp.int32, (LANES, CHUNK), 1)) // NSA_SEL_BLOCK
    forced = (block == 0) | (block == cur) | (block == cur - 1)
    diag = i // _CMP_BAND_PERIOD
    head_cols = lambda h: slice(h * HEAD_DIM, (h + 1) * HEAD_DIM)

    def attend(n_tiles):
        w = n_tiles * LANES
        ov_t = ov_ref[:, :w]
        for kvh in range(NSA_KV_HEADS):
            kc = kv_ref[0, kvh, :w, :].astype(BF16)
            vc = kv_ref[1, kvh, :w, :].astype(BF16)
            psum = jnp.zeros((CHUNK, w), F32)
            for g in range(NSA_GROUP):
                h = kvh * NSA_GROUP + g
                far = tab_ref[(REL_BUCKETS - 1) * ATTN_HEADS + h]
                band = band_ref[h]
                tiles = [jnp.full((CHUNK, LANES), far, F32)] * (n_tiles - 2)
                tiles += [band[:, :LANES], band[:, LANES:]][-min(n_tiles, 2):]
                s = _dot_nt(q[:, head_cols(h)].astype(BF16), kc) * scale + jnp.concatenate(tiles, axis=1)
                m = jnp.max(s, axis=1, keepdims=True)
                e = jnp.exp(s - m)
                l = jnp.sum(e, axis=1, keepdims=True)
                pr = e * jnp.where(m > 0.1 * NEG, 1.0 / l, 0.0)
                ocmp_ref[:, head_cols(h)] = _dot(pr.astype(BF16), vc)
                psum = psum + pr
            hi, lo = _split_bf16(psum)
            imp_ref[kvh] = _dot_nt(ov_t, hi) + _dot_nt(ov_t, lo)

    for n_tiles in range(1, width // LANES + 1):
        pl.when(diag == n_tiles - 1)(functools.partial(attend, n_tiles))

    scores = [jnp.where(forced, BIG, jnp.where(block <= cur, imp_ref[kvh], -BIG)) for kvh in range(NSA_KV_HEADS)]
    sel = _topk_mask_t(jnp.concatenate(scores, axis=1), NSA_SEL_TOPN)
    for kvh in range(NSA_KV_HEADS):
        pen_t = jnp.where(sel[:, kvh * CHUNK:(kvh + 1) * CHUNK], 0.0, NEG).astype(BF16)
        for g in range(NSA_GROUP):
            h = kvh * NSA_GROUP + g
            qa_ref[h, 0] = jnp.concatenate([(q[:, head_cols(h)] * _QK_SCALE).T.astype(BF16), pen_t], axis=0)


def _nsa_cmp(p, qcol, kv_cmp, band, rel_bias):
    t = p.shape[0]
    width = kv_cmp.shape[2]
    w = ATTN_HEADS * HEAD_DIM
    n = jnp.arange(width)[None, :] * NSA_CMP_STRIDE
    j = jnp.arange(LANES)[:, None] * NSA_SEL_BLOCK
    overlap = ((n < j + NSA_SEL_BLOCK) & (n + NSA_CMP_BLOCK > j)
               & (jnp.arange(width)[None, :] < width - 1)).astype(BF16)
    return pl.pallas_call(
        _nsa_cmp_kernel,
        out_shape=(jax.ShapeDtypeStruct((t, w), F32),
                   jax.ShapeDtypeStruct((ATTN_HEADS, t // CHUNK, 2 * HEAD_DIM, CHUNK), BF16)),
        grid_spec=pltpu.PrefetchScalarGridSpec(
            num_scalar_prefetch=1,
            grid=(t // CHUNK,),
            in_specs=[pl.BlockSpec((CHUNK, w), lambda i, tab: (i, qcol)),
                      pl.BlockSpec(kv_cmp.shape, lambda i, tab: (0, 0, 0, 0)),
                      pl.BlockSpec((ATTN_HEADS, None, CHUNK, 2 * LANES),
                                   lambda i, tab: (0, i % _CMP_BAND_PERIOD, 0, 0)),
                      pl.BlockSpec((LANES, width), lambda i, tab: (0, 0))],
            out_specs=(pl.BlockSpec((CHUNK, w), lambda i, tab: (i, 0)),
                       pl.BlockSpec((ATTN_HEADS, 1, 2 * HEAD_DIM, CHUNK), lambda i, tab: (0, i, 0, 0))),
            scratch_shapes=[pltpu.VMEM((NSA_KV_HEADS, LANES, CHUNK), F32)]),
        compiler_params=_params("parallel"),
        name="nsa_cmp_select",
    )(rel_bias.reshape(-1), p, kv_cmp, band, overlap)


_EVEN_ORDER = ((0, 2048), (2576, 5648), (2048, 2576))
_EVEN_WIDTH = 5760
_EVEN_TN = 1920
_OUT_TK = 1024
_DOWN_TK = 1408
_ODD_ORDER = ((0, 1024), (2584, 6680), (1024, 2584))
_ODD_WIDTH = 6912
_ODD_TN = 768


def _reorder_kernel(w_ref, o_ref, *, order):
    w = w_ref[...]
    parts = [w[:, a:b] for a, b in order]
    pad = o_ref.shape[1] - sum(b - a for a, b in order)
    parts.append(jnp.zeros((w.shape[0], pad), w.dtype))
    o_ref[...] = jnp.concatenate(parts, axis=1).astype(BF16)


def _reorder_cols(w, order, width):
    layers, d, n = w.shape
    rows = 256
    return pl.pallas_call(
        functools.partial(_reorder_kernel, order=order),
        out_shape=jax.ShapeDtypeStruct((layers, d, width), BF16),
        grid=(layers, d // rows),
        in_specs=[pl.BlockSpec((None, rows, n), lambda l, r: (l, r, 0))],
        out_specs=pl.BlockSpec((None, rows, width), lambda l, r: (l, r, 0)),
        compiler_params=_params("parallel", "parallel"),
        name="reorder_weight_cols",
    )(w)


def _even_mixer(hb, w_in_all, layer, conv_w, conv_b, dt_bias, a_log, d_skip, norm_w, bias_c):
    p = _matmul(hb, w_in_all, layer, _EVEN_TN)
    dt_t = p[:, 5632:5632 + SSD_HEADS].T
    y_ssd = _ssd(p, dt_t, conv_w, conv_b, dt_bias, a_log, d_skip, norm_w,
                 {"z": 0, "x": 1, "bc": 10, "dt": 44})
    k_aug, v_t, kmean = _kprep(p, 3, 4, ATTN_HEADS, MOBA_BLOCK, True)
    q_aug = _moba_select(p, 2, kmean)
    y_moba = _flash(q_aug, k_aug, v_t, bias_c, heads=ATTN_HEADS, group=4, shared_kv=False, out_dtype=BF16)
    return y_ssd, y_moba


def _odd_mixer(hb, w_in_all, layer, cmp_pe, cmp_w1, cmp_w2, rel_bias, bias_c, bias_w, cmp_band):
    p = _matmul(hb, w_in_all, layer, _ODD_TN)
    t = p.shape[0]
    x2 = p[:, 5120:5632].reshape(t, 2, NSA_KV_HEADS, HEAD_DIM).transpose(1, 2, 0, 3)
    x2 = x2.reshape(2, NSA_KV_HEADS, t // NSA_CMP_STRIDE, NSA_CMP_STRIDE * HEAD_DIM)
    kv_cmp = _compress(x2, cmp_pe.reshape(2, 2, NSA_CMP_STRIDE * HEAD_DIM),
                       cmp_w1.astype(BF16), cmp_w2.astype(BF16))
    o_cmp, q_aug = _nsa_cmp(p, 0, kv_cmp, cmp_band, rel_bias)
    k_aug, v_t = _kprep(p, 22, 23, NSA_KV_HEADS, NSA_SEL_BLOCK)
    o_sel = _flash(q_aug, k_aug, v_t, bias_c, heads=ATTN_HEADS, group=NSA_GROUP, shared_kv=True, out_dtype=F32)
    k_win, v_win_t = _kprep(p, 24, 25, NSA_KV_HEADS, None)
    y_nsa = _flash((p, 0), k_win, v_win_t, bias_w, heads=ATTN_HEADS, group=NSA_GROUP, shared_kv=True,
                   out_dtype=BF16, span=NSA_WINDOW // CHUNK + 1, combine=(o_cmp, o_sel, p, 52))
    y_ret = _retention(p, {"rq": 8, "rk": 16, "rv": 24, "rg": 32})
    return y_nsa, y_ret


def kernel(x, rel_bias, ev_w_in, ev_conv_w, ev_conv_b, ev_dt_bias, ev_a_log, ev_d_skip, ev_norm_w, ev_w_out,
           od_w_in, od_cmp_pe, od_cmp_w1, od_cmp_w2, od_w_out,
           ffn_w_up, ffn_conv_w, ffn_conv_b, ffn_w_down, ln_g, ln_b):
    _, t0, _ = x.shape
    t = -(-t0 // CHUNK) * CHUNK
    h = jnp.pad(x[0], ((0, t - t0), (0, 0)))
    hb = h.astype(BF16)
    bias_c = _bias_tiles(rel_bias, 8)
    bias_w = _bias_tiles(rel_bias, NSA_WINDOW // CHUNK + 1, window=NSA_WINDOW)
    cmp_band = _cmp_band(rel_bias)
    ev_w_in_b = _reorder_cols(ev_w_in, _EVEN_ORDER, _EVEN_WIDTH)
    od_w_in_b = _reorder_cols(od_w_in, _ODD_ORDER, _ODD_WIDTH)
    ev_w_out_b, od_w_out_b = ev_w_out.astype(BF16), od_w_out.astype(BF16)
    w_down_b = ffn_w_down.astype(BF16)
    for layer in range(DEPTH):
        i = layer // 2
        if layer % 2 == 0:
            mix = _even_mixer(hb, ev_w_in_b, i, ev_conv_w[i], ev_conv_b[i], ev_dt_bias[i], ev_a_log[i],
                              ev_d_skip[i], ev_norm_w[i], bias_c)
            w_out_b = ev_w_out_b
        else:
            mix = _odd_mixer(hb, od_w_in_b, i, od_cmp_pe[i], od_cmp_w1[i], od_cmp_w2[i], rel_bias,
                             bias_c, bias_w, cmp_band)
            w_out_b = od_w_out_b
        h, hb = _matmul_res_ln(mix, w_out_b, i, h, ln_g[layer, 0], ln_b[layer, 0], _OUT_TK)
        gated = _ffn_up(hb, ffn_w_up, layer, ffn_conv_w[layer], ffn_conv_b[layer])
        h, hb = _matmul_res_ln((gated,), w_down_b, layer, h, ln_g[layer, 1], ln_b[layer, 1], _DOWN_TK)
    return h[None, :t0]
```

```python
import functools
import math

import jax
import jax.numpy as jnp
from jax import lax
from jax.experimental import pallas as pl
from jax.experimental.pallas import tpu as pltpu

F32 = jnp.float32
BF16 = jnp.bfloat16

D_MODEL = 2048
DEPTH = 4
HEAD_DIM = 128
ATTN_HEADS = 8
REL_BUCKETS = 32
REL_MAX_DISTANCE = 2048
SSD_HEADS = 16
SSD_HEAD_DIM = 64
SSD_INNER = 1024
SSD_GROUPS = 2
SSD_STATE = 128
SSD_CONV = 4
MOBA_BLOCK = 256
MOBA_TOPK = 3
NSA_KV_HEADS = 2
NSA_GROUP = 4
NSA_CMP_BLOCK = 32
NSA_CMP_STRIDE = 16
NSA_SEL_BLOCK = 64
NSA_SEL_TOPN = 16
NSA_WINDOW = 512
RET_HEADS = 8
ROPE_BASE = 10000.0
FFN_DIM = 5632
FFN_CONV = 3
LN_EPS = 1e-5
NEG = -1e30
BIG = 3e38
LOG2E = math.log2(math.e)
_QK_SCALE = HEAD_DIM ** -0.5 * LOG2E
DEEPNORM_ALPHA = (2 * DEPTH) ** 0.25

CHUNK = 256
LANES = 128
VMEM_LIMIT = 56 * 1024 * 1024


def _params(*sem):
    return pltpu.CompilerParams(dimension_semantics=sem, vmem_limit_bytes=VMEM_LIMIT)


def _silu(x):
    return x * (1.0 / (1.0 + jnp.exp(-x)))


def _dot(a, b):
    return jnp.dot(a, b, preferred_element_type=F32)


def _dot_nt(a, b):
    return lax.dot_general(a, b, (((1,), (1,)), ((), ())), preferred_element_type=F32)


def _dot_tn(a, b):
    return lax.dot_general(a, b, (((0,), (0,)), ((), ())), preferred_element_type=F32)


def _split_bf16(a):
    hi = a.astype(BF16)
    return hi, (a - hi.astype(F32)).astype(BF16)


def _dot_nt_split(a, b):
    ah, al = _split_bf16(a)
    bh, bl = _split_bf16(b)
    return _dot_nt(ah, bh) + (_dot_nt(ah, bl) + _dot_nt(al, bh))


def _split3_bf16(a):
    hi = a.astype(BF16)
    mid, lo = _split_bf16(a - hi.astype(F32))
    return hi, mid, lo


def _dot_f32_by_01(a, b01):
    return sum(_dot(t, b01) for t in _split3_bf16(a))


def _dot_01_by_f32(a01, b):
    return sum(_dot(a01, t) for t in _split3_bf16(b))


def _shift_rows(a, carry, k):
    rolled = pltpu.roll(a, k, 0)
    head = jnp.where(lax.broadcasted_iota(jnp.int32, carry.shape, 0) < k,
                     pltpu.roll(carry, k, 0), rolled[:8])
    return jnp.concatenate([head, rolled[8:]], axis=0)


def _mm_kernel(x_ref, w_ref, o_ref):
    o_ref[...] = _dot(x_ref[...], w_ref[...]).astype(o_ref.dtype)


def _matmul(x, w_all, layer, tn, out_dtype=F32):
    m, k = x.shape
    n = w_all.shape[2]
    tm = min(1024, m)
    if (m // tm) * k * n <= (n // tn) * m * k:
        grid = (m // tm, n // tn)
        row, col = (lambda a, b: a), (lambda a, b: b)
    else:
        grid = (n // tn, m // tm)
        row, col = (lambda a, b: b), (lambda a, b: a)
    return pl.pallas_call(
        _mm_kernel,
        out_shape=jax.ShapeDtypeStruct((m, n), out_dtype),
        grid=grid,
        in_specs=[pl.BlockSpec((tm, k), lambda a, b: (row(a, b), 0)),
                  pl.BlockSpec((None, k, tn), lambda a, b: (layer, 0, col(a, b)))],
        out_specs=pl.BlockSpec((tm, tn), lambda a, b: (row(a, b), col(a, b))),
        compiler_params=_params("parallel", "arbitrary"),
        name="proj_matmul",
    )(x, w_all)


def _mm_ln_kernel(*refs, steps):
    a_refs = refs[:len(steps)]
    w_ref, h_ref, g_ref, b_ref, o_ref, ob_ref, acc_ref = refs[len(steps):]
    k = pl.program_id(1)

    @pl.when(k == 0)
    def _():
        acc_ref[...] = _dot(a_refs[0][...], w_ref[...])

    first = 0
    for a_ref, n in zip(a_refs, steps):
        @pl.when((k >= max(first, 1)) & (k < first + n))
        def _(a_ref=a_ref):
            acc_ref[...] += _dot(a_ref[...], w_ref[...])
        first += n

    @pl.when(k == pl.num_programs(1) - 1)
    def _():
        y = DEEPNORM_ALPHA * h_ref[...] + acc_ref[...]
        mu = jnp.mean(y, axis=-1, keepdims=True)
        yc = y - mu
        var = jnp.mean(yc * yc, axis=-1, keepdims=True)
        out = yc * lax.rsqrt(var + LN_EPS) * g_ref[...] + b_ref[...]
        o_ref[...] = out
        ob_ref[...] = out.astype(BF16)


def _matmul_res_ln(parts, w_all, layer, h, g, b, tk):
    m = h.shape[0]
    n = w_all.shape[2]
    tm = min(512, m)
    steps = tuple(a.shape[1] // tk for a in parts)
    starts = tuple(sum(steps[:p]) for p in range(len(parts)))

    def part_spec(p):
        return pl.BlockSpec((tm, tk), lambda i, j: (i, jnp.clip(j - starts[p], 0, steps[p] - 1)))

    return pl.pallas_call(
        functools.partial(_mm_ln_kernel, steps=steps),
        out_shape=(jax.ShapeDtypeStruct((m, n), F32), jax.ShapeDtypeStruct((m, n), BF16)),
        grid=(m // tm, sum(steps)),
        in_specs=[part_spec(p) for p in range(len(parts))] + [
            pl.BlockSpec((None, tk, n), lambda i, j: (layer, j, 0)),
            pl.BlockSpec((tm, n), lambda i, j: (i, 0)),
            pl.BlockSpec((1, n), lambda i, j: (0, 0)),
            pl.BlockSpec((1, n), lambda i, j: (0, 0))],
        out_specs=(pl.BlockSpec((tm, n), lambda i, j: (i, 0)),
                   pl.BlockSpec((tm, n), lambda i, j: (i, 0))),
        scratch_shapes=[pltpu.VMEM((tm, n), F32)],
        compiler_params=_params("parallel", "arbitrary"),
        name="matmul_res_ln",
    )(*parts, w_all, h, g.reshape(1, n), b.reshape(1, n))


def _ffn_up_kernel(x_ref, wa_ref, wu_ref, cw_ref, cb_ref, o_ref, carry_ref, wab_ref, wub_ref):
    @pl.when(pl.program_id(1) == 0)
    def _():
        carry_ref[...] = jnp.zeros_like(carry_ref)
        wab_ref[...] = wa_ref[...].astype(BF16)
        wub_ref[...] = wu_ref[...].astype(BF16)

    x = x_ref[...]
    a = _dot(x, wab_ref[...])
    u = _dot(x, wub_ref[...])
    carry = carry_ref[...]
    cw = cw_ref[...]
    conv = (cw[0:1] * _shift_rows(a, carry, 2) + cw[1:2] * _shift_rows(a, carry, 1)
            + cw[2:3] * a + cb_ref[...])
    carry_ref[...] = a[a.shape[0] - 8:]
    o_ref[...] = (_silu(conv) * u).astype(BF16)


def _ffn_up(x, w_up_all, layer, conv_w, conv_b):
    m, k = x.shape
    tn = 512
    tm = min(1024, m)
    nj = FFN_DIM // tn
    return pl.pallas_call(
        _ffn_up_kernel,
        out_shape=jax.ShapeDtypeStruct((m, FFN_DIM), BF16),
        grid=(nj, m // tm),
        in_specs=[pl.BlockSpec((tm, k), lambda j, i: (i, 0)),
                  pl.BlockSpec((None, k, tn), lambda j, i: (layer, 0, j)),
                  pl.BlockSpec((None, k, tn), lambda j, i: (layer, 0, j + nj)),
                  pl.BlockSpec((FFN_CONV, tn), lambda j, i: (0, j)),
                  pl.BlockSpec((1, tn), lambda j, i: (0, j))],
        out_specs=pl.BlockSpec((tm, tn), lambda j, i: (i, j)),
        scratch_shapes=[pltpu.VMEM((8, tn), F32), pltpu.VMEM((k, tn), BF16), pltpu.VMEM((k, tn), BF16)],
        compiler_params=_params("parallel", "arbitrary"),
        name="ffn_up_conv",
    )(x, w_up_all, w_up_all, conv_w, conv_b.reshape(1, FFN_DIM))


def _ssd_kernel(z_ref, x_ref, bc_ref, dt_ref, dtt_ref, cwx_ref, cbx_ref, cwb_ref, cbb_ref,
                dtb_ref, alog_ref, dtbc_ref, alogc_ref, dskip_ref, nw_ref, expand_ref,
                o_ref, cx_ref, cbc_ref, state_ref):
    L = CHUNK
    hpg = SSD_HEADS // SSD_GROUPS
    gw = hpg * SSD_HEAD_DIM

    @pl.when(pl.program_id(0) == 0)
    def _():
        cx_ref[...] = jnp.zeros_like(cx_ref)
        cbc_ref[...] = jnp.zeros_like(cbc_ref)
        state_ref[...] = jnp.zeros_like(state_ref)

    def conv(raw, carry_ref, w_ref, b_ref):
        carry = carry_ref[...]
        w = w_ref[...]
        y = (w[0:1] * _shift_rows(raw, carry, 3) + w[1:2] * _shift_rows(raw, carry, 2)
             + w[2:3] * _shift_rows(raw, carry, 1) + w[3:4] * raw + b_ref[...])
        carry_ref[...] = raw[L - 8:]
        return _silu(y)

    xs = conv(x_ref[...], cx_ref, cwx_ref, cbx_ref)
    bcs = conv(bc_ref[...], cbc_ref, cwb_ref, cbb_ref)

    def softplus(v):
        return jnp.maximum(v, 0.0) + jnp.log(1.0 + jnp.exp(-jnp.abs(v)))

    dt = softplus(dt_ref[...] + dtb_ref[...])
    da = dt * (-jnp.exp(alog_ref[...]))
    row = lax.broadcasted_iota(jnp.int32, (L, L), 0)
    col = lax.broadcasted_iota(jnp.int32, (L, L), 1)
    causal = row >= col
    a_cum = _dot_01_by_f32(causal.astype(BF16), da)
    dt_r = softplus(dtt_ref[...] + dtbc_ref[...])
    da_r = dt_r * (-jnp.exp(alogc_ref[...]))
    a_cum_r = _dot_f32_by_01(da_r, (row <= col).astype(BF16))

    expand = expand_ref[...]

    def widen(v):
        return _dot_f32_by_01(v, expand)

    a_last = a_cum[L - 1:L]
    dt_w = widen(dt)
    decay_in_w = widen(jnp.exp(a_cum))
    decay_end_w = widen(jnp.exp(a_last - a_cum))
    chunk_decay_w = widen(jnp.broadcast_to(jnp.exp(a_last), (8, LANES)))[0:1]
    xdt = xs * dt_w
    xdt_b = xdt.astype(BF16)
    xdt_end_b = (xdt * decay_end_w).astype(BF16)
    lane_lo = lax.broadcasted_iota(jnp.int32, (L, LANES), 1) < SSD_HEAD_DIM

    ys = []
    for g in range(SSD_GROUPS):
        b_g = bcs[:, g * SSD_STATE:(g + 1) * SSD_STATE].astype(BF16)
        c_g = bcs[:, (SSD_GROUPS + g) * SSD_STATE:(SSD_GROUPS + g + 1) * SSD_STATE].astype(BF16)
        cb = _dot_nt(c_g, b_g)
        pieces = []
        for pair in range(hpg // 2):
            outs = []
            for k in range(2):
                h = g * hpg + pair * 2 + k
                diff = a_cum[:, h:h + 1] - a_cum_r[h:h + 1, :]
                w = (cb * jnp.exp(jnp.where(causal, diff, NEG))).astype(BF16)
                lo = (g * hpg + pair * 2) * SSD_HEAD_DIM
                outs.append(_dot(w, xdt_b[:, lo:lo + LANES]))
            pieces.append(jnp.where(lane_lo, outs[0], outs[1]))
        y_diag = jnp.concatenate(pieces, axis=1)
        state = state_ref[g]
        sl = slice(g * gw, (g + 1) * gw)
        y_off = _dot(c_g, state.astype(BF16)) * decay_in_w[:, sl]
        state_ref[g] = state * chunk_decay_w[:, sl] + _dot_tn(b_g, xdt_end_b[:, sl])
        ys.append(y_diag + y_off)
    y = jnp.concatenate(ys, axis=1) + xs * dskip_ref[...]
    yg = y * _silu(z_ref[...])
    half = SSD_INNER // SSD_GROUPS
    outs = []
    for g in range(SSD_GROUPS):
        part = yg[:, g * half:(g + 1) * half]
        outs.append(part * lax.rsqrt(jnp.mean(part * part, axis=-1, keepdims=True) + LN_EPS))
    o_ref[...] = (jnp.concatenate(outs, axis=1) * nw_ref[...]).astype(BF16)


def _ssd(p, dt_t, conv_w, conv_b, dt_bias, a_log, d_skip, norm_w, col):
    t = p.shape[0]
    L = CHUNK
    pad16 = lambda v: jnp.pad(v.astype(F32), (0, LANES - SSD_HEADS)).reshape(1, LANES)
    expand = (jnp.arange(LANES)[:, None] == (jnp.arange(SSD_INNER) // SSD_HEAD_DIM)[None, :]).astype(BF16)
    full = lambda shape: pl.BlockSpec(shape, lambda c: (0,) * len(shape))
    return pl.pallas_call(
        _ssd_kernel,
        out_shape=jax.ShapeDtypeStruct((t, SSD_INNER), BF16),
        grid=(t // L,),
        in_specs=[pl.BlockSpec((L, 1024), lambda c: (c, col["z"])),
                  pl.BlockSpec((L, 1024), lambda c: (c, col["x"])),
                  pl.BlockSpec((L, 512), lambda c: (c, col["bc"])),
                  pl.BlockSpec((L, LANES), lambda c: (c, col["dt"])),
                  pl.BlockSpec((SSD_HEADS, L), lambda c: (0, c)),
                  full((SSD_CONV, 1024)), full((1, 1024)), full((SSD_CONV, 512)), full((1, 512)),
                  full((1, LANES)), full((1, LANES)), full((SSD_HEADS, 1)), full((SSD_HEADS, 1)),
                  full((1, 1024)), full((1, 1024)), full((LANES, 1024))],
        out_specs=pl.BlockSpec((L, SSD_INNER), lambda c: (c, 0)),
        scratch_shapes=[pltpu.VMEM((8, 1024), F32), pltpu.VMEM((8, 512), F32),
                        pltpu.VMEM((SSD_GROUPS, SSD_STATE, 512), F32)],
        compiler_params=_params("arbitrary"),
        name="ssd_mixer",
    )(p, p, p, p, dt_t,
      conv_w[:, :SSD_INNER], conv_b[:SSD_INNER].reshape(1, -1),
      conv_w[:, SSD_INNER:], conv_b[SSD_INNER:].reshape(1, -1),
      pad16(dt_bias), pad16(a_log), dt_bias.reshape(-1, 1), a_log.reshape(-1, 1),
      jnp.repeat(d_skip, SSD_HEAD_DIM).reshape(1, -1), norm_w.reshape(1, -1), expand)


_RET_GROUP = 4


def _ret_kernel(logg_ref, q_ref, k_ref, v_ref, g_ref, cos_ref, sin_ref, o_ref,
                state_ref, decay_ref, qdec_ref, kdec_ref, cdec_ref):
    C = CHUNK
    hg = pl.program_id(0)
    head_cols = lambda g: slice(g * HEAD_DIM, (g + 1) * HEAD_DIM)

    @pl.when(pl.program_id(1) == 0)
    def _():
        state_ref[...] = jnp.zeros_like(state_ref)
        rel = lax.broadcasted_iota(jnp.int32, (C, C), 0) - lax.broadcasted_iota(jnp.int32, (C, C), 1)
        pos = lax.broadcasted_iota(jnp.int32, (C, 1), 0).astype(F32)
        for g in range(_RET_GROUP):
            lg = logg_ref[hg * _RET_GROUP + g]
            decay_ref[g] = jnp.where(rel >= 0, jnp.exp(lg * jnp.maximum(rel, 0).astype(F32)), 0.0)
            qdec_ref[g] = jnp.exp(lg * (pos + 1.0))
            kdec_ref[g] = jnp.exp(lg * (C - 1.0 - pos))
            cdec_ref[g] = jnp.exp(lg * jnp.full((1, HEAD_DIM), float(C), F32))

    cos = cos_ref[...]
    sin = sin_ref[...]
    even_lane = lax.broadcasted_iota(jnp.int32, (C, HEAD_DIM), 1) % 2 == 0

    def rot(x):
        swapped = jnp.where(even_lane, pltpu.roll(x, HEAD_DIM - 1, 1), pltpu.roll(x, 1, 1))
        return x * cos + swapped * sin

    qs = [rot(q_ref[:, head_cols(g)]) for g in range(_RET_GROUP)]
    ks = [rot(k_ref[:, head_cols(g)]) * (HEAD_DIM ** -0.5) for g in range(_RET_GROUP)]
    vs = [v_ref[:, head_cols(g)].astype(BF16) for g in range(_RET_GROUP)]
    inners = [_dot_nt(qs[g].astype(BF16), ks[g].astype(BF16)) for g in range(_RET_GROUP)]
    for g in range(_RET_GROUP):
        state = state_ref[g]
        y = _dot((inners[g] * decay_ref[g]).astype(BF16), vs[g])
        y = y + _dot((qs[g] * qdec_ref[g]).astype(BF16), state.astype(BF16))
        state_ref[g] = state * cdec_ref[g] + _dot_tn((ks[g] * kdec_ref[g]).astype(BF16), vs[g])
        mu = jnp.mean(y, axis=-1, keepdims=True)
        yc = y - mu
        var = jnp.mean(yc * yc, axis=-1, keepdims=True)
        o_ref[:, head_cols(g)] = (_silu(g_ref[:, head_cols(g)]) * (yc * lax.rsqrt(var + LN_EPS))).astype(BF16)


def _retention(p, col):
    t = p.shape[0]
    C = CHUNK
    G = _RET_GROUP
    inv = 1.0 / (ROPE_BASE ** (jnp.arange(0, HEAD_DIM, 2, dtype=F32) / HEAD_DIM))
    ang = jnp.arange(t, dtype=F32)[:, None] * inv[None, :]
    cos = jnp.repeat(jnp.cos(ang), 2, axis=1)
    sin = jnp.stack([-jnp.sin(ang), jnp.sin(ang)], axis=-1).reshape(t, HEAD_DIM)
    log_g = jnp.log(1.0 - 2.0 ** (-5.0 - jnp.arange(RET_HEADS, dtype=F32)))
    blk = lambda name: pl.BlockSpec((C, G * HEAD_DIM), lambda h, c, lg: (c, col[name] // G + h))
    tab = pl.BlockSpec((C, HEAD_DIM), lambda h, c, lg: (c, 0))
    return pl.pallas_call(
        _ret_kernel,
        out_shape=jax.ShapeDtypeStruct((t, RET_HEADS * HEAD_DIM), BF16),
        grid_spec=pltpu.PrefetchScalarGridSpec(
            num_scalar_prefetch=1,
            grid=(RET_HEADS // G, t // C),
            in_specs=[blk("rq"), blk("rk"), blk("rv"), blk("rg"), tab, tab],
            out_specs=pl.BlockSpec((C, G * HEAD_DIM), lambda h, c, lg: (c, h)),
            scratch_shapes=[pltpu.VMEM((G, HEAD_DIM, HEAD_DIM), F32), pltpu.VMEM((G, C, C), F32),
                            pltpu.VMEM((G, C, 1), F32), pltpu.VMEM((G, C, 1), F32),
                            pltpu.VMEM((G, 1, HEAD_DIM), F32)]),
        compiler_params=_params("parallel", "arbitrary"),
        name="retention_mixer",
    )(log_g, p, p, p, p, cos, sin)


def _rel_bucket(dist):
    n = jnp.maximum(dist, 0)
    max_exact = REL_BUCKETS // 2
    nf = jnp.maximum(n, max_exact).astype(F32)
    large = max_exact + (jnp.log(nf / max_exact) / math.log(REL_MAX_DISTANCE / max_exact)
                         * (REL_BUCKETS - max_exact)).astype(jnp.int32)
    large = jnp.minimum(large, REL_BUCKETS - 1)
    return jnp.where(n < max_exact, n, large)


def _bias_lookup(tab_ref, bucket, h):
    rows, cols = bucket.shape
    lane = lax.broadcasted_iota(jnp.int32, (1, LANES), 1)
    tab_row = jnp.zeros((1, LANES), F32)
    for b in range(REL_BUCKETS):
        tab_row = jnp.where(lane == b, tab_ref[b * ATTN_HEADS + h], tab_row)
    table = jnp.broadcast_to(tab_row, (rows, LANES))
    return jnp.concatenate([jnp.take_along_axis(table, bucket[:, c:c + LANES], axis=1)
                            for c in range(0, cols, LANES)], axis=1)


def _bias_tile_kernel(tab_ref, o_ref, *, window):
    h = pl.program_id(0)
    o = pl.program_id(1)
    row = lax.broadcasted_iota(jnp.int32, (CHUNK, CHUNK), 0)
    col = lax.broadcasted_iota(jnp.int32, (CHUNK, CHUNK), 1)
    d = o * CHUNK + col - row
    valid = d >= 0
    if window is not None:
        valid = valid & (d < window)
    o_ref[0, 0] = jnp.where(valid, _bias_lookup(tab_ref, _rel_bucket(d), h) * LOG2E, NEG)


def _bias_tiles(rel_bias, n_off, window=None):
    return pl.pallas_call(
        functools.partial(_bias_tile_kernel, window=window),
        out_shape=jax.ShapeDtypeStruct((ATTN_HEADS, n_off, CHUNK, CHUNK), F32),
        grid_spec=pltpu.PrefetchScalarGridSpec(
            num_scalar_prefetch=1, grid=(ATTN_HEADS, n_off), in_specs=[],
            out_specs=pl.BlockSpec((1, 1, CHUNK, CHUNK), lambda h, o, tab: (h, o, 0, 0))),
        compiler_params=_params("parallel", "parallel"),
        name="rel_bias_tiles",
    )(rel_bias.reshape(-1))


_CMP_BAND_PERIOD = LANES * NSA_CMP_STRIDE // CHUNK


def _cmp_band_kernel(tab_ref, o_ref):
    h = pl.program_id(0)
    b = pl.program_id(1)
    row = lax.broadcasted_iota(jnp.int32, (CHUNK, 2 * LANES), 0)
    col = lax.broadcasted_iota(jnp.int32, (CHUNK, 2 * LANES), 1)
    d = b * CHUNK + row - NSA_CMP_STRIDE * (col - LANES) - (NSA_CMP_BLOCK - 1)
    o_ref[0, 0] = jnp.where(d >= 0, _bias_lookup(tab_ref, _rel_bucket(d), h), NEG)


def _cmp_band(rel_bias):
    return pl.pallas_call(
        _cmp_band_kernel,
        out_shape=jax.ShapeDtypeStruct((ATTN_HEADS, _CMP_BAND_PERIOD, CHUNK, 2 * LANES), F32),
        grid_spec=pltpu.PrefetchScalarGridSpec(
            num_scalar_prefetch=1, grid=(ATTN_HEADS, _CMP_BAND_PERIOD), in_specs=[],
            out_specs=pl.BlockSpec((1, 1, CHUNK, 2 * LANES), lambda h, b, tab: (h, b, 0, 0))),
        compiler_params=_params("parallel", "parallel"),
        name="cmp_bias_band",
    )(rel_bias.reshape(-1))


def _kprep_kernel(k_ref, v_ref, ka_ref, vt_ref, *rest, heads, block, with_mean):
    i = pl.program_id(0)
    k = k_ref[...]
    v = v_ref[...]
    if block is not None:
        row = lax.broadcasted_iota(jnp.int32, (CHUNK, LANES), 0)
        lane = lax.broadcasted_iota(jnp.int32, (CHUNK, LANES), 1)
        onehot = (lane == (i * CHUNK + row) // block).astype(BF16)
    for h in range(heads):
        sl = slice(h * HEAD_DIM, (h + 1) * HEAD_DIM)
        kh = k[:, sl].astype(BF16)
        ka_ref[h, 0] = kh if block is None else jnp.concatenate([kh, onehot], axis=1)
        vt_ref[h, 0] = v[:, sl].T.astype(BF16)
    if with_mean:
        km_ref = rest[0]

        @pl.when(i == 0)
        def _():
            km_ref[...] = jnp.zeros_like(km_ref)

        km_ref[pl.ds(i, 1), :] = jnp.mean(k, axis=0, keepdims=True)


def _kprep(p, kcol, vcol, heads, block, with_mean=False):
    t = p.shape[0]
    nt = t // CHUNK
    w = heads * HEAD_DIM
    kw = HEAD_DIM if block is None else 2 * HEAD_DIM
    out_shape = [jax.ShapeDtypeStruct((heads, nt, CHUNK, kw), BF16),
                 jax.ShapeDtypeStruct((heads, nt, HEAD_DIM, CHUNK), BF16)]
    out_specs = [pl.BlockSpec((heads, 1, CHUNK, kw), lambda i: (0, i, 0, 0)),
                 pl.BlockSpec((heads, 1, HEAD_DIM, CHUNK), lambda i: (0, i, 0, 0))]
    if with_mean:
        out_shape.append(jax.ShapeDtypeStruct((LANES, w), F32))
        out_specs.append(pl.BlockSpec((LANES, w), lambda i: (0, 0)))
    return pl.pallas_call(
        functools.partial(_kprep_kernel, heads=heads, block=block, with_mean=with_mean),
        out_shape=tuple(out_shape),
        grid=(t // CHUNK,),
        in_specs=[pl.BlockSpec((CHUNK, w), lambda i: (i, kcol)),
                  pl.BlockSpec((CHUNK, w), lambda i: (i, vcol))],
        out_specs=tuple(out_specs),
        compiler_params=_params("arbitrary"),
        name="key_prep",
    )(p, p)


def _topk_mask_t(score_t, k):
    row = lax.broadcasted_iota(jnp.int32, score_t.shape, 0)

    def body(_, g):
        m = jnp.max(g, axis=0, keepdims=True)
        idx = jnp.min(jnp.where(g == m, row, score_t.shape[0]), axis=0, keepdims=True)
        return jnp.where(row == idx, -BIG, g)

    left = lax.fori_loop(0, k, body, score_t)
    return (left < score_t) & (score_t > -0.5 * BIG)


def _moba_select_kernel(q_ref, km_ref, qa_ref):
    i = pl.program_id(0)
    q = q_ref[...]
    km = km_ref[...]
    block = lax.broadcasted_iota(jnp.int32, (LANES, CHUNK), 0)
    head_cols = lambda h: slice(h * HEAD_DIM, (h + 1) * HEAD_DIM)
    gates = [jnp.where(block < i, _dot_nt_split(km[:, head_cols(h)], q[:, head_cols(h)]), -BIG)
             for h in range(ATTN_HEADS)]
    sel = _topk_mask_t(jnp.concatenate(gates, axis=1), MOBA_TOPK)
    own = block == i
    for h in range(ATTN_HEADS):
        pen_t = jnp.where(sel[:, h * CHUNK:(h + 1) * CHUNK] | own, 0.0, NEG)
        q_t = (q[:, head_cols(h)] * _QK_SCALE).T
        qa_ref[h, 0] = jnp.concatenate([q_t, pen_t], axis=0).astype(BF16)


def _moba_select(p, qcol, kmean):
    t = p.shape[0]
    w = ATTN_HEADS * HEAD_DIM
    return pl.pallas_call(
        _moba_select_kernel,
        out_shape=jax.ShapeDtypeStruct((ATTN_HEADS, t // CHUNK, 2 * HEAD_DIM, CHUNK), BF16),
        grid=(t // CHUNK,),
        in_specs=[pl.BlockSpec((CHUNK, w), lambda i: (i, qcol)),
                  pl.BlockSpec((LANES, w), lambda i: (0, 0))],
        out_specs=pl.BlockSpec((ATTN_HEADS, 1, 2 * HEAD_DIM, CHUNK), lambda i: (0, i, 0, 0)),
        compiler_params=_params("parallel"),
        name="moba_select",
    )(p, kmean)


def _flash_kernel(q_ref, k_ref, vt_ref, bias_ref, *rest, group, shared_kv, span, combine):
    if combine:
        ocmp_ref, osel_ref, gate_ref, o_ref = rest[:4]
    else:
        o_ref = rest[0]
    m_ref, l_ref, acc_ref, alpha_ref, pt_ref, s_even_ref, s_odd_ref = rest[-7:]
    hg = pl.program_id(0)
    i = pl.program_id(1)
    n_off = bias_ref.shape[1]
    nt = k_ref.shape[-3]
    head_cols = lambda g: slice(g * HEAD_DIM, (g + 1) * HEAD_DIM)
    if len(q_ref.shape) == 3:
        qts = [q_ref[g] for g in range(group)]
    else:
        qts = [(q_ref[:, head_cols(g)] * _QK_SCALE).T.astype(BF16) for g in range(group)]
    m_ref[...] = jnp.full_like(m_ref, -BIG)
    l_ref[...] = jnp.zeros_like(l_ref)
    acc_ref[...] = jnp.zeros_like(acc_ref)
    k_tile = lambda g, j: k_ref[j] if shared_kv else k_ref[g, j]
    vt_tile = lambda g, j: vt_ref[j] if shared_kv else vt_ref[g, j]
    alpha_ref[...] = jnp.ones_like(alpha_ref)
    pt_ref[...] = jnp.zeros_like(pt_ref)

    def scores(j, s_ref):
        for g in range(group):
            s_ref[g] = _dot(k_tile(g, j), qts[g])

    def value_update(j):
        for g in range(group):
            acc_ref[g] = alpha_ref[g] * acc_ref[g] + _dot(vt_tile(g, j), pt_ref[g])

    def step(j, s_ref, s_next_ref, const_bias):
        scores(jnp.minimum(j + 1, nt - 1), s_next_ref)
        value_update(jnp.maximum(j - 1, 0))
        for g in range(group):
            m_prev = m_ref[g]
            if const_bias:
                far = bias_ref[g, n_off - 1, 0:1, :]
                st = s_ref[g]
                m_new = jnp.maximum(m_prev, jnp.max(st, axis=0, keepdims=True) + far)
                pt = jnp.exp2(st - (m_new - far))
            else:
                st = s_ref[g] + bias_ref[g, i - j]
                m_new = jnp.maximum(m_prev, jnp.max(st, axis=0, keepdims=True))
                pt = jnp.exp2(st - m_new)
            alpha = jnp.exp2(m_prev - m_new)
            l_ref[g] = alpha * l_ref[g] + jnp.sum(pt, axis=0, keepdims=True)
            m_ref[g] = m_new
            alpha_ref[g] = alpha
            pt_ref[g] = pt.astype(BF16)

    def pairs(start, count, const_bias):
        def body(t, carry):
            j = start + 2 * t
            step(j, s_even_ref, s_odd_ref, const_bias)
            step(j + 1, s_odd_ref, s_even_ref, const_bias)
            return carry
        lax.fori_loop(0, count, body, 0)

    if span is None:
        n_far = jnp.maximum(i - (n_off - 2), 0) // 2 * 2
        j_lo = 0
    else:
        n_far = 0
        j_lo = jnp.maximum(i - (span - 1), 0)
    scores(j_lo, s_even_ref)
    if span is None:
        pairs(j_lo, n_far // 2, True)
    near_lo = j_lo + n_far
    n_near = i + 1 - near_lo
    pairs(near_lo, n_near // 2, False)

    @pl.when(n_near % 2 == 1)
    def _():
        step(i, s_even_ref, s_odd_ref, False)

    value_update(i)
    if combine:
        sig = 1.0 / (1.0 + jnp.exp(-gate_ref[...]))
        lane = lax.broadcasted_iota(jnp.int32, sig.shape, 1)
    for g in range(group):
        out = (acc_ref[g] * (1.0 / l_ref[g])).T
        if combine:
            h = hg * group + g
            pick = lambda c: jnp.sum(jnp.where(lane == 3 * h + c, sig, 0.0), axis=1, keepdims=True)
            out = pick(0) * ocmp_ref[:, head_cols(g)] + pick(1) * osel_ref[:, head_cols(g)] + pick(2) * out
        o_ref[:, head_cols(g)] = out.astype(o_ref.dtype)


def _flash(q, k, vt, bias, *, heads, group, shared_kv, out_dtype, span=None, combine=None):
    n_off = bias.shape[1]
    nt = k.shape[1]
    gw = group * HEAD_DIM
    if isinstance(q, tuple):
        q, base = q
        q_spec = pl.BlockSpec((CHUNK, gw), lambda hg, i: (i, base // group + hg))
    else:
        q_spec = pl.BlockSpec((group, None) + q.shape[2:], lambda hg, i: (hg, i, 0, 0))
    kv_block = (None,) if shared_kv else (group,)
    resident = lambda shape: pl.BlockSpec(shape, lambda hg, i: (hg, 0, 0, 0), pipeline_mode=pl.Buffered(1))
    in_specs = [q_spec, resident(kv_block + k.shape[1:]), resident(kv_block + vt.shape[1:]),
                resident((group, n_off, CHUNK, CHUNK))]
    args = [q, k, vt, bias]
    tile = pl.BlockSpec((CHUNK, gw), lambda hg, i: (i, hg))
    if combine is not None:
        ocmp, osel, p, gcol = combine
        in_specs += [tile, tile, pl.BlockSpec((CHUNK, LANES), lambda hg, i: (i, gcol))]
        args += [ocmp, osel, p]
    return pl.pallas_call(
        functools.partial(_flash_kernel, group=group, shared_kv=shared_kv, span=span,
                          combine=combine is not None),
        out_shape=jax.ShapeDtypeStruct((nt * CHUNK, heads * HEAD_DIM), out_dtype),
        grid=(heads // group, nt),
        in_specs=in_specs,
        out_specs=tile,
        scratch_shapes=[pltpu.VMEM((group, 1, CHUNK), F32),
                        pltpu.VMEM((group, 1, CHUNK), F32),
                        pltpu.VMEM((group, HEAD_DIM, CHUNK), F32), pltpu.VMEM((group, 1, CHUNK), F32),
                        pltpu.VMEM((group, CHUNK, CHUNK), BF16),
                        pltpu.VMEM((group, CHUNK, CHUNK), F32), pltpu.VMEM((group, CHUNK, CHUNK), F32)],
        compiler_params=_params("parallel", "arbitrary"),
        name="flash_attention",
    )(*args)


def _compress_kernel(x_ref, pe_ref, w1_ref, w2_ref, o_ref):
    x = x_ref[0, 0]
    pe = pe_ref[0]
    half = NSA_CMP_STRIDE * HEAD_DIM
    a = _dot((x + pe[0:1]).astype(BF16), w1_ref[0, :half, :])
    b = _dot((x + pe[1:2]).astype(BF16), w1_ref[0, half:, :])
    hid = a + pltpu.roll(b, x.shape[0] - 1, 0)
    gelu = 0.5 * hid * (1.0 + jnp.tanh(math.sqrt(2.0 / math.pi) * (hid + 0.044715 * hid * hid * hid)))
    o_ref[0, 0] = _dot(gelu.astype(BF16), w2_ref[0])


def _compress(x2, pe, w1, w2):
    _, hkv, w, kd = x2.shape
    return pl.pallas_call(
        _compress_kernel,
        out_shape=jax.ShapeDtypeStruct((2, hkv, w, HEAD_DIM), F32),
        grid=(2, hkv),
        in_specs=[pl.BlockSpec((1, 1, w, kd), lambda a, h: (a, h, 0, 0)),
                  pl.BlockSpec((1, 2, kd), lambda a, h: (a, 0, 0)),
                  pl.BlockSpec((1, 2 * kd, HEAD_DIM), lambda a, h: (a, 0, 0)),
                  pl.BlockSpec((1, HEAD_DIM, HEAD_DIM), lambda a, h: (a, 0, 0))],
        out_specs=pl.BlockSpec((1, 1, w, HEAD_DIM), lambda a, h: (a, h, 0, 0)),
        compiler_params=_params("parallel", "parallel"),
        name="nsa_compress",
    )(x2, pe, w1, w2)


def _nsa_cmp_kernel(tab_ref, q_ref, kv_ref, band_ref, ov_ref, ocmp_ref, qa_ref, imp_ref):
    i = pl.program_id(0)
    width = kv_ref.shape[2]
    scale = HEAD_DIM ** -0.5
    q = q_ref[...]
    block = lax.broadcasted_iota(jnp.int32, (LANES, CHUNK), 0)
    cur = (i * CHUNK + lax.broadcasted_iota(jnp.int32, (LANES, CHUNK), 1)) // NSA_SEL_BLOCK
    forced = (block == 0) | (block == cur) | (block == cur - 1)
    diag = i // _CMP_BAND_PERIOD
    head_cols = lambda h: slice(h * HEAD_DIM, (h + 1) * HEAD_DIM)

    def attend(n_tiles):
        w = n_tiles * LANES
        ov_t = ov_ref[:, :w]
        for kvh in range(NSA_KV_HEADS):
            kc = kv_ref[0, kvh, :w, :].astype(BF16)
            vc = kv_ref[1, kvh, :w, :].astype(BF16)
            psum = jnp.zeros((CHUNK, w), F32)
            for g in range(NSA_GROUP):
                h = kvh * NSA_GROUP + g
                far = tab_ref[(REL_BUCKETS - 1) * ATTN_HEADS + h]
                band = band_ref[h]
                tiles = [jnp.full((CHUNK, LANES), far, F32)] * (n_tiles - 2)
                tiles += [band[:, :LANES], band[:, LANES:]][-min(n_tiles, 2):]
                s = _dot_nt(q[:, head_cols(h)].astype(BF16), kc) * scale + jnp.concatenate(tiles, axis=1)
                m = jnp.max(s, axis=1, keepdims=True)
                e = jnp.exp(s - m)
                l = jnp.sum(e, axis=1, keepdims=True)
                pr = e * jnp.where(m > 0.1 * NEG, 1.0 / l, 0.0)
                ocmp_ref[:, head_cols(h)] = _dot(pr.astype(BF16), vc)
                psum = psum + pr
            hi, lo = _split_bf16(psum)
            imp_ref[kvh] = _dot_nt(ov_t, hi) + _dot_nt(ov_t, lo)

    for n_tiles in range(1, width // LANES + 1):
        pl.when(diag == n_tiles - 1)(functools.partial(attend, n_tiles))

    scores = [jnp.where(forced, BIG, jnp.where(block <= cur, imp_ref[kvh], -BIG)) for kvh in range(NSA_KV_HEADS)]
    sel = _topk_mask_t(jnp.concatenate(scores, axis=1), NSA_SEL_TOPN)
    for kvh in range(NSA_KV_HEADS):
        pen_t = jnp.where(sel[:, kvh * CHUNK:(kvh + 1) * CHUNK], 0.0, NEG).astype(BF16)
        for g in range(NSA_GROUP):
            h = kvh * NSA_GROUP + g
            qa_ref[h, 0] = jnp.concatenate([(q[:, head_cols(h)] * _QK_SCALE).T.astype(BF16), pen_t], axis=0)


def _nsa_cmp(p, qcol, kv_cmp, band, rel_bias):
    t = p.shape[0]
    width = kv_cmp.shape[2]
    w = ATTN_HEADS * HEAD_DIM
    n = jnp.arange(width)[None, :] * NSA_CMP_STRIDE
    j = jnp.arange(LANES)[:, None] * NSA_SEL_BLOCK
    overlap = ((n < j + NSA_SEL_BLOCK) & (n + NSA_CMP_BLOCK > j)
               & (jnp.arange(width)[None, :] < width - 1)).astype(BF16)
    return pl.pallas_call(
        _nsa_cmp_kernel,
        out_shape=(jax.ShapeDtypeStruct((t, w), F32),
                   jax.ShapeDtypeStruct((ATTN_HEADS, t // CHUNK, 2 * HEAD_DIM, CHUNK), BF16)),
        grid_spec=pltpu.PrefetchScalarGridSpec(
            num_scalar_prefetch=1,
            grid=(t // CHUNK,),
            in_specs=[pl.BlockSpec((CHUNK, w), lambda i, tab: (i, qcol)),
                      pl.BlockSpec(kv_cmp.shape, lambda i, tab: (0, 0, 0, 0)),
                      pl.BlockSpec((ATTN_HEADS, None, CHUNK, 2 * LANES),
                                   lambda i, tab: (0, i % _CMP_BAND_PERIOD, 0, 0)),
                      pl.BlockSpec((LANES, width), lambda i, tab: (0, 0))],
            out_specs=(pl.BlockSpec((CHUNK, w), lambda i, tab: (i, 0)),
                       pl.BlockSpec((ATTN_HEADS, 1, 2 * HEAD_DIM, CHUNK), lambda i, tab: (0, i, 0, 0))),
            scratch_shapes=[pltpu.VMEM((NSA_KV_HEADS, LANES, CHUNK), F32)]),
        compiler_params=_params("parallel"),
        name="nsa_cmp_select",
    )(rel_bias.reshape(-1), p, kv_cmp, band, overlap)


_EVEN_ORDER = ((0, 2048), (2576, 5648), (2048, 2576))
_EVEN_WIDTH = 5760
_EVEN_TN = 1920
_OUT_TK = 1024
_DOWN_TK = 1408
_ODD_ORDER = ((0, 1024), (2584, 6680), (1024, 2584))
_ODD_WIDTH = 6912
_ODD_TN = 768


def _reorder_kernel(w_ref, o_ref, *, order):
    w = w_ref[...]
    parts = [w[:, a:b] for a, b in order]
    pad = o_ref.shape[1] - sum(b - a for a, b in order)
    parts.append(jnp.zeros((w.shape[0], pad), w.dtype))
    o_ref[...] = jnp.concatenate(parts, axis=1).astype(BF16)


def _reorder_cols(w, order, width):
    layers, d, n = w.shape
    rows = 256
    return pl.pallas_call(
        functools.partial(_reorder_kernel, order=order),
        out_shape=jax.ShapeDtypeStruct((layers, d, width), BF16),
        grid=(layers, d // rows),
        in_specs=[pl.BlockSpec((None, rows, n), lambda l, r: (l, r, 0))],
        out_specs=pl.BlockSpec((None, rows, width), lambda l, r: (l, r, 0)),
        compiler_params=_params("parallel", "parallel"),
        name="reorder_weight_cols",
    )(w)


def _even_mixer(hb, w_in_all, layer, conv_w, conv_b, dt_bias, a_log, d_skip, norm_w, bias_c):
    p = _matmul(hb, w_in_all, layer, _EVEN_TN)
    dt_t = p[:, 5632:5632 + SSD_HEADS].T
    y_ssd = _ssd(p, dt_t, conv_w, conv_b, dt_bias, a_log, d_skip, norm_w,
                 {"z": 0, "x": 1, "bc": 10, "dt": 44})
    k_aug, v_t, kmean = _kprep(p, 3, 4, ATTN_HEADS, MOBA_BLOCK, True)
    q_aug = _moba_select(p, 2, kmean)
    y_moba = _flash(q_aug, k_aug, v_t, bias_c, heads=ATTN_HEADS, group=4, shared_kv=False, out_dtype=BF16)
    return y_ssd, y_moba


def _odd_mixer(hb, w_in_all, layer, cmp_pe, cmp_w1, cmp_w2, rel_bias, bias_c, bias_w, cmp_band):
    p = _matmul(hb, w_in_all, layer, _ODD_TN)
    t = p.shape[0]
    x2 = p[:, 5120:5632].reshape(t, 2, NSA_KV_HEADS, HEAD_DIM).transpose(1, 2, 0, 3)
    x2 = x2.reshape(2, NSA_KV_HEADS, t // NSA_CMP_STRIDE, NSA_CMP_STRIDE * HEAD_DIM)
    kv_cmp = _compress(x2, cmp_pe.reshape(2, 2, NSA_CMP_STRIDE * HEAD_DIM),
                       cmp_w1.astype(BF16), cmp_w2.astype(BF16))
    o_cmp, q_aug = _nsa_cmp(p, 0, kv_cmp, cmp_band, rel_bias)
    k_aug, v_t = _kprep(p, 22, 23, NSA_KV_HEADS, NSA_SEL_BLOCK)
    o_sel = _flash(q_aug, k_aug, v_t, bias_c, heads=ATTN_HEADS, group=NSA_GROUP, shared_kv=True, out_dtype=F32)
    k_win, v_win_t = _kprep(p, 24, 25, NSA_KV_HEADS, None)
    y_nsa = _flash((p, 0), k_win, v_win_t, bias_w, heads=ATTN_HEADS, group=NSA_GROUP, shared_kv=True,
                   out_dtype=BF16, span=NSA_WINDOW // CHUNK + 1, combine=(o_cmp, o_sel, p, 52))
    y_ret = _retention(p, {"rq": 8, "rk": 16, "rv": 24, "rg": 32})
    return y_nsa, y_ret


def kernel(x, rel_bias, ev_w_in, ev_conv_w, ev_conv_b, ev_dt_bias, ev_a_log, ev_d_skip, ev_norm_w, ev_w_out,
           od_w_in, od_cmp_pe, od_cmp_w1, od_cmp_w2, od_w_out,
           ffn_w_up, ffn_conv_w, ffn_conv_b, ffn_w_down, ln_g, ln_b):
    _, t0, _ = x.shape
    t = -(-t0 // CHUNK) * CHUNK
    h = jnp.pad(x[0], ((0, t - t0), (0, 0)))
    hb = h.astype(BF16)
    bias_c = _bias_tiles(rel_bias, 8)
    bias_w = _bias_tiles(rel_bias, NSA_WINDOW // CHUNK + 1, window=NSA_WINDOW)
    cmp_band = _cmp_band(rel_bias)
    ev_w_in_b = _reorder_cols(ev_w_in, _EVEN_ORDER, _EVEN_WIDTH)
    od_w_in_b = _reorder_cols(od_w_in, _ODD_ORDER, _ODD_WIDTH)
    ev_w_out_b, od_w_out_b = ev_w_out.astype(BF16), od_w_out.astype(BF16)
    w_down_b = ffn_w_down.astype(BF16)
    for layer in range(DEPTH):
        i = layer // 2
        if layer % 2 == 0:
            mix = _even_mixer(hb, ev_w_in_b, i, ev_conv_w[i], ev_conv_b[i], ev_dt_bias[i], ev_a_log[i],
                              ev_d_skip[i], ev_norm_w[i], bias_c)
            w_out_b = ev_w_out_b
        else:
            mix = _odd_mixer(hb, od_w_in_b, i, od_cmp_pe[i], od_cmp_w1[i], od_cmp_w2[i], rel_bias,
                             bias_c, bias_w, cmp_band)
            w_out_b = od_w_out_b
        h, hb = _matmul_res_ln(mix, w_out_b, i, h, ln_g[layer, 0], ln_b[layer, 0], _OUT_TK)
        gated = _ffn_up(hb, ffn_w_up, layer, ffn_conv_w[layer], ffn_conv_b[layer])
        h, hb = _matmul_res_ln((gated,), w_down_b, layer, h, ln_g[layer, 1], ln_b[layer, 1], _DOWN_TK)
    return h[None, :t0]
```
